```python
import jax, jax.numpy as jnp
from jax import lax
import numpy as np

D_MODEL = 1024
BATCH = 32
SEQ = 256
DEPTH = 4
DEC_BATCH = 2
DEC_SEQ = 2048
PAST_LEN = 256

GRID_W = 64
N_MIXERS = 3
CHUNK = 128
D_GMLP = 2 * D_MODEL
N_GROUPS_A = 8
POOL_WINDOWS = (2, 4, 8, 16)
N_POOL_GROUPS = 4
D_POOL_GROUP = D_MODEL // N_POOL_GROUPS
N_HEADS = 16
N_KV_HEADS = 4
HEAD_DIM = 64
Q_PER_KV = N_HEADS // N_KV_HEADS
WINDOW = 128
BLOCK = 128
ROPE_THETA = 10000.0
D_FF = 2816
CONV_W = 3
N_A_LAYERS = (DEPTH + 2) // 3
N_B_LAYERS = (DEPTH + 1) // 3
N_C_LAYERS = DEPTH // 3
EPS = 1e-6
NEG_INF = -1e30

kernel_name = 'hybrid_dit_gmlp_pool_swa_step'


def rms_norm(x, g):
    xf = x.astype(jnp.float32)
    y = xf * lax.rsqrt(jnp.mean(xf * xf, axis=-1, keepdims=True) + EPS)
    return (y * g.astype(jnp.float32)).astype(x.dtype)


def ada_params(cond, w, b):
    m = jax.nn.silu(cond) @ w + b
    return jnp.split(m[:, None, :], 6, axis=-1)


def chunk_gmlp(h, w_in, g_v, w_s, b_s, w_out):
    B, S, _ = h.shape
    z = jax.nn.gelu(h @ w_in)
    u, v = jnp.split(z, 2, axis=-1)
    v = rms_norm(v, g_v)
    vc = v.reshape(B, S // CHUNK, CHUNK, N_GROUPS_A, D_GMLP // N_GROUPS_A)
    sv = jnp.einsum('gij,bnjgc->bnigc', w_s, vc) + b_s.T[None, None, :, :, None]
    return (u * sv.reshape(B, S, D_GMLP)) @ w_out


def multi_pool(h, w_grp, b_grp, scale):
    B, S, D = h.shape
    hf = h.astype(jnp.float32).reshape(B, S, N_POOL_GROUPS, D_POOL_GROUP)
    cs = jnp.concatenate([jnp.zeros_like(hf[:, :1]), jnp.cumsum(hf, axis=1)], axis=1)
    t = jnp.arange(S)[:, None]
    half = jnp.array(POOL_WINDOWS, dtype=jnp.int32)[None, :] // 2
    lo = jnp.clip(t - half, 0, S - 1)
    hi = jnp.clip(t + half - 1, 0, S - 1)
    gi = jnp.arange(N_POOL_GROUPS)[None, :]
    cnt = (hi - lo + 1).astype(jnp.float32)[None, :, :, None]
    pooled = (cs[:, hi + 1, gi] - cs[:, lo, gi]) / cnt
    d = (pooled - hf).astype(h.dtype)
    y = jnp.einsum('bsgi,gio->bsgo', d, w_grp).reshape(B, S, D) + b_grp
    return y * scale


def qkv_proj(h, w_qkv, g_q, g_k):
    B, S, _ = h.shape
    q, k, v = jnp.split(h @ w_qkv, [N_HEADS * HEAD_DIM, (N_HEADS + N_KV_HEADS) * HEAD_DIM], axis=-1)
    q = rms_norm(q.reshape(B, S, N_HEADS, HEAD_DIM), g_q)
    k = rms_norm(k.reshape(B, S, N_KV_HEADS, HEAD_DIM), g_k)
    return q, k, v.reshape(B, S, N_KV_HEADS, HEAD_DIM)


def axial_rope(x):
    B, S, H, _ = x.shape
    rows = S // GRID_W
    row = jnp.repeat(jnp.arange(rows), GRID_W).astype(jnp.float32)
    col = jnp.tile(jnp.arange(GRID_W), rows).astype(jnp.float32)
    n_freq = HEAD_DIM // 4
    inv = ROPE_THETA ** (-jnp.arange(n_freq, dtype=jnp.float32) / n_freq)
    ang = jnp.stack([row[:, None] * inv, col[:, None] * inv], axis=1)[:, None]
    cos, sin = jnp.cos(ang), jnp.sin(ang)
    xr = x.astype(jnp.float32).reshape(B, S, H, 2, 2, n_freq)
    a, b = xr[..., 0, :], xr[..., 1, :]
    out = jnp.stack([a * cos - b * sin, a * sin + b * cos], axis=-2)
    return out.reshape(x.shape).astype(x.dtype)


def sink_softmax(scores, sink):
    sk = jnp.broadcast_to(sink[:, :, None, None].astype(jnp.float32), scores.shape[:-1] + (1,))
    p = jax.nn.softmax(jnp.concatenate([scores, sk], axis=-1), axis=-1)
    return p[..., :-1]


def context_attention(q, k, v, sink):
    B, S = q.shape[:2]
    nb = S // BLOCK
    scale = HEAD_DIM ** -0.5
    qb = q.reshape(B, nb, BLOCK, N_KV_HEADS, Q_PER_KV, HEAD_DIM).transpose(1, 0, 2, 3, 4, 5)

    def one_block(qi):
        s = jnp.einsum('bqkgd,bskd->bkgqs', qi, k, preferred_element_type=jnp.float32) * scale
        p = sink_softmax(s, sink).astype(v.dtype)
        return jnp.einsum('bkgqs,bskd->bqkgd', p, v)

    o = lax.map(one_block, qb)
    return o.transpose(1, 0, 2, 3, 4, 5).reshape(B, S, N_HEADS * HEAD_DIM)


def latent_attention(q, k, v, ctx_k, ctx_v, sink):
    B, S = q.shape[:2]
    nb = S // BLOCK
    span = 3 * BLOCK
    scale = HEAD_DIM ** -0.5
    qb = q.reshape(B, nb, BLOCK, N_KV_HEADS, Q_PER_KV, HEAD_DIM).transpose(1, 0, 2, 3, 4, 5)
    pad = ((0, 0), (BLOCK, BLOCK), (0, 0), (0, 0))
    kp, vp = jnp.pad(k, pad), jnp.pad(v, pad)

    def one_block(args):
        qi, n = args
        start = n * BLOCK
        kw = lax.dynamic_slice_in_dim(kp, start, span, axis=1)
        vw = lax.dynamic_slice_in_dim(vp, start, span, axis=1)
        qpos = start + jnp.arange(BLOCK)
        kpos = start - BLOCK + jnp.arange(span)
        valid = (kpos[None, :] >= 0) & (kpos[None, :] < S) & (jnp.abs(qpos[:, None] - kpos[None, :]) <= WINDOW)
        s_w = jnp.einsum('bqkgd,bskd->bkgqs', qi, kw, preferred_element_type=jnp.float32) * scale
        s_w = jnp.where(valid, s_w, NEG_INF)
        s_c = jnp.einsum('bqkgd,bskd->bkgqs', qi, ctx_k, preferred_element_type=jnp.float32) * scale
        p = sink_softmax(jnp.concatenate([s_w, s_c], axis=-1), sink).astype(v.dtype)
        return (jnp.einsum('bkgqs,bskd->bqkgd', p[..., :span], vw)
                + jnp.einsum('bkgqs,bskd->bqkgd', p[..., span:], ctx_v))

    o = lax.map(one_block, (qb, jnp.arange(nb)))
    return o.transpose(1, 0, 2, 3, 4, 5).reshape(B, S, N_HEADS * HEAD_DIM)


def conv_ffn(h, w_in, conv_w, conv_b, w_out):
    a, u = jnp.split(h @ w_in, 2, axis=-1)
    ap = jnp.pad(a, ((0, 0), (1, 1), (0, 0)))
    a = ap[:, :-2] * conv_w[0] + ap[:, 1:-1] * conv_w[1] + ap[:, 2:] * conv_w[2] + conv_b
    return (jax.nn.gelu(a) * u) @ w_out


def run_trunk(x, cond, is_context, cache_k, cache_v, weights):
    (w_ada, b_ada, g_mix, g_ffn, w_ffn_in, ffn_conv_w, ffn_conv_b, w_ffn_out,
     a_w_in, a_g_v, a_w_s, a_b_s, a_w_out,
     p_w, p_b, p_scale,
     c_w_qkv, c_g_q, c_g_k, c_sink, c_w_o) = weights
    new_k, new_v = [], []
    for i in range(DEPTH):
        kind, j = i % N_MIXERS, i // N_MIXERS
        sh1, sc1, gt1, sh2, sc2, gt2 = ada_params(cond, w_ada[i], b_ada[i])
        h = rms_norm(x, g_mix[i]) * (1 + sc1) + sh1
        if kind == 0:
            y = chunk_gmlp(h, a_w_in[j], a_g_v[j], a_w_s[j], a_b_s[j], a_w_out[j])
        elif kind == 1:
            y = multi_pool(h, p_w[j], p_b[j], p_scale[j])
        else:
            q, k, v = qkv_proj(h, c_w_qkv[j], c_g_q[j], c_g_k[j])
            sink = c_sink[j].reshape(N_KV_HEADS, Q_PER_KV)
            if is_context:
                o = context_attention(q, k, v, sink)
                new_k.append(k)
                new_v.append(v)
            else:
                o = latent_attention(axial_rope(q), axial_rope(k), v, cache_k[:, j], cache_v[:, j], sink)
            y = o @ c_w_o[j]
        x = x + gt1 * y
        h = rms_norm(x, g_ffn[i]) * (1 + sc2) + sh2
        x = x + gt2 * conv_ffn(h, w_ffn_in[i], ffn_conv_w[i], ffn_conv_b[i], w_ffn_out[i])
    return x, new_k, new_v


def setup_inputs(seed: int = 0) -> dict:
    key = jax.random.key(seed)
    ks = jax.random.split(key, 32)
    nrm = jax.random.normal
    D = D_MODEL
    f32 = jnp.float32
    kvd = (N_C_LAYERS, PAST_LEN, N_KV_HEADS, HEAD_DIM)
    return {
        'x_prompt': nrm(ks[0], (BATCH, SEQ, D), f32),
        'x_sample': nrm(ks[1], (DEC_BATCH, DEC_SEQ, D), f32),
        'cache_k': nrm(ks[2], (DEC_BATCH,) + kvd, f32),
        'cache_v': nrm(ks[3], (DEC_BATCH,) + kvd, f32),
        'c': nrm(ks[4], (DEC_BATCH, D), f32),
        'c_ctx': nrm(ks[5], (D,), f32),
        'w_ada': nrm(ks[6], (DEPTH, D, 6 * D), f32) * (0.5 * D ** -0.5),
        'b_ada': nrm(ks[7], (DEPTH, 6 * D), f32) * 0.01,
        'g_mix': 1.0 + 0.1 * nrm(ks[8], (DEPTH, D), f32),
        'g_ffn': 1.0 + 0.1 * nrm(ks[9], (DEPTH, D), f32),
        'w_ffn_in': nrm(ks[10], (DEPTH, D, 2 * D_FF), f32) * D ** -0.5,
        'ffn_conv_w': nrm(ks[11], (DEPTH, CONV_W, D_FF), f32) * CONV_W ** -0.5,
        'ffn_conv_b': nrm(ks[12], (DEPTH, D_FF), f32) * 0.02,
        'w_ffn_out': nrm(ks[13], (DEPTH, D_FF, D), f32) * D_FF ** -0.5,
        'a_w_in': nrm(ks[14], (N_A_LAYERS, D, 2 * D_GMLP), f32) * D ** -0.5,
        'a_g_v': 1.0 + 0.1 * nrm(ks[15], (N_A_LAYERS, D_GMLP), f32),
        'a_w_s': nrm(ks[16], (N_A_LAYERS, N_GROUPS_A, CHUNK, CHUNK), f32) * CHUNK ** -0.5,
        'a_b_s': 1.0 + 0.1 * nrm(ks[17], (N_A_LAYERS, N_GROUPS_A, CHUNK), f32),
        'a_w_out': nrm(ks[18], (N_A_LAYERS, D_GMLP, D), f32) * D_GMLP ** -0.5,
        'p_w': nrm(ks[19], (N_B_LAYERS, N_POOL_GROUPS, D_POOL_GROUP, D_POOL_GROUP), f32) * D_POOL_GROUP ** -0.5,
        'p_b': nrm(ks[20], (N_B_LAYERS, D), f32) * 0.02,
        'p_scale': 1.0 + 0.1 * nrm(ks[21], (N_B_LAYERS, D), f32),
        'c_w_qkv': nrm(ks[22], (N_C_LAYERS, D, (N_HEADS + 2 * N_KV_HEADS) * HEAD_DIM), f32) * D ** -0.5,
        'c_g_q': 1.0 + 0.1 * nrm(ks[23], (N_C_LAYERS, HEAD_DIM), f32),
        'c_g_k': 1.0 + 0.1 * nrm(ks[24], (N_C_LAYERS, HEAD_DIM), f32),
        'c_sink': nrm(ks[25], (N_C_LAYERS, N_HEADS), f32),
        'c_w_o': nrm(ks[26], (N_C_LAYERS, N_HEADS * HEAD_DIM, D), f32) * (N_HEADS * HEAD_DIM) ** -0.5,
    }


def reference(x_prompt, x_sample, cache_k, cache_v, c, c_ctx,
              w_ada, b_ada, g_mix, g_ffn, w_ffn_in, ffn_conv_w, ffn_conv_b, w_ffn_out,
              a_w_in, a_g_v, a_w_s, a_b_s, a_w_out,
              p_w, p_b, p_scale,
              c_w_qkv, c_g_q, c_g_k, c_sink, c_w_o):
    weights = (w_ada, b_ada, g_mix, g_ffn, w_ffn_in, ffn_conv_w, ffn_conv_b, w_ffn_out,
               a_w_in, a_g_v, a_w_s, a_b_s, a_w_out,
               p_w, p_b, p_scale,
               c_w_qkv, c_g_q, c_g_k, c_sink, c_w_o)
    y_prompt, ks_ctx, vs_ctx = run_trunk(x_prompt, c_ctx[None, :], True, None, None, weights)
    new_cache_k = jnp.stack(ks_ctx, axis=1)
    new_cache_v = jnp.stack(vs_ctx, axis=1)
    y_sample, _, _ = run_trunk(x_sample, c, False, cache_k, cache_v, weights)
    return (y_prompt, y_sample, new_cache_k, new_cache_v)
```

```python
import functools

import numpy as np
import jax
import jax.numpy as jnp
from jax import lax
from jax.experimental import pallas as pl
from jax.experimental.pallas import tpu as pltpu

D_MODEL = 1024
BATCH = 32
SEQ = 256
DEPTH = 4
DEC_BATCH = 2
DEC_SEQ = 2048
PAST_LEN = 256
GRID_W = 64
N_MIXERS = 3
CHUNK = 128
D_GMLP = 2 * D_MODEL
N_GROUPS_A = 8
D_GROUP_A = D_GMLP // N_GROUPS_A
POOL_WINDOWS = (2, 4, 8, 16)
N_POOL_GROUPS = 4
D_POOL_GROUP = D_MODEL // N_POOL_GROUPS
N_HEADS = 16
N_KV_HEADS = 4
HEAD_DIM = 64
Q_PER_KV = N_HEADS // N_KV_HEADS
WINDOW = 128
BLOCK = 128
ROPE_THETA = 10000.0
D_FF = 2816
EPS = 1e-6
NEG_INF = -1e30

F32 = jnp.float32
BF16 = jnp.bfloat16

TM = 512
N_CTX_ROWS = BATCH * SEQ
N_SMP_ROWS = DEC_BATCH * DEC_SEQ
N_ROWS = N_CTX_ROWS + N_SMP_ROWS
NB = N_ROWS // TM
NCB = N_CTX_ROWS // TM
BPS = DEC_SEQ // TM
N_COND = 1 + DEC_BATCH
HALO = 16
MOD_ROWS = 8
LANES = 128
TN = 256
Q_DIM = N_HEADS * HEAD_DIM
KV_DIM = N_KV_HEADS * HEAD_DIM
KV_EXT = N_KV_HEADS * 2 * LANES
VMEM_LIMIT = 56 * 1024 * 1024


def _cond_idx(i):
    return jnp.where(i < NCB, 0, 1 + (i - NCB) // BPS)


def _seq_len(i):
    return jnp.where(i < NCB, SEQ, DEC_SEQ)


def _gelu(x):
    return 0.5 * x * (1.0 + jnp.tanh(0.7978845608028654 * (x + 0.044715 * (x * x * x))))


def _norm_mod(x, g, sc, sh):
    ms = jnp.mean(x * x, axis=-1, keepdims=True)
    return x * lax.rsqrt(ms + EPS) * g * (1.0 + sc) + sh


def _dot(a, b):
    return jnp.dot(a, b, preferred_element_type=F32)


def _dot_nt(a, b):
    return lax.dot_general(a, b, (((1,), (1,)), ((), ())), preferred_element_type=F32)


def _params():
    return pltpu.CompilerParams(dimension_semantics=("arbitrary",), vmem_limit_bytes=VMEM_LIMIT)


def _resident(shape):
    nd = len(shape)
    return pl.BlockSpec(shape, lambda *_: (0,) * nd, pipeline_mode=pl.Buffered(1))


def _row_spec(width, offset=0):
    return pl.BlockSpec((TM, width), lambda i: (i + offset, 0))


def _mod_spec(layer, offset=0):
    return pl.BlockSpec((None, MOD_ROWS, D_MODEL), lambda i: (layer * N_COND + _cond_idx(i + offset), 0, 0))


def _halo_specs(width, rows, offset=0):
    per = TM // rows
    last = N_ROWS // rows - 1
    prev = pl.BlockSpec((rows, width), lambda i: (jnp.maximum((i + offset) * per - 1, 0), 0))
    nxt = pl.BlockSpec((rows, width), lambda i: (jnp.minimum((i + offset + 1) * per, last), 0))
    return prev, nxt


ADA_TN = 1536


def _ada_kernel(c_ref, w_ref, b_ref, o_ref):
    c = c_ref[...]
    s = (c * (1.0 / (1.0 + jnp.exp(-c)))).astype(BF16)
    o_ref[...] = _dot(s, w_ref[...].astype(BF16)) + b_ref[...]


def _ada_call(cond8, w_ada, b_ada):
    nt = 6 * D_MODEL // ADA_TN
    return pl.pallas_call(
        _ada_kernel,
        grid=(DEPTH, nt),
        in_specs=[
            pl.BlockSpec((MOD_ROWS, D_MODEL), lambda l, j: (0, 0)),
            pl.BlockSpec((None, D_MODEL, ADA_TN), lambda l, j: (l, 0, j)),
            pl.BlockSpec((None, 1, ADA_TN), lambda l, j: (l, 0, j)),
        ],
        out_specs=pl.BlockSpec((None, MOD_ROWS, ADA_TN), lambda l, j: (l, 0, j)),
        out_shape=jax.ShapeDtypeStruct((DEPTH, MOD_ROWS, 6 * D_MODEL), F32),
        compiler_params=pltpu.CompilerParams(dimension_semantics=("arbitrary", "arbitrary"),
                                             vmem_limit_bytes=VMEM_LIMIT),
        name="ada",
    )(cond8, w_ada, b_ada.reshape(DEPTH, 1, 6 * D_MODEL))


def _gmlp_kernel(x_ref, g_ref, mod_ref, win_ref, gv_ref, ws_ref, bs_ref, wout_ref, o_ref,
                 h_ref, u_ref, v_ref, p_ref):
    x = x_ref[...]
    h_ref[...] = _norm_mod(x, g_ref[...], mod_ref[1:2, :], mod_ref[0:1, :]).astype(BF16)
    ssq = jnp.zeros((TM, 1), F32)
    for j in range(D_GMLP // TN):
        sl = slice(j * TN, (j + 1) * TN)
        u_ref[:, sl] = _gelu(_dot(h_ref[...], win_ref[:, sl]))
        v = _gelu(_dot(h_ref[...], win_ref[:, D_GMLP + j * TN:D_GMLP + (j + 1) * TN]))
        v_ref[:, sl] = v
        ssq = ssq + jnp.sum(v * v, axis=-1, keepdims=True)
    r = lax.rsqrt(ssq * (1.0 / D_GMLP) + EPS)
    for g in range(N_GROUPS_A):
        sl = slice(g * D_GROUP_A, (g + 1) * D_GROUP_A)
        vn = (v_ref[:, sl] * r * gv_ref[:, sl]).astype(BF16)
        for c in range(TM // CHUNK):
            rows = slice(c * CHUNK, (c + 1) * CHUNK)
            sv = _dot(ws_ref[g], vn[rows, :]) + bs_ref[g]
            p_ref[rows, sl] = (u_ref[rows, sl] * sv).astype(BF16)
    y = _dot(p_ref[...], wout_ref[...])
    o_ref[...] = x + mod_ref[2:3, :] * y


def _gmlp_call(x, layer, mods, g_mix, w_in, g_v, w_s, b_s, w_out):
    return pl.pallas_call(
        _gmlp_kernel,
        grid=(NB,),
        in_specs=[
            _row_spec(D_MODEL),
            _resident((1, D_MODEL)),
            _mod_spec(layer),
            _resident((D_MODEL, 2 * D_GMLP)),
            _resident((1, D_GMLP)),
            _resident((N_GROUPS_A, CHUNK, CHUNK)),
            _resident((N_GROUPS_A, CHUNK, 1)),
            _resident((D_GMLP, D_MODEL)),
        ],
        out_specs=_row_spec(D_MODEL),
        out_shape=jax.ShapeDtypeStruct((N_ROWS, D_MODEL), F32),
        scratch_shapes=[
            pltpu.VMEM((TM, D_MODEL), BF16),
            pltpu.VMEM((TM, D_GMLP), F32),
            pltpu.VMEM((TM, D_GMLP), F32),
            pltpu.VMEM((TM, D_GMLP), BF16),
        ],
        compiler_params=_params(),
        name="gmlp",
    )(x, g_mix, mods, w_in, g_v, w_s, b_s, w_out)


EXT = TM + 2 * HALO


def _ext_pos(i, width):
    e = lax.broadcasted_iota(jnp.int32, (EXT, width), 0)
    e = jnp.where(e < TM + HALO, e, e - EXT)
    return (i * TM + e) & (_seq_len(i) - 1)


def _pool_kernel(x_ref, xp_ref, xn_ref, g_ref, mod_ref, pw_ref, pb_ref, ps_ref, o_ref, hext_ref):
    i = pl.program_id(0)
    g = g_ref[...]
    sc = mod_ref[1:2, :]
    sh = mod_ref[0:1, :]
    x = x_ref[...]
    hext_ref[0:TM, :] = _norm_mod(x, g, sc, sh)
    hext_ref[TM:TM + HALO, :] = _norm_mod(xn_ref[...], g, sc, sh)
    hext_ref[TM + HALO:EXT, :] = _norm_mod(xp_ref[...], g, sc, sh)
    seq = _seq_len(i)
    pos = _ext_pos(i, D_POOL_GROUP)
    p = (i * TM + lax.broadcasted_iota(jnp.int32, (TM, D_POOL_GROUP), 0)) & (seq - 1)
    for grp in range(N_POOL_GROUPS):
        sl = slice(grp * D_POOL_GROUP, (grp + 1) * D_POOL_GROUP)
        half = POOL_WINDOWS[grp] // 2
        he = hext_ref[:, sl]
        fwd = he
        bwd = jnp.where(pos >= 1, pltpu.roll(he, 1, 0), 0.0)
        span = 1
        while span < half:
            fwd = fwd + jnp.where(pos < seq - span, pltpu.roll(fwd, EXT - span, 0), 0.0)
            bwd = bwd + jnp.where(pos >= span, pltpu.roll(bwd, span, 0), 0.0)
            span *= 2
        cnt =(jnp.minimum(p + half - 1, seq - 1) - jnp.maximum(p - half, 0) + 1).astype(F32)
        d = (fwd[:TM, :] + bwd[:TM, :]) / cnt - he[:TM, :]
        y = (_dot(d.astype(BF16), pw_ref[grp]) + pb_ref[:, sl]) * ps_ref[:, sl]
        o_ref[:, sl] = x[:, sl] + mod_ref[2:3, sl] * y


def _pool_call(x, layer, mods, g_mix, p_w, p_b, p_scale):
    prev, nxt = _halo_specs(D_MODEL, HALO)
    return pl.pallas_call(
        _pool_kernel,
        grid=(NB,),
        in_specs=[
            _row_spec(D_MODEL), prev, nxt,
            _resident((1, D_MODEL)),
            _mod_spec(layer),
            _resident((N_POOL_GROUPS, D_POOL_GROUP, D_POOL_GROUP)),
            _resident((1, D_MODEL)),
            _resident((1, D_MODEL)),
        ],
        out_specs=_row_spec(D_MODEL),
        out_shape=jax.ShapeDtypeStruct((N_ROWS, D_MODEL), F32),
        scratch_shapes=[pltpu.VMEM((EXT, D_MODEL), F32)],
        compiler_params=_params(),
        name="pool",
    )(x, x, x, g_mix, mods, p_w, p_b, p_scale)


def _rope_tables():
    n_freq = HEAD_DIM // 4
    inv = ROPE_THETA ** (-np.arange(n_freq, dtype=np.float64) / n_freq)
    t = np.arange(DEC_SEQ)
    lane = np.arange(LANES) % HEAD_DIM
    quarter, f = lane // n_freq, lane % n_freq
    posn = np.where(quarter[None, :] < 2, (t // GRID_W)[:, None], (t % GRID_W)[:, None])
    ang = posn * inv[f][None, :]
    cos, sin = np.cos(ang), np.sin(ang)
    odd = (quarter % 2 == 1)[None, :]
    tab = np.concatenate([cos, np.where(odd, sin, 0.0), np.where(odd, 0.0, -sin)], axis=1)
    ident = np.concatenate([np.ones((TM, LANES)), np.zeros((TM, 2 * LANES))], axis=1)
    return np.concatenate([ident, tab], axis=0).astype(np.float32)


def _head_sumsq(t, bd_ref):
    sq = t * t
    hi = sq.astype(BF16)
    lo = (sq - hi.astype(F32)).astype(BF16)
    return _dot(hi, bd_ref[...]) + _dot(lo, bd_ref[...])


def _rope(t, tab_ref):
    q = HEAD_DIM // 4
    return (t * tab_ref[:, 0:LANES]
            + pltpu.roll(t, q, 1) * tab_ref[:, LANES:2 * LANES]
            + pltpu.roll(t, LANES - q, 1) * tab_ref[:, 2 * LANES:3 * LANES])


def _kv_ext(t, o_ref):
    lo = lax.broadcasted_iota(jnp.int32, (TM, LANES), 1) < HEAD_DIM
    for m in range(KV_DIM // LANES):
        a = t[:, m * LANES:(m + 1) * LANES]
        r = pltpu.roll(a, HEAD_DIM, 1)
        base = m * 4 * LANES
        o_ref[:, base:base + LANES] = jnp.where(lo, a, 0.0).astype(o_ref.dtype)
        o_ref[:, base + LANES:base + 2 * LANES] = jnp.where(lo, 0.0, r).astype(o_ref.dtype)
        o_ref[:, base + 2 * LANES:base + 3 * LANES] = jnp.where(lo, r, 0.0).astype(o_ref.dtype)
        o_ref[:, base + 3 * LANES:base + 4 * LANES] = jnp.where(lo, 0.0, a).astype(o_ref.dtype)


def _qkv_kernel(x_ref, g_ref, mod_ref, w_ref, gq_ref, gk_ref, bd_ref, tab_ref,
                q_ref, kx_ref, vx_ref, kn_ref, v_ref, h_ref, kr_ref):
    h_ref[...] = _norm_mod(x_ref[...], g_ref[...], mod_ref[1:2, :], mod_ref[0:1, :]).astype(BF16)
    scale = HEAD_DIM ** -0.5
    for c in range(Q_DIM // TN):
        sl = slice(c * TN, (c + 1) * TN)
        t = _dot(h_ref[...], w_ref[:, sl])
        t = t * lax.rsqrt(_head_sumsq(t, bd_ref) * (1.0 / HEAD_DIM) + EPS) * (gq_ref[...] * scale)
        for m in range(TN // LANES):
            q_ref[:, c * TN + m * LANES:c * TN + (m + 1) * LANES] = _rope(
                t[:, m * LANES:(m + 1) * LANES], tab_ref).astype(BF16)
    t = _dot(h_ref[...], w_ref[:, Q_DIM:Q_DIM + KV_DIM])
    t = t * lax.rsqrt(_head_sumsq(t, bd_ref) * (1.0 / HEAD_DIM) + EPS) * gk_ref[...]
    kn_ref[...] = t
    for m in range(KV_DIM // LANES):
        kr_ref[:, m * LANES:(m + 1) * LANES] = _rope(t[:, m * LANES:(m + 1) * LANES], tab_ref)
    _kv_ext(kr_ref[...], kx_ref)
    v = _dot(h_ref[...], w_ref[:, Q_DIM + KV_DIM:Q_DIM + 2 * KV_DIM])
    v_ref[...] = v
    _kv_ext(v, vx_ref)


def _qkv_call(x, layer, mods, g_mix, w_qkv, g_q, g_k, bd, tab):
    tab_spec = pl.BlockSpec((TM, 3 * LANES), lambda i: (jnp.where(i < NCB, 0, 1 + (i - NCB) % BPS), 0))
    return pl.pallas_call(
        _qkv_kernel,
        grid=(NB,),
        in_specs=[
            _row_spec(D_MODEL),
            _resident((1, D_MODEL)),
            _mod_spec(layer),
            _resident((D_MODEL, Q_DIM + 2 * KV_DIM)),
            _resident((1, TN)),
            _resident((1, TN)),
            _resident((TN, TN)),
            tab_spec,
        ],
        out_specs=[_row_spec(Q_DIM), _row_spec(KV_EXT), _row_spec(KV_EXT), _row_spec(KV_DIM), _row_spec(KV_DIM)],
        out_shape=[
            jax.ShapeDtypeStruct((N_ROWS, Q_DIM), BF16),
            jax.ShapeDtypeStruct((N_ROWS, KV_EXT), BF16),
            jax.ShapeDtypeStruct((N_ROWS, KV_EXT), BF16),
            jax.ShapeDtypeStruct((N_ROWS, KV_DIM), F32),
            jax.ShapeDtypeStruct((N_ROWS, KV_DIM), F32),
        ],
        scratch_shapes=[pltpu.VMEM((TM, D_MODEL), BF16), pltpu.VMEM((TM, KV_DIM), F32)],
        compiler_params=_params(),
        name="qkv",
    )(x, g_mix, mods, w_qkv, g_q, g_k, bd, tab)


def _head_lanes(g, j):
    base = (g * 2 + j) * LANES
    return slice(base, base + LANES)


def _ctx_attn_kernel(sink_ref, x_ref, mod_ref, q_ref, kx_ref, vx_ref, wo_ref, o_ref, a_ref):
    for s in range(TM // SEQ):
        rows = slice(s * SEQ, (s + 1) * SEQ)
        for g in range(N_KV_HEADS):
            for pr in range(2):
                qp = q_ref[rows, _head_lanes(g, pr)]
                acc = jnp.zeros((SEQ, LANES), F32)
                for var in range(2):
                    sink = sink_ref[g * Q_PER_KV + 2 * pr + var]
                    sc = _dot_nt(qp, kx_ref[rows, _head_lanes(g, var)])
                    m = jnp.maximum(jnp.max(sc, axis=-1, keepdims=True), sink)
                    e = jnp.exp(sc - m)
                    den = jnp.sum(e, axis=-1, keepdims=True) + jnp.exp(sink - m)
                    acc = acc + _dot(e.astype(BF16), vx_ref[rows, _head_lanes(g, var)]) / den
                a_ref[rows, _head_lanes(g, pr)] = acc.astype(BF16)
    y = _dot(a_ref[...], wo_ref[...])
    o_ref[...] = x_ref[...] + mod_ref[2:3, :] * y


def _ctx_attn_call(x, layer, mods, sink, q, kx, vx, w_o):
    return pl.pallas_call(
        _ctx_attn_kernel,
        grid=(NCB,),
        in_specs=[
            pl.BlockSpec(memory_space=pltpu.SMEM),
            _row_spec(D_MODEL),
            _mod_spec(layer),
            _row_spec(Q_DIM), _row_spec(KV_EXT), _row_spec(KV_EXT),
            _resident((Q_DIM, D_MODEL)),
        ],
        out_specs=_row_spec(D_MODEL),
        out_shape=jax.ShapeDtypeStruct((N_ROWS, D_MODEL), F32),
        input_output_aliases={1: 0},
        scratch_shapes=[pltpu.VMEM((TM, Q_DIM), BF16)],
        compiler_params=_params(),
        name="ctx_attn",
    )(sink, x, mods, q, kx, vx, w_o)


N_QB = TM // BLOCK
KWIN = TM + 2 * BLOCK


def _smp_attn_kernel(sink_ref, x_ref, mod_ref, q_ref, kc_ref, kp_ref, kn_ref, vc_ref, vp_ref, vn_ref,
                     ck_ref, cv_ref, wo_ref, o_ref, a_ref, kall_ref, vall_ref):
    i = pl.program_id(0)
    kall_ref[0:BLOCK, :] = kp_ref[...]
    kall_ref[BLOCK:BLOCK + TM, :] = kc_ref[...]
    kall_ref[BLOCK + TM:KWIN, :] = kn_ref[...]
    vall_ref[0:BLOCK, :] = vp_ref[...]
    vall_ref[BLOCK:BLOCK + TM, :] = vc_ref[...]
    vall_ref[BLOCK + TM:KWIN, :] = vn_ref[...]
    first = (i % BPS) == 0
    last = (i % BPS) == BPS - 1
    qi = lax.broadcasted_iota(jnp.int32, (BLOCK, BLOCK), 0)
    kj = lax.broadcasted_iota(jnp.int32, (BLOCK, BLOCK), 1)
    for b in range(N_QB):
        rows = slice(b * BLOCK, (b + 1) * BLOCK)
        win = slice(b * BLOCK, (b + 3) * BLOCK)
        ok_prev = kj >= (qi + jnp.where(first, BLOCK, 0) if b == 0 else qi)
        ok_next = kj <= (qi - jnp.where(last, BLOCK, 0) if b == N_QB - 1 else qi)
        for g in range(N_KV_HEADS):
            for pr in range(2):
                qp = q_ref[rows, _head_lanes(g, pr)]
                acc = jnp.zeros((BLOCK, LANES), F32)
                for var in range(2):
                    sink = sink_ref[g * Q_PER_KV + 2 * pr + var]
                    lanes = _head_lanes(g, var)
                    sw = _dot_nt(qp, kall_ref[win, lanes])
                    s0 = jnp.where(ok_prev, sw[:, 0:BLOCK], NEG_INF)
                    s1 = sw[:, BLOCK:2 * BLOCK]
                    s2 = jnp.where(ok_next, sw[:, 2 * BLOCK:3 * BLOCK], NEG_INF)
                    s3 = _dot_nt(qp, ck_ref[:, lanes])
                    m = jnp.maximum(jnp.maximum(jnp.max(s0, axis=-1, keepdims=True),
                                                jnp.max(s1, axis=-1, keepdims=True)),
                                    jnp.maximum(jnp.max(s2, axis=-1, keepdims=True),
                                                jnp.max(s3, axis=-1, keepdims=True)))
                    m = jnp.maximum(m, sink)
                    ew = jnp.exp(jnp.concatenate([s0, s1, s2], axis=-1) - m)
                    ec = jnp.exp(s3 - m)
                    den = (jnp.sum(ew, axis=-1, keepdims=True) + jnp.sum(ec, axis=-1, keepdims=True)
                           + jnp.exp(sink - m))
                    pv = _dot(ew.astype(BF16), vall_ref[win, lanes]) + _dot(ec.astype(BF16), cv_ref[:, lanes])
                    acc = acc + pv / den
                a_ref[rows, _head_lanes(g, pr)] = acc.astype(BF16)
    y = _dot(a_ref[...], wo_ref[...])
    o_ref[...] = x_ref[...] + mod_ref[2:3, :] * y


def _smp_attn_call(x, layer, mods, sink, q, kx, vx, ckx, cvx, w_o):
    prev, nxt = _halo_specs(KV_EXT, BLOCK, offset=NCB)
    ctx_spec = pl.BlockSpec((None, PAST_LEN, KV_EXT), lambda i: (i // BPS, 0, 0))
    return pl.pallas_call(
        _smp_attn_kernel,
        grid=(NB - NCB,),
        in_specs=[
            pl.BlockSpec(memory_space=pltpu.SMEM),
            _row_spec(D_MODEL, NCB),
            _mod_spec(layer, NCB),
            _row_spec(Q_DIM, NCB),
            _row_spec(KV_EXT, NCB), prev, nxt,
            _row_spec(KV_EXT, NCB), prev, nxt,
            ctx_spec, ctx_spec,
            _resident((Q_DIM, D_MODEL)),
        ],
        out_specs=_row_spec(D_MODEL, NCB),
        out_shape=jax.ShapeDtypeStruct((N_ROWS, D_MODEL), F32),
        input_output_aliases={1: 0},
        scratch_shapes=[pltpu.VMEM((TM, Q_DIM), BF16), pltpu.VMEM((KWIN, KV_EXT), BF16),
                        pltpu.VMEM((KWIN, KV_EXT), BF16)],
        compiler_params=_params(),
        name="smp_attn",
    )(sink, x, mods, q, kx, kx, kx, vx, vx, vx, ckx, cvx, w_o)


def _ext_heads(t):
    lead = t.shape[:-1]
    h = t.reshape(lead + (N_KV_HEADS, 1, HEAD_DIM))
    z = jnp.zeros_like(h)
    e = jnp.concatenate([jnp.concatenate([h, z], axis=-1), jnp.concatenate([z, h], axis=-1)], axis=-2)
    return e.reshape(lead + (KV_EXT,))


FEXT = TM + 2 * HALO
N_FT = D_FF // TN


def _ffn_kernel(x_ref, xp_ref, xn_ref, g_ref, mod_ref, wa_ref, wu_ref, cw_ref, cb_ref, wo_ref, o_ref,
                hext_ref, gbuf_ref):
    i = pl.program_id(0)
    g = g_ref[...]
    sh = mod_ref[3:4, :]
    sc = mod_ref[4:5, :]
    x = x_ref[...]
    hext_ref[0:TM, :] = _norm_mod(x, g, sc, sh).astype(BF16)
    hext_ref[TM:TM + HALO, :] = _norm_mod(xn_ref[...], g, sc, sh).astype(BF16)
    hext_ref[TM + HALO:FEXT, :] = _norm_mod(xp_ref[...], g, sc, sh).astype(BF16)
    seq = _seq_len(i)
    pos = (i * TM + lax.broadcasted_iota(jnp.int32, (TM, TN), 0)) & (seq - 1)
    has_prev = pos != 0
    has_next = pos != seq - 1
    for j in range(N_FT):
        sl = slice(j * TN, (j + 1) * TN)
        a = _dot(hext_ref[...], wa_ref[:, sl])
        u = _dot(hext_ref[0:TM, :], wu_ref[:, sl])
        a_prev = jnp.where(has_prev, pltpu.roll(a, 1, 0)[:TM, :], 0.0)
        a_next = jnp.where(has_next, pltpu.roll(a, FEXT - 1, 0)[:TM, :], 0.0)
        a = a_prev * cw_ref[0:1, sl] + a[:TM, :] * cw_ref[1:2, sl] + a_next * cw_ref[2:3, sl] + cb_ref[:, sl]
        gbuf_ref[:, sl] = (_gelu(a) * u).astype(BF16)
    y = _dot(gbuf_ref[...], wo_ref[...])
    o_ref[...] = x + mod_ref[5:6, :] * y


def _ffn_call(x, layer, mods, g_ffn, w_a, w_u, conv_w, conv_b, w_out):
    prev, nxt = _halo_specs(D_MODEL, HALO)
    return pl.pallas_call(
        _ffn_kernel,
        grid=(NB,),
        in_specs=[
            _row_spec(D_MODEL), prev, nxt,
            _resident((1, D_MODEL)),
            _mod_spec(layer),
            _resident((D_MODEL, D_FF)),
            _resident((D_MODEL, D_FF)),
            _resident((MOD_ROWS, D_FF)),
            _resident((1, D_FF)),
            _resident((D_FF, D_MODEL)),
        ],
        out_specs=_row_spec(D_MODEL),
        out_shape=jax.ShapeDtypeStruct((N_ROWS, D_MODEL), F32),
        scratch_shapes=[pltpu.VMEM((FEXT, D_MODEL), BF16), pltpu.VMEM((TM, D_FF), BF16)],
        compiler_params=_params(),
        name="ffn",
    )(x, x, x, g_ffn, mods, w_a, w_u, conv_w, conv_b, w_out)


def kernel(x_prompt, x_sample, cache_k, cache_v, c, c_ctx, w_ada, b_ada, g_mix, g_ffn, w_ffn_in, ffn_conv_w,
           ffn_conv_b, w_ffn_out, a_w_in, a_g_v, a_w_s, a_b_s, a_w_out, p_w, p_b, p_scale, c_w_qkv, c_g_q,
           c_g_k, c_sink, c_w_o):
    x = jnp.concatenate([x_prompt.reshape(N_CTX_ROWS, D_MODEL), x_sample.reshape(N_SMP_ROWS, D_MODEL)], axis=0)

    cond8 = jnp.concatenate([c_ctx[None, :], c, jnp.zeros((MOD_ROWS - N_COND, D_MODEL), F32)], axis=0)
    ada = _ada_call(cond8, w_ada, b_ada)
    mods = ada[:, :N_COND, :].reshape(DEPTH, N_COND, 6, D_MODEL)
    mods = jnp.pad(mods, ((0, 0), (0, 0), (0, MOD_ROWS - 6), (0, 0))).reshape(DEPTH * N_COND, MOD_ROWS, D_MODEL)

    bd = jnp.asarray(np.kron(np.eye(TN // HEAD_DIM), np.ones((HEAD_DIM, HEAD_DIM))), dtype=BF16)
    tab = jnp.asarray(_rope_tables())
    new_k = new_v = None

    for i in range(DEPTH):
        kind, j = i % N_MIXERS, i // N_MIXERS
        gm = g_mix[i][None, :]
        if kind == 0:
            x = _gmlp_call(x, i, mods, gm, a_w_in[j].astype(BF16), a_g_v[j][None, :], a_w_s[j].astype(BF16),
                           a_b_s[j][:, :, None], a_w_out[j].astype(BF16))
        elif kind == 1:
            x = _pool_call(x, i, mods, gm, p_w[j].astype(BF16), p_b[j][None, :], p_scale[j][None, :])
        else:
            gq = jnp.tile(c_g_q[j], TN // HEAD_DIM)[None, :]
            gk = jnp.tile(c_g_k[j], TN // HEAD_DIM)[None, :]
            q, kx, vx, kn, v = _qkv_call(x, i, mods, gm, c_w_qkv[j].astype(BF16), gq, gk, bd, tab)
            new_k = kn[:N_CTX_ROWS].reshape(BATCH, 1, SEQ, N_KV_HEADS, HEAD_DIM)
            new_v = v[:N_CTX_ROWS].reshape(BATCH, 1, SEQ, N_KV_HEADS, HEAD_DIM)
            ckx = _ext_heads(cache_k[:, j].reshape(DEC_BATCH, PAST_LEN, KV_DIM)).astype(BF16)
            cvx = _ext_heads(cache_v[:, j].reshape(DEC_BATCH, PAST_LEN, KV_DIM)).astype(BF16)
            w_o = c_w_o[j].astype(BF16)
            x = _ctx_attn_call(x, i, mods, c_sink[j], q, kx, vx, w_o)
            x = _smp_attn_call(x, i, mods, c_sink[j], q, kx, vx, ckx, cvx, w_o)
        conv_w = jnp.pad(ffn_conv_w[i], ((0, MOD_ROWS - 3), (0, 0)))
        x = _ffn_call(x, i, mods, g_ffn[i][None, :], w_ffn_in[i][:, :D_FF].astype(BF16),
                      w_ffn_in[i][:, D_FF:].astype(BF16), conv_w, ffn_conv_b[i][None, :],
                      w_ffn_out[i].astype(BF16))

    y_prompt = x[:N_CTX_ROWS].reshape(BATCH, SEQ, D_MODEL)
    y_sample = x[N_CTX_ROWS:].reshape(DEC_BATCH, DEC_SEQ, D_MODEL)
    return (y_prompt, y_sample, new_k, new_v)
```

```python
import functools

import numpy as np
import jax
import jax.numpy as jnp
from jax import lax
from jax.experimental import pallas as pl
from jax.experimental.pallas import tpu as pltpu

D_MODEL = 1024
BATCH = 32
SEQ = 256
DEPTH = 4
DEC_BATCH = 2
DEC_SEQ = 2048
PAST_LEN = 256
GRID_W = 64
N_MIXERS = 3
CHUNK = 128
D_GMLP = 2 * D_MODEL
N_GROUPS_A = 8
D_GROUP_A = D_GMLP // N_GROUPS_A
POOL_WINDOWS = (2, 4, 8, 16)
N_POOL_GROUPS = 4
D_POOL_GROUP = D_MODEL // N_POOL_GROUPS
N_HEADS = 16
N_KV_HEADS = 4
HEAD_DIM = 64
Q_PER_KV = N_HEADS // N_KV_HEADS
WINDOW = 128
BLOCK = 128
ROPE_THETA = 10000.0
D_FF = 2816
EPS = 1e-6
NEG_INF = -1e30
LOG2E = 1.4426950408889634

F32 = jnp.float32
BF16 = jnp.bfloat16

TM = 512
N_CTX_ROWS = BATCH * SEQ
N_SMP_ROWS = DEC_BATCH * DEC_SEQ
N_ROWS = N_CTX_ROWS + N_SMP_ROWS
NB = N_ROWS // TM
NCB = N_CTX_ROWS // TM
BPS = DEC_SEQ // TM
N_COND = 1 + DEC_BATCH
HALO = 16
MOD_ROWS = 8
LANES = 128
TN = 256
Q_DIM = N_HEADS * HEAD_DIM
KV_DIM = N_KV_HEADS * HEAD_DIM
KV_EXT = N_KV_HEADS * 2 * LANES
VMEM_LIMIT = 56 * 1024 * 1024


def _cond_idx(i):
    return jnp.where(i < NCB, 0, 1 + (i - NCB) // BPS)


def _seq_len(i):
    return jnp.where(i < NCB, SEQ, DEC_SEQ)


def _gelu(x):
    return 0.5 * x * (1.0 + jnp.tanh(0.7978845608028654 * (x + 0.044715 * (x * x * x))))


def _norm_mod(x, g, sc, sh):
    ms = jnp.mean(x * x, axis=-1, keepdims=True)
    return x * lax.rsqrt(ms + EPS) * g * (1.0 + sc) + sh


def _dot(a, b):
    return jnp.dot(a, b, preferred_element_type=F32)


def _dot_nt(a, b):
    return lax.dot_general(a, b, (((1,), (1,)), ((), ())), preferred_element_type=F32)


def _params():
    return pltpu.CompilerParams(dimension_semantics=("arbitrary",), vmem_limit_bytes=VMEM_LIMIT)


def _resident(shape):
    nd = len(shape)
    return pl.BlockSpec(shape, lambda *_: (0,) * nd, pipeline_mode=pl.Buffered(1))


def _row_spec(width, offset=0):
    return pl.BlockSpec((TM, width), lambda i: (i + offset, 0))


def _ctx_rows_spec(width):
    return pl.BlockSpec((TM, width), lambda i: (jnp.minimum(i, NCB - 1), 0))


def _smp_rows_spec(width):
    return pl.BlockSpec((TM, width), lambda i: (jnp.maximum(i - NCB, 0), 0))


def _mod_spec(layer, offset=0):
    return pl.BlockSpec((None, MOD_ROWS, D_MODEL), lambda i: (layer * N_COND + _cond_idx(i + offset), 0, 0))


def _halo_specs(width, rows, offset=0):
    per = TM // rows
    last = N_ROWS // rows - 1
    prev = pl.BlockSpec((rows, width), lambda i: (jnp.maximum((i + offset) * per - 1, 0), 0))
    nxt = pl.BlockSpec((rows, width), lambda i: (jnp.minimum((i + offset + 1) * per, last), 0))
    return prev, nxt


ADA_TN = 1536


def _ada_kernel(c_ref, w_ref, b_ref, o_ref):
    c = c_ref[...]
    s = (c * (1.0 / (1.0 + jnp.exp(-c)))).astype(BF16)
    o_ref[...] = _dot(s, w_ref[...].astype(BF16)) + b_ref[...]


def _ada_call(cond8, w_ada, b_ada):
    nt = 6 * D_MODEL // ADA_TN
    return pl.pallas_call(
        _ada_kernel,
        grid=(DEPTH, nt),
        in_specs=[
            pl.BlockSpec((MOD_ROWS, D_MODEL), lambda l, j: (0, 0)),
            pl.BlockSpec((None, D_MODEL, ADA_TN), lambda l, j: (l, 0, j)),
            pl.BlockSpec((None, 1, ADA_TN), lambda l, j: (l, 0, j)),
        ],
        out_specs=pl.BlockSpec((None, MOD_ROWS, ADA_TN), lambda l, j: (l, 0, j)),
        out_shape=jax.ShapeDtypeStruct((DEPTH, MOD_ROWS, 6 * D_MODEL), F32),
        compiler_params=pltpu.CompilerParams(dimension_semantics=("arbitrary", "arbitrary"),
                                             vmem_limit_bytes=VMEM_LIMIT),
        name="ada",
    )(cond8, w_ada, b_ada.reshape(DEPTH, 1, 6 * D_MODEL))


def _gmlp_kernel(split_in, *refs):
    if split_in:
        xc_ref, xs_ref, *refs = refs
        x = jnp.where(pl.program_id(0) < NCB, xc_ref[...], xs_ref[...])
    else:
        x_ref, *refs = refs
        x = x_ref[...]
    g_ref, mod_ref, win_ref, gv_ref, ws_ref, bs_ref, wout_ref, o_ref, h_ref, u_ref, v_ref, p_ref = refs
    h_ref[...] = _norm_mod(x, g_ref[...], mod_ref[1:2, :], mod_ref[0:1, :]).astype(BF16)
    ssq = jnp.zeros((TM, 1), F32)
    for j in range(D_GMLP // TN):
        sl = slice(j * TN, (j + 1) * TN)
        u_ref[:, sl] = _gelu(_dot(h_ref[...], win_ref[:, sl]))
        v = _gelu(_dot(h_ref[...], win_ref[:, D_GMLP + j * TN:D_GMLP + (j + 1) * TN]))
        v_ref[:, sl] = v
        ssq = ssq + jnp.sum(v * v, axis=-1, keepdims=True)
    r = lax.rsqrt(ssq * (1.0 / D_GMLP) + EPS)
    for g in range(N_GROUPS_A):
        sl = slice(g * D_GROUP_A, (g + 1) * D_GROUP_A)
        vn = (v_ref[:, sl] * r * gv_ref[:, sl]).astype(BF16)
        for c in range(TM // CHUNK):
            rows = slice(c * CHUNK, (c + 1) * CHUNK)
            sv = _dot(ws_ref[g], vn[rows, :]) + bs_ref[g]
            p_ref[rows, sl] = (u_ref[rows, sl] * sv).astype(BF16)
    y = _dot(p_ref[...], wout_ref[...])
    o_ref[...] = x + mod_ref[2:3, :] * y


def _gmlp_call(x, layer, mods, g_mix, w_in, g_v, w_s, b_s, w_out):
    split_in = isinstance(x, tuple)
    if split_in:
        x_specs = [_ctx_rows_spec(D_MODEL), _smp_rows_spec(D_MODEL)]
        xs = x
    else:
        x_specs = [_row_spec(D_MODEL)]
        xs = (x,)
    return pl.pallas_call(
        functools.partial(_gmlp_kernel, split_in),
        grid=(NB,),
        in_specs=x_specs + [
            _resident((1, D_MODEL)),
            _mod_spec(layer),
            _resident((D_MODEL, 2 * D_GMLP)),
            _resident((1, D_GMLP)),
            _resident((N_GROUPS_A, CHUNK, CHUNK)),
            _resident((N_GROUPS_A, CHUNK, 1)),
            _resident((D_GMLP, D_MODEL)),
        ],
        out_specs=_row_spec(D_MODEL),
        out_shape=jax.ShapeDtypeStruct((N_ROWS, D_MODEL), F32),
        scratch_shapes=[
            pltpu.VMEM((TM, D_MODEL), BF16),
            pltpu.VMEM((TM, D_GMLP), F32),
            pltpu.VMEM((TM, D_GMLP), F32),
            pltpu.VMEM((TM, D_GMLP), BF16),
        ],
        compiler_params=_params(),
        name="gmlp",
    )(*xs, g_mix, mods, w_in, g_v, w_s, b_s, w_out)


EXT = TM + 2 * HALO


def _ext_pos(i, width):
    e = lax.broadcasted_iota(jnp.int32, (EXT, width), 0)
    e = jnp.where(e < TM + HALO, e, e - EXT)
    return (i * TM + e) & (_seq_len(i) - 1)


def _pool_kernel(x_ref, xp_ref, xn_ref, g_ref, mod_ref, pw_ref, pb_ref, ps_ref, o_ref, hext_ref):
    i = pl.program_id(0)
    g = g_ref[...]
    sc = mod_ref[1:2, :]
    sh = mod_ref[0:1, :]
    x = x_ref[...]
    hext_ref[0:TM, :] = _norm_mod(x, g, sc, sh)
    hext_ref[TM:TM + HALO, :] = _norm_mod(xn_ref[...], g, sc, sh)
    hext_ref[TM + HALO:EXT, :] = _norm_mod(xp_ref[...], g, sc, sh)
    seq = _seq_len(i)
    pos = _ext_pos(i, D_POOL_GROUP)
    p = (i * TM + lax.broadcasted_iota(jnp.int32, (TM, D_POOL_GROUP), 0)) & (seq - 1)
    for grp in range(N_POOL_GROUPS):
        sl = slice(grp * D_POOL_GROUP, (grp + 1) * D_POOL_GROUP)
        half = POOL_WINDOWS[grp] // 2
        he = hext_ref[:, sl]
        fwd = he
        bwd = jnp.where(pos >= 1, pltpu.roll(he, 1, 0), 0.0)
        span = 1
        while span < half:
            fwd = fwd + jnp.where(pos < seq - span, pltpu.roll(fwd, EXT - span, 0), 0.0)
            bwd = bwd + jnp.where(pos >= span, pltpu.roll(bwd, span, 0), 0.0)
            span *= 2
        cnt =(jnp.minimum(p + half - 1, seq - 1) - jnp.maximum(p - half, 0) + 1).astype(F32)
        d = (fwd[:TM, :] + bwd[:TM, :]) / cnt - he[:TM, :]
        y = (_dot(d.astype(BF16), pw_ref[grp]) + pb_ref[:, sl]) * ps_ref[:, sl]
        o_ref[:, sl] = x[:, sl] + mod_ref[2:3, sl] * y


def _pool_call(x, layer, mods, g_mix, p_w, p_b, p_scale):
    prev, nxt = _halo_specs(D_MODEL, HALO)
    return pl.pallas_call(
        _pool_kernel,
        grid=(NB,),
        in_specs=[
            _row_spec(D_MODEL), prev, nxt,
            _resident((1, D_MODEL)),
            _mod_spec(layer),
            _resident((N_POOL_GROUPS, D_POOL_GROUP, D_POOL_GROUP)),
            _resident((1, D_MODEL)),
            _resident((1, D_MODEL)),
        ],
        out_specs=_row_spec(D_MODEL),
        out_shape=jax.ShapeDtypeStruct((N_ROWS, D_MODEL), F32),
        scratch_shapes=[pltpu.VMEM((EXT, D_MODEL), F32)],
        compiler_params=_params(),
        name="pool",
    )(x, x, x, g_mix, mods, p_w, p_b, p_scale)


def _rope_tables():
    n_freq = HEAD_DIM // 4
    inv = ROPE_THETA ** (-np.arange(n_freq, dtype=np.float64) / n_freq)
    t = np.arange(DEC_SEQ)
    lane = np.arange(LANES) % HEAD_DIM
    quarter, f = lane // n_freq, lane % n_freq
    posn = np.where(quarter[None, :] < 2, (t // GRID_W)[:, None], (t % GRID_W)[:, None])
    ang = posn * inv[f][None, :]
    cos, sin = np.cos(ang), np.sin(ang)
    odd = (quarter % 2 == 1)[None, :]
    tab = np.concatenate([cos, np.where(odd, sin, 0.0), np.where(odd, 0.0, -sin)], axis=1)
    ident = np.concatenate([np.ones((TM, LANES)), np.zeros((TM, 2 * LANES))], axis=1)
    return np.concatenate([ident, tab], axis=0).astype(np.float32)


def _head_sumsq(t, bd_ref):
    sq = t * t
    hi = sq.astype(BF16)
    lo = (sq - hi.astype(F32)).astype(BF16)
    return _dot(hi, bd_ref[...]) + _dot(lo, bd_ref[...])


def _rope(t, tab_ref):
    q = HEAD_DIM // 4
    return (t * tab_ref[:, 0:LANES]
            + pltpu.roll(t, q, 1) * tab_ref[:, LANES:2 * LANES]
            + pltpu.roll(t, LANES - q, 1) * tab_ref[:, 2 * LANES:3 * LANES])


def _kv_ext(t, o_ref):
    lo = lax.broadcasted_iota(jnp.int32, (TM, LANES), 1) < HEAD_DIM
    for m in range(KV_DIM // LANES):
        a = t[:, m * LANES:(m + 1) * LANES]
        r = pltpu.roll(a, HEAD_DIM, 1)
        base = m * 4 * LANES
        o_ref[:, base:base + LANES] = jnp.where(lo, a, 0.0).astype(o_ref.dtype)
        o_ref[:, base + LANES:base + 2 * LANES] = jnp.where(lo, 0.0, r).astype(o_ref.dtype)
        o_ref[:, base + 2 * LANES:base + 3 * LANES] = jnp.where(lo, r, 0.0).astype(o_ref.dtype)
        o_ref[:, base + 3 * LANES:base + 4 * LANES] = jnp.where(lo, 0.0, a).astype(o_ref.dtype)


def _qkv_kernel(x_ref, g_ref, mod_ref, w_ref, gq_ref, gk_ref, bd_ref, tab_ref,
                q_ref, kx_ref, vx_ref, kn_ref, v_ref, h_ref, kr_ref):
    h_ref[...] = _norm_mod(x_ref[...], g_ref[...], mod_ref[1:2, :], mod_ref[0:1, :]).astype(BF16)
    scale = HEAD_DIM ** -0.5 * LOG2E
    for c in range(Q_DIM // TN):
        sl = slice(c * TN, (c + 1) * TN)
        t = _dot(h_ref[...], w_ref[:, sl])
        t = t * lax.rsqrt(_head_sumsq(t, bd_ref) * (1.0 / HEAD_DIM) + EPS) * (gq_ref[...] * scale)
        for m in range(TN // LANES):
            q_ref[:, c * TN + m * LANES:c * TN + (m + 1) * LANES] = _rope(
                t[:, m * LANES:(m + 1) * LANES], tab_ref).astype(BF16)
    t = _dot(h_ref[...], w_ref[:, Q_DIM:Q_DIM + KV_DIM])
    t = t * lax.rsqrt(_head_sumsq(t, bd_ref) * (1.0 / HEAD_DIM) + EPS) * gk_ref[...]
    is_ctx = pl.program_id(0) < NCB

    @pl.when(is_ctx)
    def _():
        kn_ref[...] = t

    for m in range(KV_DIM // LANES):
        kr_ref[:, m * LANES:(m + 1) * LANES] = _rope(t[:, m * LANES:(m + 1) * LANES], tab_ref)
    _kv_ext(kr_ref[...], kx_ref)
    v = _dot(h_ref[...], w_ref[:, Q_DIM + KV_DIM:Q_DIM + 2 * KV_DIM])

    @pl.when(is_ctx)
    def _():
        v_ref[...] = v

    _kv_ext(v, vx_ref)


def _qkv_call(x, layer, mods, g_mix, w_qkv, g_q, g_k, bd, tab):
    tab_spec = pl.BlockSpec((TM, 3 * LANES), lambda i: (jnp.where(i < NCB, 0, 1 + (i - NCB) % BPS), 0))
    return pl.pallas_call(
        _qkv_kernel,
        grid=(NB,),
        in_specs=[
            _row_spec(D_MODEL),
            _resident((1, D_MODEL)),
            _mod_spec(layer),
            _resident((D_MODEL, Q_DIM + 2 * KV_DIM)),
            _resident((1, TN)),
            _resident((1, TN)),
            _resident((TN, TN)),
            tab_spec,
        ],
        out_specs=[_row_spec(Q_DIM), _row_spec(KV_EXT), _row_spec(KV_EXT), _ctx_rows_spec(KV_DIM),
                   _ctx_rows_spec(KV_DIM)],
        out_shape=[
            jax.ShapeDtypeStruct((N_ROWS, Q_DIM), BF16),
            jax.ShapeDtypeStruct((N_ROWS, KV_EXT), BF16),
            jax.ShapeDtypeStruct((N_ROWS, KV_EXT), BF16),
            jax.ShapeDtypeStruct((N_CTX_ROWS, KV_DIM), F32),
            jax.ShapeDtypeStruct((N_CTX_ROWS, KV_DIM), F32),
        ],
        scratch_shapes=[pltpu.VMEM((TM, D_MODEL), BF16), pltpu.VMEM((TM, KV_DIM), F32)],
        compiler_params=_params(),
        name="qkv",
    )(x, g_mix, mods, w_qkv, g_q, g_k, bd, tab)


def _head_lanes(g, j):
    base = (g * 2 + j) * LANES
    return slice(base, base + LANES)


def _group_queries(q_ref, rows, g):
    return jnp.concatenate([q_ref[rows, _head_lanes(g, 0)], q_ref[rows, _head_lanes(g, 1)]], axis=0)


def _group_sinks(sink_ref, g, n):
    heads = (0, 2, 1, 3)
    return jnp.concatenate(
        [jnp.full((n, 1), sink_ref[g * Q_PER_KV + h] * LOG2E, F32) for h in heads], axis=0)


def _sink_softmax(pieces, sink_col):
    m = sink_col
    for s in pieces:
        m = jnp.maximum(m, jnp.max(s, axis=-1, keepdims=True))
    es = [jnp.exp2(s - m) for s in pieces]
    den = jnp.exp2(sink_col - m)
    for e in es:
        den = den + jnp.sum(e, axis=-1, keepdims=True)
    return es, 1.0 / den


def _store_group(a_ref, rows, g, pv, n):
    for pr in range(2):
        sub = slice(pr * n, (pr + 1) * n)
        a_ref[rows, _head_lanes(g, pr)] = (pv[0][sub, :] + pv[1][sub, :]).astype(BF16)


def _ctx_attn_kernel(sink_ref, x_ref, mod_ref, q_ref, kx_ref, vx_ref, wo_ref, o_ref, a_ref):
    for s in range(TM // SEQ):
        rows = slice(s * SEQ, (s + 1) * SEQ)
        for g in range(N_KV_HEADS):
            for pr in range(2):
                qp = q_ref[rows, _head_lanes(g, pr)]
                acc = jnp.zeros((SEQ, LANES), F32)
                for var in range(2):
                    sink = sink_ref[g * Q_PER_KV + 2 * pr + var] * LOG2E
                    sc = _dot_nt(qp, kx_ref[rows, _head_lanes(g, var)])
                    m = jnp.maximum(jnp.max(sc, axis=-1, keepdims=True), sink)
                    e = jnp.exp2(sc - m)
                    den = jnp.sum(e, axis=-1, keepdims=True) + jnp.exp2(sink - m)
                    acc = acc + _dot(e.astype(BF16), vx_ref[rows, _head_lanes(g, var)]) / den
                a_ref[rows, _head_lanes(g, pr)] = acc.astype(BF16)
    y = _dot(a_ref[...], wo_ref[...])
    o_ref[...] = x_ref[...] + mod_ref[2:3, :] * y


def _ctx_attn_call(x, layer, mods, sink, q, kx, vx, w_o):
    return pl.pallas_call(
        _ctx_attn_kernel,
        grid=(NCB,),
        in_specs=[
            pl.BlockSpec(memory_space=pltpu.SMEM),
            _row_spec(D_MODEL),
            _mod_spec(layer),
            _row_spec(Q_DIM), _row_spec(KV_EXT), _row_spec(KV_EXT),
            _resident((Q_DIM, D_MODEL)),
        ],
        out_specs=_row_spec(D_MODEL),
        out_shape=jax.ShapeDtypeStruct((N_ROWS, D_MODEL), F32),
        input_output_aliases={1: 0},
        scratch_shapes=[pltpu.VMEM((TM, Q_DIM), BF16)],
        compiler_params=_params(),
        name="ctx_attn",
    )(sink, x, mods, q, kx, vx, w_o)


N_QB = TM // BLOCK
KWIN = TM + 2 * BLOCK


def _smp_attn_kernel(sink_ref, x_ref, mod_ref, q_ref, kc_ref, kp_ref, kn_ref, vc_ref, vp_ref, vn_ref,
                     ck_ref, cv_ref, wo_ref, o_ref, a_ref, kall_ref, vall_ref):
    i = pl.program_id(0)
    kall_ref[0:BLOCK, :] = kp_ref[...]
    kall_ref[BLOCK:BLOCK + TM, :] = kc_ref[...]
    kall_ref[BLOCK + TM:KWIN, :] = kn_ref[...]
    vall_ref[0:BLOCK, :] = vp_ref[...]
    vall_ref[BLOCK:BLOCK + TM, :] = vc_ref[...]
    vall_ref[BLOCK + TM:KWIN, :] = vn_ref[...]
    first = (i % BPS) == 0
    last = (i % BPS) == BPS - 1
    qi = lax.broadcasted_iota(jnp.int32, (Q_PER_KV * BLOCK, BLOCK), 0) & (BLOCK - 1)
    kj = lax.broadcasted_iota(jnp.int32, (Q_PER_KV * BLOCK, BLOCK), 1)
    for b in range(N_QB):
        rows = slice(b * BLOCK, (b + 1) * BLOCK)
        win = slice(b * BLOCK, (b + 3) * BLOCK)
        ok_prev = kj >= (qi + jnp.where(first, BLOCK, 0) if b == 0 else qi)
        ok_next = kj <= (qi - jnp.where(last, BLOCK, 0) if b == N_QB - 1 else qi)
        for g in range(N_KV_HEADS):
            q2 = _group_queries(q_ref, rows, g)
            sw = jnp.concatenate([_dot_nt(q2, kall_ref[win, _head_lanes(g, var)]) for var in range(2)], axis=0)
            sc = jnp.concatenate([_dot_nt(q2, ck_ref[:, _head_lanes(g, var)]) for var in range(2)], axis=0)
            pieces = [jnp.where(ok_prev, sw[:, 0:BLOCK], NEG_INF), sw[:, BLOCK:2 * BLOCK],
                      jnp.where(ok_next, sw[:, 2 * BLOCK:3 * BLOCK], NEG_INF), sc]
            es, rinv = _sink_softmax(pieces, _group_sinks(sink_ref, g, BLOCK))
            ew = jnp.concatenate(es[:3], axis=-1).astype(BF16)
            ec = es[3].astype(BF16)
            pv = []
            for var in range(2):
                sub = slice(var * 2 * BLOCK, (var + 1) * 2 * BLOCK)
                lanes = _head_lanes(g, var)
                pv.append((_dot(ew[sub, :], vall_ref[win, lanes]) + _dot(ec[sub, :], cv_ref[:, lanes]))
                          * rinv[sub, :])
            _store_group(a_ref, rows, g, pv, BLOCK)
    y = _dot(a_ref[...], wo_ref[...])
    o_ref[...] = x_ref[...] + mod_ref[2:3, :] * y


def _smp_attn_call(x, layer, mods, sink, q, kx, vx, ckx, cvx, w_o):
    prev, nxt = _halo_specs(KV_EXT, BLOCK, offset=NCB)
    ctx_spec = pl.BlockSpec((None, PAST_LEN, KV_EXT), lambda i: (i // BPS, 0, 0))
    return pl.pallas_call(
        _smp_attn_kernel,
        grid=(NB - NCB,),
        in_specs=[
            pl.BlockSpec(memory_space=pltpu.SMEM),
            _row_spec(D_MODEL, NCB),
            _mod_spec(layer, NCB),
            _row_spec(Q_DIM, NCB),
            _row_spec(KV_EXT, NCB), prev, nxt,
            _row_spec(KV_EXT, NCB), prev, nxt,
            ctx_spec, ctx_spec,
            _resident((Q_DIM, D_MODEL)),
        ],
        out_specs=_row_spec(D_MODEL, NCB),
        out_shape=jax.ShapeDtypeStruct((N_ROWS, D_MODEL), F32),
        input_output_aliases={1: 0},
        scratch_shapes=[pltpu.VMEM((TM, Q_DIM), BF16), pltpu.VMEM((KWIN, KV_EXT), BF16),
                        pltpu.VMEM((KWIN, KV_EXT), BF16)],
        compiler_params=_params(),
        name="smp_attn",
    )(sink, x, mods, q, kx, kx, kx, vx, vx, vx, ckx, cvx, w_o)


def _ext_heads(t):
    lead = t.shape[:-1]
    h = t.reshape(lead + (N_KV_HEADS, 1, HEAD_DIM))
    z = jnp.zeros_like(h)
    e = jnp.concatenate([jnp.concatenate([h, z], axis=-1), jnp.concatenate([z, h], axis=-1)], axis=-2)
    return e.reshape(lead + (KV_EXT,))


FEXT = TM + 2 * HALO
N_FT = D_FF // TN


def _ffn_kernel(split_out, x_ref, xp_ref, xn_ref, g_ref, mod_ref, wa_ref, wu_ref, cw_ref, cb_ref, wo_ref, *refs):
    i = pl.program_id(0)
    hext_ref, gbuf_ref = refs[-2:]
    g = g_ref[...]
    sh = mod_ref[3:4, :]
    sc = mod_ref[4:5, :]
    x = x_ref[...]
    hext_ref[0:TM, :] = _norm_mod(x, g, sc, sh).astype(BF16)
    hext_ref[TM:TM + HALO, :] = _norm_mod(xn_ref[...], g, sc, sh).astype(BF16)
    hext_ref[TM + HALO:FEXT, :] = _norm_mod(xp_ref[...], g, sc, sh).astype(BF16)
    seq = _seq_len(i)
    pos = (i * TM + lax.broadcasted_iota(jnp.int32, (TM, TN), 0)) & (seq - 1)
    has_prev = pos != 0
    has_next = pos != seq - 1
    for j in range(N_FT):
        sl = slice(j * TN, (j + 1) * TN)
        a = _dot(hext_ref[...], wa_ref[:, sl])
        u = _dot(hext_ref[0:TM, :], wu_ref[:, sl])
        a_prev = jnp.where(has_prev, pltpu.roll(a, 1, 0)[:TM, :], 0.0)
        a_next = jnp.where(has_next, pltpu.roll(a, FEXT - 1, 0)[:TM, :], 0.0)
        a = a_prev * cw_ref[0:1, sl] + a[:TM, :] * cw_ref[1:2, sl] + a_next * cw_ref[2:3, sl] + cb_ref[:, sl]
        gbuf_ref[:, sl] = (_gelu(a) * u).astype(BF16)
    y = _dot(gbuf_ref[...], wo_ref[...])
    res = x + mod_ref[5:6, :] * y
    if split_out:
        oc_ref, os_ref = refs[:2]

        @pl.when(i < NCB)
        def _():
            oc_ref[...] = res

        @pl.when(i >= NCB)
        def _():
            os_ref[...] = res
    else:
        refs[0][...] = res


def _ffn_call(x, layer, mods, g_ffn, w_a, w_u, conv_w, conv_b, w_out, split_out):
    prev, nxt = _halo_specs(D_MODEL, HALO)
    if split_out:
        out_specs = [_ctx_rows_spec(D_MODEL), _smp_rows_spec(D_MODEL)]
        out_shape = [jax.ShapeDtypeStruct((N_CTX_ROWS, D_MODEL), F32),
                     jax.ShapeDtypeStruct((N_SMP_ROWS, D_MODEL), F32)]
    else:
        out_specs = _row_spec(D_MODEL)
        out_shape = jax.ShapeDtypeStruct((N_ROWS, D_MODEL), F32)
    return pl.pallas_call(
        functools.partial(_ffn_kernel, split_out),
        grid=(NB,),
        in_specs=[
            _row_spec(D_MODEL), prev, nxt,
            _resident((1, D_MODEL)),
            _mod_spec(layer),
            _resident((D_MODEL, D_FF)),
            _resident((D_MODEL, D_FF)),
            _resident((MOD_ROWS, D_FF)),
            _resident((1, D_FF)),
            _resident((D_FF, D_MODEL)),
        ],
        out_specs=out_specs,
        out_shape=out_shape,
        scratch_shapes=[pltpu.VMEM((FEXT, D_MODEL), BF16), pltpu.VMEM((TM, D_FF), BF16)],
        compiler_params=_params(),
        name="ffn",
    )(x, x, x, g_ffn, mods, w_a, w_u, conv_w, conv_b, w_out)


def kernel(x_prompt, x_sample, cache_k, cache_v, c, c_ctx, w_ada, b_ada, g_mix, g_ffn, w_ffn_in, ffn_conv_w,
           ffn_conv_b, w_ffn_out, a_w_in, a_g_v, a_w_s, a_b_s, a_w_out, p_w, p_b, p_scale, c_w_qkv, c_g_q,
           c_g_k, c_sink, c_w_o):
    x = (x_prompt.reshape(N_CTX_ROWS, D_MODEL), x_sample.reshape(N_SMP_ROWS, D_MODEL))

    cond8 = jnp.concatenate([c_ctx[None, :], c, jnp.zeros((MOD_ROWS - N_COND, D_MODEL), F32)], axis=0)
    ada = _ada_call(cond8, w_ada, b_ada)
    mods = ada[:, :N_COND, :].reshape(DEPTH, N_COND, 6, D_MODEL)
    mods = jnp.pad(mods, ((0, 0), (0, 0), (0, MOD_ROWS - 6), (0, 0))).reshape(DEPTH * N_COND, MOD_ROWS, D_MODEL)

    bd = jnp.asarray(np.kron(np.eye(TN // HEAD_DIM), np.ones((HEAD_DIM, HEAD_DIM))), dtype=BF16)
    tab = jnp.asarray(_rope_tables())
    new_k = new_v = None

    for i in range(DEPTH):
        kind, j = i % N_MIXERS, i // N_MIXERS
        gm = g_mix[i][None, :]
        if kind == 0:
            x = _gmlp_call(x, i, mods, gm, a_w_in[j].astype(BF16), a_g_v[j][None, :], a_w_s[j].astype(BF16),
                           a_b_s[j][:, :, None], a_w_out[j].astype(BF16))
        elif kind == 1:
            x = _pool_call(x, i, mods, gm, p_w[j].astype(BF16), p_b[j][None, :], p_scale[j][None, :])
        else:
            gq = jnp.tile(c_g_q[j], TN // HEAD_DIM)[None, :]
            gk = jnp.tile(c_g_k[j], TN // HEAD_DIM)[None, :]
            q, kx, vx, kn, v = _qkv_call(x, i, mods, gm, c_w_qkv[j].astype(BF16), gq, gk, bd, tab)
            new_k = kn.reshape(BATCH, 1, SEQ, N_KV_HEADS, HEAD_DIM)
            new_v = v.reshape(BATCH, 1, SEQ, N_KV_HEADS, HEAD_DIM)
            ckx = _ext_heads(cache_k[:, j].reshape(DEC_BATCH, PAST_LEN, KV_DIM)).astype(BF16)
            cvx = _ext_heads(cache_v[:, j].reshape(DEC_BATCH, PAST_LEN, KV_DIM)).astype(BF16)
            w_o = c_w_o[j].astype(BF16)
            x = _ctx_attn_call(x, i, mods, c_sink[j], q, kx, vx, w_o)
            x = _smp_attn_call(x, i, mods, c_sink[j], q, kx, vx, ckx, cvx, w_o)
        conv_w = jnp.pad(ffn_conv_w[i], ((0, MOD_ROWS - 3), (0, 0)))
        x = _ffn_call(x, i, mods, g_ffn[i][None, :], w_ffn_in[i][:, :D_FF].astype(BF16),
                      w_ffn_in[i][:, D_FF:].astype(BF16), conv_w, ffn_conv_b[i][None, :],
                      w_ffn_out[i].astype(BF16), split_out=(i == DEPTH - 1))

    y_prompt = x[0].reshape(BATCH, SEQ, D_MODEL)
    y_sample = x[1].reshape(DEC_BATCH, DEC_SEQ, D_MODEL)
    return (y_prompt, y_sample, new_k, new_v)
```

```python
import functools

import numpy as np
import jax
import jax.numpy as jnp
from jax import lax
from jax.experimental import pallas as pl
from jax.experimental.pallas import tpu as pltpu

D_MODEL = 1024
BATCH = 32
SEQ = 256
DEPTH = 4
DEC_BATCH = 2
DEC_SEQ = 2048
PAST_LEN = 256
GRID_W = 64
N_MIXERS = 3
CHUNK = 128
D_GMLP = 2 * D_MODEL
N_GROUPS_A = 8
D_GROUP_A = D_GMLP // N_GROUPS_A
POOL_WINDOWS = (2, 4, 8, 16)
N_POOL_GROUPS = 4
D_POOL_GROUP = D_MODEL // N_POOL_GROUPS
N_HEADS = 16
N_C_LAYERS = DEPTH // N_MIXERS
N_KV_HEADS = 4
HEAD_DIM = 64
Q_PER_KV = N_HEADS // N_KV_HEADS
WINDOW = 128
BLOCK = 128
ROPE_THETA = 10000.0
D_FF = 2816
CONV_W = 3
EPS = 1e-6
NEG_INF = -1e30
LOG2E = 1.4426950408889634

F32 = jnp.float32
BF16 = jnp.bfloat16

N_CTX_ROWS = BATCH * SEQ
N_SMP_ROWS = DEC_BATCH * DEC_SEQ
N_ROWS = N_CTX_ROWS + N_SMP_ROWS
N_COND = 1 + DEC_BATCH
N_MOD = 6
HALO = 16
SUBLANES = 8
LANES = 128
TN = 256
Q_DIM = N_HEADS * HEAD_DIM
KV_DIM = N_KV_HEADS * HEAD_DIM
KV_EXT = N_KV_HEADS * 2 * LANES
VMEM_LIMIT = 56 * 1024 * 1024


class _Rows:
    def __init__(self, tm):
        self.tm = tm
        self.nb = N_ROWS // tm
        self.ncb = N_CTX_ROWS // tm
        self.bps = DEC_SEQ // tm

    def cond_idx(self, i):
        return jnp.where(i < self.ncb, 0, 1 + (i - self.ncb) // self.bps)

    def seq_len(self, i):
        return jnp.where(i < self.ncb, SEQ, DEC_SEQ)

    def row_spec(self, width, offset=0):
        return pl.BlockSpec((self.tm, width), lambda i: (i + offset, 0))

    def ctx_rows_spec(self, width):
        return pl.BlockSpec((self.tm, width), lambda i: (jnp.minimum(i, self.ncb - 1), 0))

    def smp_rows_spec(self, width):
        return pl.BlockSpec((self.tm, width), lambda i: (jnp.maximum(i - self.ncb, 0), 0))

    def halo_specs(self, width, rows, offset=0):
        per = self.tm // rows
        last = N_ROWS // rows - 1
        prev = pl.BlockSpec((rows, width), lambda i: (jnp.maximum((i + offset) * per - 1, 0), 0))
        nxt = pl.BlockSpec((rows, width), lambda i: (jnp.minimum((i + offset + 1) * per, last), 0))
        return prev, nxt


ROWS_WIDE = _Rows(1024)
ROWS_NARROW = _Rows(512)


def _gelu(x):
    return 0.5 * x * (1.0 + jnp.tanh(0.7978845608028654 * (x + 0.044715 * (x * x * x))))


def _norm_mod(x, g, sc, sh):
    ms = jnp.mean(x * x, axis=-1, keepdims=True)
    return x * lax.rsqrt(ms + EPS) * g * (1.0 + sc) + sh


def _mod_row(mod_ref, cond, k):
    return mod_ref[pl.ds(cond, 1), k * D_MODEL:(k + 1) * D_MODEL]


def _dot(a, b):
    return jnp.dot(a, b, preferred_element_type=F32)


def _dot_nt(a, b):
    return lax.dot_general(a, b, (((1,), (1,)), ((), ())), preferred_element_type=F32)


def _params():
    return pltpu.CompilerParams(dimension_semantics=("arbitrary",), vmem_limit_bytes=VMEM_LIMIT)


def _resident(shape, layer=None):
    nd = len(shape)
    if layer is None:
        return pl.BlockSpec(shape, lambda *_: (0,) * nd, pipeline_mode=pl.Buffered(1))
    return pl.BlockSpec((None,) + tuple(shape), lambda *_: (layer,) + (0,) * nd, pipeline_mode=pl.Buffered(1))


def _mod_spec(layer):
    return _resident((SUBLANES, N_MOD * D_MODEL), layer)


ADA_TN = 1536


def _ada_kernel(c_ref, w_ref, b_ref, o_ref):
    c = c_ref[...]
    s = (c * (1.0 / (1.0 + jnp.exp(-c)))).astype(BF16)
    o_ref[...] = _dot(s, w_ref[...].astype(BF16)) + b_ref[...]


def _ada_call(cond8, w_ada, b_ada):
    nt = N_MOD * D_MODEL // ADA_TN
    return pl.pallas_call(
        _ada_kernel,
        grid=(DEPTH, nt),
        in_specs=[
            pl.BlockSpec((SUBLANES, D_MODEL), lambda l, j: (0, 0)),
            pl.BlockSpec((None, D_MODEL, ADA_TN), lambda l, j: (l, 0, j)),
            pl.BlockSpec((None, 1, ADA_TN), lambda l, j: (l, 0, j)),
        ],
        out_specs=pl.BlockSpec((None, SUBLANES, ADA_TN), lambda l, j: (l, 0, j)),
        out_shape=jax.ShapeDtypeStruct((DEPTH, SUBLANES, N_MOD * D_MODEL), F32),
        compiler_params=pltpu.CompilerParams(dimension_semantics=("arbitrary", "arbitrary"),
                                             vmem_limit_bytes=VMEM_LIMIT),
        name="ada",
    )(cond8, w_ada, b_ada.reshape(DEPTH, 1, N_MOD * D_MODEL))


def _gmlp_kernel(geo, split_in, *refs):
    i = pl.program_id(0)
    tm = geo.tm
    if split_in:
        xc_ref, xs_ref, *refs = refs
        x = jnp.where(i < geo.ncb, xc_ref[...], xs_ref[...])
    else:
        x_ref, *refs = refs
        x = x_ref[...]
    g_ref, mod_ref, win_ref, gv_ref, ws_ref, bs_ref, wout_ref, o_ref, h_ref, u_ref, v_ref, p_ref = refs
    cond = geo.cond_idx(i)
    h_ref[...] = _norm_mod(x, g_ref[...], _mod_row(mod_ref, cond, 1), _mod_row(mod_ref, cond, 0)).astype(BF16)
    ssq = jnp.zeros((tm, 1), F32)
    for j in range(D_GMLP // TN):
        sl = slice(j * TN, (j + 1) * TN)
        u_ref[:, sl] = _gelu(_dot(h_ref[...], win_ref[:, sl]))
        v = _gelu(_dot(h_ref[...], win_ref[:, D_GMLP + j * TN:D_GMLP + (j + 1) * TN]))
        v_ref[:, sl] = v
        ssq = ssq + jnp.sum(v * v, axis=-1, keepdims=True)
    r = lax.rsqrt(ssq * (1.0 / D_GMLP) + EPS)
    for g in range(N_GROUPS_A):
        sl = slice(g * D_GROUP_A, (g + 1) * D_GROUP_A)
        vn = (v_ref[:, sl] * r * gv_ref[:, sl]).astype(BF16)
        for c in range(tm // CHUNK):
            rows = slice(c * CHUNK, (c + 1) * CHUNK)
            sv = _dot(ws_ref[g], vn[rows, :]) + bs_ref[g]
            p_ref[rows, sl] = (u_ref[rows, sl] * sv).astype(BF16)
    y = _dot(p_ref[...], wout_ref[...])
    o_ref[...] = x + _mod_row(mod_ref, cond, 2) * y


def _gmlp_call(x, layer, j, mods, g_mix, w_in, g_v, w_s, b_s, w_out):
    split_in = isinstance(x, tuple)
    geo = ROWS_NARROW if split_in else ROWS_WIDE
    if split_in:
        x_specs = [geo.ctx_rows_spec(D_MODEL), geo.smp_rows_spec(D_MODEL)]
        xs = x
    else:
        x_specs = [geo.row_spec(D_MODEL)]
        xs = (x,)
    return pl.pallas_call(
        functools.partial(_gmlp_kernel, geo, split_in),
        grid=(geo.nb,),
        in_specs=x_specs + [
            _resident((1, D_MODEL), layer),
            _mod_spec(layer),
            _resident((D_MODEL, 2 * D_GMLP), j),
            _resident((1, D_GMLP), j),
            _resident((N_GROUPS_A, CHUNK, CHUNK), j),
            _resident((N_GROUPS_A, CHUNK, 1), j),
            _resident((D_GMLP, D_MODEL), j),
        ],
        out_specs=geo.row_spec(D_MODEL),
        out_shape=jax.ShapeDtypeStruct((N_ROWS, D_MODEL), F32),
        scratch_shapes=[
            pltpu.VMEM((geo.tm, D_MODEL), BF16),
            pltpu.VMEM((geo.tm, D_GMLP), F32),
            pltpu.VMEM((geo.tm, D_GMLP), F32),
            pltpu.VMEM((geo.tm, D_GMLP), BF16),
        ],
        compiler_params=_params(),
        name="gmlp",
    )(*xs, g_mix, mods, w_in, g_v, w_s, b_s, w_out)


def _pool_kernel(geo, x_ref, xp_ref, xn_ref, g_ref, mod_ref, pw_ref, pb_ref, ps_ref, o_ref, hext_ref):
    i = pl.program_id(0)
    tm = geo.tm
    ext = tm + 2 * HALO
    cond = geo.cond_idx(i)
    g = g_ref[...]
    sc = _mod_row(mod_ref, cond, 1)
    sh = _mod_row(mod_ref, cond, 0)
    gate = _mod_row(mod_ref, cond, 2)
    x = x_ref[...]
    hext_ref[0:tm, :] = _norm_mod(x, g, sc, sh)
    hext_ref[tm:tm + HALO, :] = _norm_mod(xn_ref[...], g, sc, sh)
    hext_ref[tm + HALO:ext, :] = _norm_mod(xp_ref[...], g, sc, sh)
    seq = geo.seq_len(i)
    e = lax.broadcasted_iota(jnp.int32, (ext, D_POOL_GROUP), 0)
    pos = (i * tm + jnp.where(e < tm + HALO, e, e - ext)) & (seq - 1)
    p = (i * tm + lax.broadcasted_iota(jnp.int32, (tm, D_POOL_GROUP), 0)) & (seq - 1)
    for grp in range(N_POOL_GROUPS):
        sl = slice(grp * D_POOL_GROUP, (grp + 1) * D_POOL_GROUP)
        half = POOL_WINDOWS[grp] // 2
        he = hext_ref[:, sl]
        fwd = he
        bwd = jnp.where(pos >= 1, pltpu.roll(he, 1, 0), 0.0)
        span = 1
        while span < half:
            fwd = fwd + jnp.where(pos < seq - span, pltpu.roll(fwd, ext - span, 0), 0.0)
            bwd = bwd + jnp.where(pos >= span, pltpu.roll(bwd, span, 0), 0.0)
            span *= 2
        cnt = (jnp.minimum(p + half - 1, seq - 1) - jnp.maximum(p - half, 0) + 1).astype(F32)
        d = (fwd[:tm, :] + bwd[:tm, :]) / cnt - he[:tm, :]
        y = (_dot(d.astype(BF16), pw_ref[grp]) + pb_ref[:, sl]) * ps_ref[:, sl]
        o_ref[:, sl] = x[:, sl] + gate[:, sl] * y


def _pool_call(x, layer, j, mods, g_mix, p_w, p_b, p_scale):
    geo = ROWS_WIDE
    prev, nxt = geo.halo_specs(D_MODEL, HALO)
    return pl.pallas_call(
        functools.partial(_pool_kernel, geo),
        grid=(geo.nb,),
        in_specs=[
            geo.row_spec(D_MODEL), prev, nxt,
            _resident((1, D_MODEL), layer),
            _mod_spec(layer),
            _resident((N_POOL_GROUPS, D_POOL_GROUP, D_POOL_GROUP), j),
            _resident((1, D_MODEL), j),
            _resident((1, D_MODEL), j),
        ],
        out_specs=geo.row_spec(D_MODEL),
        out_shape=jax.ShapeDtypeStruct((N_ROWS, D_MODEL), F32),
        scratch_shapes=[pltpu.VMEM((geo.tm + 2 * HALO, D_MODEL), F32)],
        compiler_params=_params(),
        name="pool",
    )(x, x, x, g_mix, mods, p_w, p_b, p_scale)


def _rope_tables(tm):
    n_freq = HEAD_DIM // 4
    inv = ROPE_THETA ** (-np.arange(n_freq, dtype=np.float64) / n_freq)
    t = np.arange(DEC_SEQ)
    lane = np.arange(LANES) % HEAD_DIM
    quarter, f = lane // n_freq, lane % n_freq
    posn = np.where(quarter[None, :] < 2, (t // GRID_W)[:, None], (t % GRID_W)[:, None])
    ang = posn * inv[f][None, :]
    cos, sin = np.cos(ang), np.sin(ang)
    odd = (quarter % 2 == 1)[None, :]
    tab = np.concatenate([cos, np.where(odd, sin, 0.0), np.where(odd, 0.0, -sin)], axis=1)
    ident = np.concatenate([np.ones((tm, LANES)), np.zeros((tm, 2 * LANES))], axis=1)
    return np.concatenate([ident, tab], axis=0).astype(np.float32)


def _head_sumsq(t, bd_ref):
    sq = t * t
    hi = sq.astype(BF16)
    lo = (sq - hi.astype(F32)).astype(BF16)
    return _dot(hi, bd_ref[...]) + _dot(lo, bd_ref[...])


def _rope(t, tab_ref):
    q = HEAD_DIM // 4
    return (t * tab_ref[:, 0:LANES]
            + pltpu.roll(t, q, 1) * tab_ref[:, LANES:2 * LANES]
            + pltpu.roll(t, LANES - q, 1) * tab_ref[:, 2 * LANES:3 * LANES])


def _kv_ext(t, o_ref):
    lo = lax.broadcasted_iota(jnp.int32, (t.shape[0], LANES), 1) < HEAD_DIM
    for m in range(KV_DIM // LANES):
        a = t[:, m * LANES:(m + 1) * LANES]
        r = pltpu.roll(a, HEAD_DIM, 1)
        base = m * 4 * LANES
        o_ref[:, base:base + LANES] = jnp.where(lo, a, 0.0).astype(o_ref.dtype)
        o_ref[:, base + LANES:base + 2 * LANES] = jnp.where(lo, 0.0, r).astype(o_ref.dtype)
        o_ref[:, base + 2 * LANES:base + 3 * LANES] = jnp.where(lo, r, 0.0).astype(o_ref.dtype)
        o_ref[:, base + 3 * LANES:base + 4 * LANES] = jnp.where(lo, 0.0, a).astype(o_ref.dtype)


def _qkv_kernel(geo, x_ref, g_ref, mod_ref, w_ref, gq_ref, gk_ref, bd_ref, tab_ref,
                q_ref, kx_ref, vx_ref, kn_ref, v_ref, h_ref, kr_ref):
    i = pl.program_id(0)
    cond = geo.cond_idx(i)
    h_ref[...] = _norm_mod(x_ref[...], g_ref[...], _mod_row(mod_ref, cond, 1),
                           _mod_row(mod_ref, cond, 0)).astype(BF16)
    scale = HEAD_DIM ** -0.5 * LOG2E
    for c in range(Q_DIM // TN):
        sl = slice(c * TN, (c + 1) * TN)
        t = _dot(h_ref[...], w_ref[:, sl])
        t = t * lax.rsqrt(_head_sumsq(t, bd_ref) * (1.0 / HEAD_DIM) + EPS) * (gq_ref[...] * scale)
        for m in range(TN // LANES):
            q_ref[:, c * TN + m * LANES:c * TN + (m + 1) * LANES] = _rope(
                t[:, m * LANES:(m + 1) * LANES], tab_ref).astype(BF16)
    t = _dot(h_ref[...], w_ref[:, Q_DIM:Q_DIM + KV_DIM])
    t = t * lax.rsqrt(_head_sumsq(t, bd_ref) * (1.0 / HEAD_DIM) + EPS) * gk_ref[...]
    is_ctx = i < geo.ncb

    @pl.when(is_ctx)
    def _():
        kn_ref[...] = t

    for m in range(KV_DIM // LANES):
        kr_ref[:, m * LANES:(m + 1) * LANES] = _rope(t[:, m * LANES:(m + 1) * LANES], tab_ref)
    _kv_ext(kr_ref[...], kx_ref)
    v = _dot(h_ref[...], w_ref[:, Q_DIM + KV_DIM:Q_DIM + 2 * KV_DIM])

    @pl.when(is_ctx)
    def _():
        v_ref[...] = v

    _kv_ext(v, vx_ref)


def _qkv_call(x, layer, j, mods, g_mix, w_qkv, g_q, g_k, bd):
    geo = ROWS_WIDE
    tab = jnp.asarray(_rope_tables(geo.tm))
    tab_spec = pl.BlockSpec((geo.tm, 3 * LANES),
                            lambda i: (jnp.where(i < geo.ncb, 0, 1 + (i - geo.ncb) % geo.bps), 0))
    return pl.pallas_call(
        functools.partial(_qkv_kernel, geo),
        grid=(geo.nb,),
        in_specs=[
            geo.row_spec(D_MODEL),
            _resident((1, D_MODEL), layer),
            _mod_spec(layer),
            _resident((D_MODEL, Q_DIM + 2 * KV_DIM), j),
            _resident((1, TN), j),
            _resident((1, TN), j),
            _resident((TN, TN)),
            tab_spec,
        ],
        out_specs=[geo.row_spec(Q_DIM), geo.row_spec(KV_EXT), geo.row_spec(KV_EXT), geo.ctx_rows_spec(KV_DIM),
                   geo.ctx_rows_spec(KV_DIM)],
        out_shape=[
            jax.ShapeDtypeStruct((N_ROWS, Q_DIM), BF16),
            jax.ShapeDtypeStruct((N_ROWS, KV_EXT), BF16),
            jax.ShapeDtypeStruct((N_ROWS, KV_EXT), BF16),
            jax.ShapeDtypeStruct((N_CTX_ROWS, KV_DIM), F32),
            jax.ShapeDtypeStruct((N_CTX_ROWS, KV_DIM), F32),
        ],
        scratch_shapes=[pltpu.VMEM((geo.tm, D_MODEL), BF16), pltpu.VMEM((geo.tm, KV_DIM), F32)],
        compiler_params=_params(),
        name="qkv",
    )(x, g_mix, mods, w_qkv, g_q, g_k, bd, tab)


def _head_lanes(g, j):
    base = (g * 2 + j) * LANES
    return slice(base, base + LANES)


def _ctx_attn_kernel(geo, j, sink_ref, x_ref, mod_ref, q_ref, kx_ref, vx_ref, wo_ref, o_ref, a_ref):
    for s in range(geo.tm // SEQ):
        rows = slice(s * SEQ, (s + 1) * SEQ)
        for g in range(N_KV_HEADS):
            for pr in range(2):
                qp = q_ref[rows, _head_lanes(g, pr)]
                acc = jnp.zeros((SEQ, LANES), F32)
                for var in range(2):
                    sink = sink_ref[j * N_HEADS + g * Q_PER_KV + 2 * pr + var] * LOG2E
                    sc = _dot_nt(qp, kx_ref[rows, _head_lanes(g, var)])
                    m = jnp.maximum(jnp.max(sc, axis=-1, keepdims=True), sink)
                    e = jnp.exp2(sc - m)
                    den = jnp.sum(e, axis=-1, keepdims=True) + jnp.exp2(sink - m)
                    acc = acc + _dot(e.astype(BF16), vx_ref[rows, _head_lanes(g, var)]) / den
                a_ref[rows, _head_lanes(g, pr)] = acc.astype(BF16)
    y = _dot(a_ref[...], wo_ref[...])
    o_ref[...] = x_ref[...] + _mod_row(mod_ref, 0, 2) * y


def _ctx_attn_call(x, layer, j, mods, sink, q, kx, vx, w_o):
    geo = ROWS_WIDE
    return pl.pallas_call(
        functools.partial(_ctx_attn_kernel, geo, j),
        grid=(geo.ncb,),
        in_specs=[
            pl.BlockSpec(memory_space=pltpu.SMEM),
            geo.row_spec(D_MODEL),
            _mod_spec(layer),
            geo.row_spec(Q_DIM), geo.row_spec(KV_EXT), geo.row_spec(KV_EXT),
            _resident((Q_DIM, D_MODEL), j),
        ],
        out_specs=geo.row_spec(D_MODEL),
        out_shape=jax.ShapeDtypeStruct((N_ROWS, D_MODEL), F32),
        input_output_aliases={1: 0},
        scratch_shapes=[pltpu.VMEM((geo.tm, Q_DIM), BF16)],
        compiler_params=_params(),
        name="ctx_attn",
    )(sink, x, mods, q, kx, vx, w_o)


def _group_queries(q_ref, rows, g):
    return jnp.concatenate([q_ref[rows, _head_lanes(g, 0)], q_ref[rows, _head_lanes(g, 1)]], axis=0)


def _group_sinks(sink_ref, j, g, n):
    heads = (0, 2, 1, 3)
    return jnp.concatenate(
        [jnp.full((n, 1), sink_ref[j * N_HEADS + g * Q_PER_KV + h] * LOG2E, F32) for h in heads], axis=0)


def _sink_softmax(pieces, sink_col):
    m = sink_col
    for s in pieces:
        m = jnp.maximum(m, jnp.max(s, axis=-1, keepdims=True))
    es = [jnp.exp2(s - m) for s in pieces]
    den = jnp.exp2(sink_col - m)
    for e in es:
        den = den + jnp.sum(e, axis=-1, keepdims=True)
    return es, 1.0 / den


def _store_group(a_ref, rows, g, pv, n):
    for pr in range(2):
        sub = slice(pr * n, (pr + 1) * n)
        a_ref[rows, _head_lanes(g, pr)] = (pv[0][sub, :] + pv[1][sub, :]).astype(BF16)


def _smp_attn_kernel(geo, j, sink_ref, x_ref, mod_ref, q_ref, kc_ref, kp_ref, kn_ref, vc_ref, vp_ref, vn_ref,
                     ck_ref, cv_ref, wo_ref, o_ref, a_ref, kall_ref, vall_ref):
    i = pl.program_id(0)
    tm = geo.tm
    n_qb = tm // BLOCK
    kwin = tm + 2 * BLOCK
    kall_ref[0:BLOCK, :] = kp_ref[...]
    kall_ref[BLOCK:BLOCK + tm, :] = kc_ref[...]
    kall_ref[BLOCK + tm:kwin, :] = kn_ref[...]
    vall_ref[0:BLOCK, :] = vp_ref[...]
    vall_ref[BLOCK:BLOCK + tm, :] = vc_ref[...]
    vall_ref[BLOCK + tm:kwin, :] = vn_ref[...]
    first = (i % geo.bps) == 0
    last = (i % geo.bps) == geo.bps - 1
    qi = lax.broadcasted_iota(jnp.int32, (Q_PER_KV * BLOCK, BLOCK), 0) & (BLOCK - 1)
    kj = lax.broadcasted_iota(jnp.int32, (Q_PER_KV * BLOCK, BLOCK), 1)
    for b in range(n_qb):
        rows = slice(b * BLOCK, (b + 1) * BLOCK)
        win = slice(b * BLOCK, (b + 3) * BLOCK)
        ok_prev = kj >= (qi + jnp.where(first, BLOCK, 0) if b == 0 else qi)
        ok_next = kj <= (qi - jnp.where(last, BLOCK, 0) if b == n_qb - 1 else qi)
        for g in range(N_KV_HEADS):
            q2 = _group_queries(q_ref, rows, g)
            sw = jnp.concatenate([_dot_nt(q2, kall_ref[win, _head_lanes(g, var)]) for var in range(2)], axis=0)
            sc = jnp.concatenate([_dot_nt(q2, ck_ref[:, _head_lanes(g, var)]) for var in range(2)], axis=0)
            pieces = [jnp.where(ok_prev, sw[:, 0:BLOCK], NEG_INF), sw[:, BLOCK:2 * BLOCK],
                      jnp.where(ok_next, sw[:, 2 * BLOCK:3 * BLOCK], NEG_INF), sc]
            es, rinv = _sink_softmax(pieces, _group_sinks(sink_ref, j, g, BLOCK))
            ew = jnp.concatenate(es[:3], axis=-1).astype(BF16)
            ec = es[3].astype(BF16)
            pv = []
            for var in range(2):
                sub = slice(var * 2 * BLOCK, (var + 1) * 2 * BLOCK)
                lanes = _head_lanes(g, var)
                pv.append((_dot(ew[sub, :], vall_ref[win, lanes]) + _dot(ec[sub, :], cv_ref[:, lanes]))
                          * rinv[sub, :])
            _store_group(a_ref, rows, g, pv, BLOCK)
    y = _dot(a_ref[...], wo_ref[...])
    o_ref[...] = x_ref[...] + _mod_row(mod_ref, 1 + i // geo.bps, 2) * y


def _smp_attn_call(x, layer, j, mods, sink, q, kx, vx, ckx, cvx, w_o):
    geo = ROWS_NARROW
    prev, nxt = geo.halo_specs(KV_EXT, BLOCK, offset=geo.ncb)
    ctx_spec = pl.BlockSpec((None, None, PAST_LEN, KV_EXT), lambda i: (i // geo.bps, j, 0, 0))
    kwin = geo.tm + 2 * BLOCK
    return pl.pallas_call(
        functools.partial(_smp_attn_kernel, geo, j),
        grid=(geo.nb - geo.ncb,),
        in_specs=[
            pl.BlockSpec(memory_space=pltpu.SMEM),
            geo.row_spec(D_MODEL, geo.ncb),
            _mod_spec(layer),
            geo.row_spec(Q_DIM, geo.ncb),
            geo.row_spec(KV_EXT, geo.ncb), prev, nxt,
            geo.row_spec(KV_EXT, geo.ncb), prev, nxt,
            ctx_spec, ctx_spec,
            _resident((Q_DIM, D_MODEL), j),
        ],
        out_specs=geo.row_spec(D_MODEL, geo.ncb),
        out_shape=jax.ShapeDtypeStruct((N_ROWS, D_MODEL), F32),
        input_output_aliases={1: 0},
        scratch_shapes=[pltpu.VMEM((geo.tm, Q_DIM), BF16), pltpu.VMEM((kwin, KV_EXT), BF16),
                        pltpu.VMEM((kwin, KV_EXT), BF16)],
        compiler_params=_params(),
        name="smp_attn",
    )(sink, x, mods, q, kx, kx, kx, vx, vx, vx, ckx, cvx, w_o)


def _ext_heads(t):
    lead = t.shape[:-1]
    h = t.reshape(lead + (N_KV_HEADS, 1, HEAD_DIM))
    z = jnp.zeros_like(h)
    e = jnp.concatenate([jnp.concatenate([h, z], axis=-1), jnp.concatenate([z, h], axis=-1)], axis=-2)
    return e.reshape(lead + (KV_EXT,))


N_FT = D_FF // TN


def _ffn_kernel(geo, split_out, x_ref, xp_ref, xn_ref, g_ref, mod_ref, win_ref, cw_ref, cb_ref, wo_ref, *refs):
    i = pl.program_id(0)
    tm = geo.tm
    ext = tm + 2 * HALO
    hext_ref, gbuf_ref = refs[-2:]
    cond = geo.cond_idx(i)
    g = g_ref[...]
    sh = _mod_row(mod_ref, cond, 3)
    sc = _mod_row(mod_ref, cond, 4)
    x = x_ref[...]
    hext_ref[0:tm, :] = _norm_mod(x, g, sc, sh).astype(BF16)
    hext_ref[tm:tm + HALO, :] = _norm_mod(xn_ref[...], g, sc, sh).astype(BF16)
    hext_ref[tm + HALO:ext, :] = _norm_mod(xp_ref[...], g, sc, sh).astype(BF16)
    seq = geo.seq_len(i)
    pos = (i * tm + lax.broadcasted_iota(jnp.int32, (tm, TN), 0)) & (seq - 1)
    has_prev = pos != 0
    has_next = pos != seq - 1
    for j in range(N_FT):
        sl = slice(j * TN, (j + 1) * TN)
        a = _dot(hext_ref[...], win_ref[:, sl])
        u = _dot(hext_ref[0:tm, :], win_ref[:, D_FF + j * TN:D_FF + (j + 1) * TN])
        a_prev = jnp.where(has_prev, pltpu.roll(a, 1, 0)[:tm, :], 0.0)
        a_next = jnp.where(has_next, pltpu.roll(a, ext - 1, 0)[:tm, :], 0.0)
        a = a_prev * cw_ref[0:1, sl] + a[:tm, :] * cw_ref[1:2, sl] + a_next * cw_ref[2:3, sl] + cb_ref[:, sl]
        gbuf_ref[:, sl] = (_gelu(a) * u).astype(BF16)
    y = _dot(gbuf_ref[...], wo_ref[...])
    res = x + _mod_row(mod_ref, cond, 5) * y
    if split_out:
        oc_ref, os_ref = refs[:2]

        @pl.when(i < geo.ncb)
        def _():
            oc_ref[...] = res

        @pl.when(i >= geo.ncb)
        def _():
            os_ref[...] = res
    else:
        refs[0][...] = res


def _ffn_call(x, layer, mods, g_ffn, w_in, conv_w, conv_b, w_out, split_out):
    geo = ROWS_WIDE
    prev, nxt = geo.halo_specs(D_MODEL, HALO)
    if split_out:
        out_specs = [geo.ctx_rows_spec(D_MODEL), geo.smp_rows_spec(D_MODEL)]
        out_shape = [jax.ShapeDtypeStruct((N_CTX_ROWS, D_MODEL), F32),
                     jax.ShapeDtypeStruct((N_SMP_ROWS, D_MODEL), F32)]
    else:
        out_specs = geo.row_spec(D_MODEL)
        out_shape = jax.ShapeDtypeStruct((N_ROWS, D_MODEL), F32)
    return pl.pallas_call(
        functools.partial(_ffn_kernel, geo, split_out),
        grid=(geo.nb,),
        in_specs=[
            geo.row_spec(D_MODEL), prev, nxt,
            _resident((1, D_MODEL), layer),
            _mod_spec(layer),
            _resident((D_MODEL, 2 * D_FF), layer),
            _resident((CONV_W, D_FF), layer),
            _resident((1, D_FF), layer),
            _resident((D_FF, D_MODEL), layer),
        ],
        out_specs=out_specs,
        out_shape=out_shape,
        scratch_shapes=[pltpu.VMEM((geo.tm + 2 * HALO, D_MODEL), BF16), pltpu.VMEM((geo.tm, D_FF), BF16)],
        compiler_params=_params(),
        name="ffn",
    )(x, x, x, g_ffn, mods, w_in, conv_w, conv_b, w_out)


def kernel(x_prompt, x_sample, cache_k, cache_v, c, c_ctx, w_ada, b_ada, g_mix, g_ffn, w_ffn_in, ffn_conv_w,
           ffn_conv_b, w_ffn_out, a_w_in, a_g_v, a_w_s, a_b_s, a_w_out, p_w, p_b, p_scale, c_w_qkv, c_g_q,
           c_g_k, c_sink, c_w_o):
    x = (x_prompt.reshape(N_CTX_ROWS, D_MODEL), x_sample.reshape(N_SMP_ROWS, D_MODEL))

    cond8 = jnp.concatenate([c_ctx[None, :], c, jnp.zeros((SUBLANES - N_COND, D_MODEL), F32)], axis=0)
    mods = _ada_call(cond8, w_ada, b_ada)

    g_mix, g_ffn = g_mix[:, None, :], g_ffn[:, None, :]
    w_ffn_in, w_ffn_out = w_ffn_in.astype(BF16), w_ffn_out.astype(BF16)
    ffn_conv_b = ffn_conv_b[:, None, :]
    a_w_in, a_w_s, a_w_out = a_w_in.astype(BF16), a_w_s.astype(BF16), a_w_out.astype(BF16)
    a_g_v, a_b_s = a_g_v[:, None, :], a_b_s[:, :, :, None]
    p_w, p_b, p_scale = p_w.astype(BF16), p_b[:, None, :], p_scale[:, None, :]
    c_w_qkv, c_w_o = c_w_qkv.astype(BF16), c_w_o.astype(BF16)
    gq = jnp.tile(c_g_q, (1, TN // HEAD_DIM))[:, None, :]
    gk = jnp.tile(c_g_k, (1, TN // HEAD_DIM))[:, None, :]
    bd = jnp.asarray(np.kron(np.eye(TN // HEAD_DIM), np.ones((HEAD_DIM, HEAD_DIM))), dtype=BF16)
    sink = c_sink.reshape(N_C_LAYERS * N_HEADS)
    ckx = _ext_heads(cache_k.reshape(DEC_BATCH, N_C_LAYERS, PAST_LEN, KV_DIM)).astype(BF16)
    cvx = _ext_heads(cache_v.reshape(DEC_BATCH, N_C_LAYERS, PAST_LEN, KV_DIM)).astype(BF16)
    new_k = new_v = None

    for i in range(DEPTH):
        kind, j = i % N_MIXERS, i // N_MIXERS
        if kind == 0:
            x = _gmlp_call(x, i, j, mods, g_mix, a_w_in, a_g_v, a_w_s, a_b_s, a_w_out)
        elif kind == 1:
            x = _pool_call(x, i, j, mods, g_mix, p_w, p_b, p_scale)
        else:
            q, kx, vx, kn, v = _qkv_call(x, i, j, mods, g_mix, c_w_qkv, gq, gk, bd)
            new_k = kn.reshape(BATCH, 1, SEQ, N_KV_HEADS, HEAD_DIM)
            new_v = v.reshape(BATCH, 1, SEQ, N_KV_HEADS, HEAD_DIM)
            x = _ctx_attn_call(x, i, j, mods, sink, q, kx, vx, c_w_o)
            x = _smp_attn_call(x, i, j, mods, sink, q, kx, vx, ckx, cvx, c_w_o)
        x = _ffn_call(x, i, mods, g_ffn, w_ffn_in, ffn_conv_w, ffn_conv_b, w_ffn_out, split_out=(i == DEPTH - 1))

    y_prompt = x[0].reshape(BATCH, SEQ, D_MODEL)
    y_sample = x[1].reshape(DEC_BATCH, DEC_SEQ, D_MODEL)
    return (y_prompt, y_sample, new_k, new_v)
```

```python
import functools
from typing import NamedTuple

import numpy as np
import jax
import jax.numpy as jnp
from jax import lax
from jax.experimental import pallas as pl
from jax.experimental.pallas import tpu as pltpu

D_MODEL = 1024
BATCH = 32
SEQ = 256
DEPTH = 4
DEC_BATCH = 2
DEC_SEQ = 2048
PAST_LEN = 256
GRID_W = 64
N_MIXERS = 3
CHUNK = 128
D_GMLP = 2 * D_MODEL
N_GROUPS_A = 8
D_GROUP_A = D_GMLP // N_GROUPS_A
POOL_WINDOWS = (2, 4, 8, 16)
N_POOL_GROUPS = 4
D_POOL_GROUP = D_MODEL // N_POOL_GROUPS
N_HEADS = 16
N_C_LAYERS = DEPTH // N_MIXERS
N_KV_HEADS = 4
HEAD_DIM = 64
Q_PER_KV = N_HEADS // N_KV_HEADS
WINDOW = 128
BLOCK = 128
ROPE_THETA = 10000.0
D_FF = 2816
CONV_W = 3
EPS = 1e-6
NEG_INF = -1e30
LOG2E = 1.4426950408889634

F32 = jnp.float32
BF16 = jnp.bfloat16

N_CTX_ROWS = BATCH * SEQ
N_SMP_ROWS = DEC_BATCH * DEC_SEQ
N_ROWS = N_CTX_ROWS + N_SMP_ROWS
N_COND = 1 + DEC_BATCH
N_MOD = 6
HALO = 16
SUBLANES = 8
LANES = 128
TN = 256
Q_DIM = N_HEADS * HEAD_DIM
KV_DIM = N_KV_HEADS * HEAD_DIM
KV_EXT = N_KV_HEADS * 2 * LANES
VMEM_LIMIT = 56 * 1024 * 1024


class _Rows:
    def __init__(self, tm):
        self.tm = tm
        self.nb = N_ROWS // tm
        self.ncb = N_CTX_ROWS // tm
        self.bps = DEC_SEQ // tm

    def cond_idx(self, i):
        return jnp.where(i < self.ncb, 0, 1 + (i - self.ncb) // self.bps)

    def seq_len(self, i):
        return jnp.where(i < self.ncb, SEQ, DEC_SEQ)

    def row_spec(self, width, offset=0):
        return pl.BlockSpec((self.tm, width), lambda i: (i + offset, 0))

    def ctx_rows_spec(self, width):
        return pl.BlockSpec((self.tm, width), lambda i: (jnp.minimum(i, self.ncb - 1), 0))

    def smp_rows_spec(self, width):
        return pl.BlockSpec((self.tm, width), lambda i: (jnp.maximum(i - self.ncb, 0), 0))

    def halo_specs(self, width, rows, offset=0):
        per = self.tm // rows
        last = N_ROWS // rows - 1
        prev = pl.BlockSpec((rows, width), lambda i: (jnp.maximum((i + offset) * per - 1, 0), 0))
        nxt = pl.BlockSpec((rows, width), lambda i: (jnp.minimum((i + offset + 1) * per, last), 0))
        return prev, nxt


ROWS_WIDE = _Rows(1024)
ROWS_NARROW = _Rows(512)


def _gelu(x):
    return 0.5 * x * (1.0 + jnp.tanh(0.7978845608028654 * (x + 0.044715 * (x * x * x))))


def _norm_mod(x, g, sc, sh):
    ms = jnp.mean(x * x, axis=-1, keepdims=True)
    return x * lax.rsqrt(ms + EPS) * g * (1.0 + sc) + sh


def _mod_row(mod_ref, cond, k):
    return mod_ref[pl.ds(cond, 1), k * D_MODEL:(k + 1) * D_MODEL]


def _dot(a, b):
    return jnp.dot(a, b, preferred_element_type=F32)


def _dot_nt(a, b):
    return lax.dot_general(a, b, (((1,), (1,)), ((), ())), preferred_element_type=F32)


def _params():
    return pltpu.CompilerParams(dimension_semantics=("arbitrary",), vmem_limit_bytes=VMEM_LIMIT)


def _resident(shape, layer=None):
    nd = len(shape)
    if layer is None:
        return pl.BlockSpec(shape, lambda *_: (0,) * nd, pipeline_mode=pl.Buffered(1))
    return pl.BlockSpec((None,) + tuple(shape), lambda *_: (layer,) + (0,) * nd, pipeline_mode=pl.Buffered(1))


def _mod_spec(layer):
    return _resident((SUBLANES, N_MOD * D_MODEL), layer)


class _Side(NamedTuple):
    src: jax.Array
    layer: int
    axis: int
    n: int


def _side_plumbing(sides):
    in_specs, out_specs, out_shape = [], [], []
    for s in sides:
        _, r, c = s.src.shape
        blk = (r // s.n, c) if s.axis == 0 else (r, c // s.n)

        def idx(i, s=s):
            k = jnp.minimum(i, s.n - 1)
            return (k, 0) if s.axis == 0 else (0, k)

        in_specs.append(pl.BlockSpec((None,) + blk, lambda i, s=s, idx=idx: (s.layer,) + idx(i)))
        out_specs.append(pl.BlockSpec(blk, idx))
        out_shape.append(jax.ShapeDtypeStruct((r, c), BF16))
    return in_specs, out_specs, out_shape


def _with_sides(body, n_in, n_out, n_side):
    def kernel(*refs):
        ins, rest = refs[:n_in], refs[n_in:]
        side_in, rest = rest[:n_side], rest[n_side:]
        outs, rest = rest[:n_out], rest[n_out:]
        side_out, scratch = rest[:n_side], rest[n_side:]
        for src, dst in zip(side_in, side_out):
            dst[...] = src[...].astype(BF16)
        body(*ins, *outs, *scratch)
    return kernel


def _hosted_call(body, geo_steps, in_specs, out_specs, out_shape, scratch_shapes, sides, name, aliases=None):
    s_in, s_out, s_shape = _side_plumbing(sides)
    assert all(s.n <= geo_steps for s in sides)
    return pl.pallas_call(
        _with_sides(body, len(in_specs), len(out_specs), len(sides)),
        grid=(geo_steps,),
        in_specs=list(in_specs) + s_in,
        out_specs=list(out_specs) + s_out,
        out_shape=list(out_shape) + s_shape,
        input_output_aliases=aliases or {},
        scratch_shapes=scratch_shapes,
        compiler_params=_params(),
        name=name,
    )


ADA_TN = 1536


def _ada_kernel(c_ref, w_ref, b_ref, o_ref):
    c = c_ref[...]
    s = (c * (1.0 / (1.0 + jnp.exp(-c)))).astype(BF16)
    o_ref[...] = _dot(s, w_ref[...].astype(BF16)) + b_ref[...]


def _ada_call(cond8, w_ada, b_ada):
    nt = N_MOD * D_MODEL // ADA_TN
    return pl.pallas_call(
        _ada_kernel,
        grid=(DEPTH, nt),
        in_specs=[
            pl.BlockSpec((SUBLANES, D_MODEL), lambda l, j: (0, 0)),
            pl.BlockSpec((None, D_MODEL, ADA_TN), lambda l, j: (l, 0, j)),
            pl.BlockSpec((None, 1, ADA_TN), lambda l, j: (l, 0, j)),
        ],
        out_specs=pl.BlockSpec((None, SUBLANES, ADA_TN), lambda l, j: (l, 0, j)),
        out_shape=jax.ShapeDtypeStruct((DEPTH, SUBLANES, N_MOD * D_MODEL), F32),
        compiler_params=pltpu.CompilerParams(dimension_semantics=("arbitrary", "arbitrary"),
                                             vmem_limit_bytes=VMEM_LIMIT),
        name="ada",
    )(cond8, w_ada, b_ada.reshape(DEPTH, 1, N_MOD * D_MODEL))


def _gmlp_kernel(geo, split_in, *refs):
    i = pl.program_id(0)
    tm = geo.tm
    if split_in:
        xc_ref, xs_ref, *refs = refs
        x = jnp.where(i < geo.ncb, xc_ref[...], xs_ref[...])
    else:
        x_ref, *refs = refs
        x = x_ref[...]
    g_ref, mod_ref, win_ref, gv_ref, ws_ref, bs_ref, wout_ref, o_ref, h_ref, u_ref, v_ref, p_ref = refs
    cond = geo.cond_idx(i)
    h_ref[...] = _norm_mod(x, g_ref[...], _mod_row(mod_ref, cond, 1), _mod_row(mod_ref, cond, 0)).astype(BF16)
    ssq = jnp.zeros((tm, 1), F32)
    for j in range(D_GMLP // TN):
        sl = slice(j * TN, (j + 1) * TN)
        u_ref[:, sl] = _gelu(_dot(h_ref[...], win_ref[:, sl]))
        v = _gelu(_dot(h_ref[...], win_ref[:, D_GMLP + j * TN:D_GMLP + (j + 1) * TN]))
        v_ref[:, sl] = v
        ssq = ssq + jnp.sum(v * v, axis=-1, keepdims=True)
    r = lax.rsqrt(ssq * (1.0 / D_GMLP) + EPS)
    for g in range(N_GROUPS_A):
        sl = slice(g * D_GROUP_A, (g + 1) * D_GROUP_A)
        vn = (v_ref[:, sl] * r * gv_ref[:, sl]).astype(BF16)
        for c in range(tm // CHUNK):
            rows = slice(c * CHUNK, (c + 1) * CHUNK)
            sv = _dot(ws_ref[g], vn[rows, :]) + bs_ref[g]
            p_ref[rows, sl] = (u_ref[rows, sl] * sv).astype(BF16)
    y = _dot(p_ref[...], wout_ref[...])
    o_ref[...] = x + _mod_row(mod_ref, cond, 2) * y


def _gmlp_call(x, layer, j, mods, g_mix, w_in, g_v, w_s, b_s, w_out, sides=()):
    split_in = isinstance(x, tuple)
    geo = ROWS_NARROW if split_in else ROWS_WIDE
    if split_in:
        x_specs = [geo.ctx_rows_spec(D_MODEL), geo.smp_rows_spec(D_MODEL)]
        xs = x
    else:
        x_specs = [geo.row_spec(D_MODEL)]
        xs = (x,)
    return _hosted_call(
        functools.partial(_gmlp_kernel, geo, split_in), geo.nb,
        in_specs=x_specs + [
            _resident((1, D_MODEL), layer),
            _mod_spec(layer),
            _resident((D_MODEL, 2 * D_GMLP)),
            _resident((1, D_GMLP), j),
            _resident((N_GROUPS_A, CHUNK, CHUNK), j),
            _resident((N_GROUPS_A, CHUNK, 1), j),
            _resident((D_GMLP, D_MODEL)),
        ],
        out_specs=[geo.row_spec(D_MODEL)],
        out_shape=[jax.ShapeDtypeStruct((N_ROWS, D_MODEL), F32)],
        scratch_shapes=[
            pltpu.VMEM((geo.tm, D_MODEL), BF16),
            pltpu.VMEM((geo.tm, D_GMLP), F32),
            pltpu.VMEM((geo.tm, D_GMLP), F32),
            pltpu.VMEM((geo.tm, D_GMLP), BF16),
        ],
        sides=sides, name="gmlp",
    )(*xs, g_mix, mods, w_in, g_v, w_s, b_s, w_out, *[s.src for s in sides])


def _pool_kernel(geo, x_ref, xp_ref, xn_ref, g_ref, mod_ref, pw_ref, pb_ref, ps_ref, o_ref, hext_ref):
    i = pl.program_id(0)
    tm = geo.tm
    ext = tm + 2 * HALO
    cond = geo.cond_idx(i)
    g = g_ref[...]
    sc = _mod_row(mod_ref, cond, 1)
    sh = _mod_row(mod_ref, cond, 0)
    gate = _mod_row(mod_ref, cond, 2)
    x = x_ref[...]
    hext_ref[0:tm, :] = _norm_mod(x, g, sc, sh)
    hext_ref[tm:tm + HALO, :] = _norm_mod(xn_ref[...], g, sc, sh)
    hext_ref[tm + HALO:ext, :] = _norm_mod(xp_ref[...], g, sc, sh)
    seq = geo.seq_len(i)
    e = lax.broadcasted_iota(jnp.int32, (ext, D_POOL_GROUP), 0)
    pos = (i * tm + jnp.where(e < tm + HALO, e, e - ext)) & (seq - 1)
    p = (i * tm + lax.broadcasted_iota(jnp.int32, (tm, D_POOL_GROUP), 0)) & (seq - 1)
    for grp in range(N_POOL_GROUPS):
        sl = slice(grp * D_POOL_GROUP, (grp + 1) * D_POOL_GROUP)
        half = POOL_WINDOWS[grp] // 2
        he = hext_ref[:, sl]
        fwd = he
        bwd = jnp.where(pos >= 1, pltpu.roll(he, 1, 0), 0.0)
        span = 1
        while span < half:
            fwd = fwd + jnp.where(pos < seq - span, pltpu.roll(fwd, ext - span, 0), 0.0)
            bwd = bwd + jnp.where(pos >= span, pltpu.roll(bwd, span, 0), 0.0)
            span *= 2
        cnt = (jnp.minimum(p + half - 1, seq - 1) - jnp.maximum(p - half, 0) + 1).astype(F32)
        d = (fwd[:tm, :] + bwd[:tm, :]) / cnt - he[:tm, :]
        y = (_dot(d.astype(BF16), pw_ref[grp]) + pb_ref[:, sl]) * ps_ref[:, sl]
        o_ref[:, sl] = x[:, sl] + gate[:, sl] * y


def _pool_call(x, layer, j, mods, g_mix, p_w, p_b, p_scale, sides=()):
    geo = ROWS_WIDE
    prev, nxt = geo.halo_specs(D_MODEL, HALO)
    return _hosted_call(
        functools.partial(_pool_kernel, geo), geo.nb,
        in_specs=[
            geo.row_spec(D_MODEL), prev, nxt,
            _resident((1, D_MODEL), layer),
            _mod_spec(layer),
            _resident((N_POOL_GROUPS, D_POOL_GROUP, D_POOL_GROUP), j),
            _resident((1, D_MODEL), j),
            _resident((1, D_MODEL), j),
        ],
        out_specs=[geo.row_spec(D_MODEL)],
        out_shape=[jax.ShapeDtypeStruct((N_ROWS, D_MODEL), F32)],
        scratch_shapes=[pltpu.VMEM((geo.tm + 2 * HALO, D_MODEL), F32)],
        sides=sides, name="pool",
    )(x, x, x, g_mix, mods, p_w, p_b, p_scale, *[s.src for s in sides])


def _rope_tables(tm):
    n_freq = HEAD_DIM // 4
    inv = ROPE_THETA ** (-np.arange(n_freq, dtype=np.float64) / n_freq)
    t = np.arange(DEC_SEQ)
    lane = np.arange(LANES) % HEAD_DIM
    quarter, f = lane // n_freq, lane % n_freq
    posn = np.where(quarter[None, :] < 2, (t // GRID_W)[:, None], (t % GRID_W)[:, None])
    ang = posn * inv[f][None, :]
    cos, sin = np.cos(ang), np.sin(ang)
    odd = (quarter % 2 == 1)[None, :]
    tab = np.concatenate([cos, np.where(odd, sin, 0.0), np.where(odd, 0.0, -sin)], axis=1)
    ident = np.concatenate([np.ones((tm, LANES)), np.zeros((tm, 2 * LANES))], axis=1)
    return np.concatenate([ident, tab], axis=0).astype(np.float32)


def _head_sumsq(t, bd_ref):
    sq = t * t
    hi = sq.astype(BF16)
    lo = (sq - hi.astype(F32)).astype(BF16)
    return _dot(hi, bd_ref[...]) + _dot(lo, bd_ref[...])


def _rope(t, tab_ref):
    q = HEAD_DIM // 4
    return (t * tab_ref[:, 0:LANES]
            + pltpu.roll(t, q, 1) * tab_ref[:, LANES:2 * LANES]
            + pltpu.roll(t, LANES - q, 1) * tab_ref[:, 2 * LANES:3 * LANES])


def _kv_ext(t, o_ref):
    lo = lax.broadcasted_iota(jnp.int32, (t.shape[0], LANES), 1) < HEAD_DIM
    for m in range(KV_DIM // LANES):
        a = t[:, m * LANES:(m + 1) * LANES]
        r = pltpu.roll(a, HEAD_DIM, 1)
        base = m * 4 * LANES
        o_ref[:, base:base + LANES] = jnp.where(lo, a, 0.0).astype(o_ref.dtype)
        o_ref[:, base + LANES:base + 2 * LANES] = jnp.where(lo, 0.0, r).astype(o_ref.dtype)
        o_ref[:, base + 2 * LANES:base + 3 * LANES] = jnp.where(lo, r, 0.0).astype(o_ref.dtype)
        o_ref[:, base + 3 * LANES:base + 4 * LANES] = jnp.where(lo, 0.0, a).astype(o_ref.dtype)


def _qkv_kernel(geo, x_ref, g_ref, mod_ref, w_ref, gq_ref, gk_ref, bd_ref, tab_ref,
                q_ref, kx_ref, vx_ref, kn_ref, v_ref, h_ref, kr_ref):
    i = pl.program_id(0)
    cond = geo.cond_idx(i)
    h_ref[...] = _norm_mod(x_ref[...], g_ref[...], _mod_row(mod_ref, cond, 1),
                           _mod_row(mod_ref, cond, 0)).astype(BF16)
    scale = HEAD_DIM ** -0.5 * LOG2E
    for c in range(Q_DIM // TN):
        sl = slice(c * TN, (c + 1) * TN)
        t = _dot(h_ref[...], w_ref[:, sl])
        t = t * lax.rsqrt(_head_sumsq(t, bd_ref) * (1.0 / HEAD_DIM) + EPS) * (gq_ref[...] * scale)
        for m in range(TN // LANES):
            q_ref[:, c * TN + m * LANES:c * TN + (m + 1) * LANES] = _rope(
                t[:, m * LANES:(m + 1) * LANES], tab_ref).astype(BF16)
    t = _dot(h_ref[...], w_ref[:, Q_DIM:Q_DIM + KV_DIM])
    t = t * lax.rsqrt(_head_sumsq(t, bd_ref) * (1.0 / HEAD_DIM) + EPS) * gk_ref[...]
    is_ctx = i < geo.ncb

    @pl.when(is_ctx)
    def _():
        kn_ref[...] = t

    for m in range(KV_DIM // LANES):
        kr_ref[:, m * LANES:(m + 1) * LANES] = _rope(t[:, m * LANES:(m + 1) * LANES], tab_ref)
    _kv_ext(kr_ref[...], kx_ref)
    v = _dot(h_ref[...], w_ref[:, Q_DIM + KV_DIM:Q_DIM + 2 * KV_DIM])

    @pl.when(is_ctx)
    def _():
        v_ref[...] = v

    _kv_ext(v, vx_ref)


def _qkv_call(x, layer, j, mods, g_mix, w_qkv, g_q, g_k, bd):
    geo = ROWS_WIDE
    tab = jnp.asarray(_rope_tables(geo.tm))
    tab_spec = pl.BlockSpec((geo.tm, 3 * LANES),
                            lambda i: (jnp.where(i < geo.ncb, 0, 1 + (i - geo.ncb) % geo.bps), 0))
    return pl.pallas_call(
        functools.partial(_qkv_kernel, geo),
        grid=(geo.nb,),
        in_specs=[
            geo.row_spec(D_MODEL),
            _resident((1, D_MODEL), layer),
            _mod_spec(layer),
            _resident((D_MODEL, Q_DIM + 2 * KV_DIM)),
            _resident((1, TN), j),
            _resident((1, TN), j),
            _resident((TN, TN)),
            tab_spec,
        ],
        out_specs=[geo.row_spec(Q_DIM), geo.row_spec(KV_EXT), geo.row_spec(KV_EXT), geo.ctx_rows_spec(KV_DIM),
                   geo.ctx_rows_spec(KV_DIM)],
        out_shape=[
            jax.ShapeDtypeStruct((N_ROWS, Q_DIM), BF16),
            jax.ShapeDtypeStruct((N_ROWS, KV_EXT), BF16),
            jax.ShapeDtypeStruct((N_ROWS, KV_EXT), BF16),
            jax.ShapeDtypeStruct((N_CTX_ROWS, KV_DIM), F32),
            jax.ShapeDtypeStruct((N_CTX_ROWS, KV_DIM), F32),
        ],
        scratch_shapes=[pltpu.VMEM((geo.tm, D_MODEL), BF16), pltpu.VMEM((geo.tm, KV_DIM), F32)],
        compiler_params=_params(),
        name="qkv",
    )(x, g_mix, mods, w_qkv, g_q, g_k, bd, tab)


def _head_lanes(g, j):
    base = (g * 2 + j) * LANES
    return slice(base, base + LANES)


def _ctx_attn_kernel(geo, j, sink_ref, x_ref, mod_ref, q_ref, kx_ref, vx_ref, wo_ref, o_ref, a_ref):
    for s in range(geo.tm // SEQ):
        rows = slice(s * SEQ, (s + 1) * SEQ)
        for g in range(N_KV_HEADS):
            for pr in range(2):
                qp = q_ref[rows, _head_lanes(g, pr)]
                acc = jnp.zeros((SEQ, LANES), F32)
                for var in range(2):
                    sink = sink_ref[j * N_HEADS + g * Q_PER_KV + 2 * pr + var] * LOG2E
                    sc = _dot_nt(qp, kx_ref[rows, _head_lanes(g, var)])
                    m = jnp.maximum(jnp.max(sc, axis=-1, keepdims=True), sink)
                    e = jnp.exp2(sc - m)
                    den = jnp.sum(e, axis=-1, keepdims=True) + jnp.exp2(sink - m)
                    acc = acc + _dot(e.astype(BF16), vx_ref[rows, _head_lanes(g, var)]) / den
                a_ref[rows, _head_lanes(g, pr)] = acc.astype(BF16)
    y = _dot(a_ref[...], wo_ref[...])
    o_ref[...] = x_ref[...] + _mod_row(mod_ref, 0, 2) * y


def _ctx_attn_call(x, layer, j, mods, sink, q, kx, vx, w_o):
    geo = ROWS_WIDE
    return pl.pallas_call(
        functools.partial(_ctx_attn_kernel, geo, j),
        grid=(geo.ncb,),
        in_specs=[
            pl.BlockSpec(memory_space=pltpu.SMEM),
            geo.row_spec(D_MODEL),
            _mod_spec(layer),
            geo.row_spec(Q_DIM), geo.row_spec(KV_EXT), geo.row_spec(KV_EXT),
            _resident((Q_DIM, D_MODEL)),
        ],
        out_specs=geo.row_spec(D_MODEL),
        out_shape=jax.ShapeDtypeStruct((N_ROWS, D_MODEL), F32),
        input_output_aliases={1: 0},
        scratch_shapes=[pltpu.VMEM((geo.tm, Q_DIM), BF16)],
        compiler_params=_params(),
        name="ctx_attn",
    )(sink, x, mods, q, kx, vx, w_o)


def _group_queries(q_ref, rows, g):
    return jnp.concatenate([q_ref[rows, _head_lanes(g, 0)], q_ref[rows, _head_lanes(g, 1)]], axis=0)


def _group_sinks(sink_ref, j, g, n):
    heads = (0, 2, 1, 3)
    return jnp.concatenate(
        [jnp.full((n, 1), sink_ref[j * N_HEADS + g * Q_PER_KV + h] * LOG2E, F32) for h in heads], axis=0)


def _sink_softmax(pieces, sink_col):
    m = sink_col
    for s in pieces:
        m = jnp.maximum(m, jnp.max(s, axis=-1, keepdims=True))
    es = [jnp.exp2(s - m) for s in pieces]
    den = jnp.exp2(sink_col - m)
    for e in es:
        den = den + jnp.sum(e, axis=-1, keepdims=True)
    return es, 1.0 / den


def _store_group(a_ref, rows, g, pv, n):
    for pr in range(2):
        sub = slice(pr * n, (pr + 1) * n)
        a_ref[rows, _head_lanes(g, pr)] = (pv[0][sub, :] + pv[1][sub, :]).astype(BF16)


def _smp_attn_kernel(geo, j, sink_ref, x_ref, mod_ref, q_ref, kc_ref, kp_ref, kn_ref, vc_ref, vp_ref, vn_ref,
                     ck_ref, cv_ref, wo_ref, o_ref, a_ref, kall_ref, vall_ref):
    i = pl.program_id(0)
    tm = geo.tm
    n_qb = tm // BLOCK
    kwin = tm + 2 * BLOCK
    kall_ref[0:BLOCK, :] = kp_ref[...]
    kall_ref[BLOCK:BLOCK + tm, :] = kc_ref[...]
    kall_ref[BLOCK + tm:kwin, :] = kn_ref[...]
    vall_ref[0:BLOCK, :] = vp_ref[...]
    vall_ref[BLOCK:BLOCK + tm, :] = vc_ref[...]
    vall_ref[BLOCK + tm:kwin, :] = vn_ref[...]
    first = (i % geo.bps) == 0
    last = (i % geo.bps) == geo.bps - 1
    qi = lax.broadcasted_iota(jnp.int32, (Q_PER_KV * BLOCK, BLOCK), 0) & (BLOCK - 1)
    kj = lax.broadcasted_iota(jnp.int32, (Q_PER_KV * BLOCK, BLOCK), 1)
    for b in range(n_qb):
        rows = slice(b * BLOCK, (b + 1) * BLOCK)
        win = slice(b * BLOCK, (b + 3) * BLOCK)
        ok_prev = kj >= (qi + jnp.where(first, BLOCK, 0) if b == 0 else qi)
        ok_next = kj <= (qi - jnp.where(last, BLOCK, 0) if b == n_qb - 1 else qi)
        for g in range(N_KV_HEADS):
            q2 = _group_queries(q_ref, rows, g)
            sw = jnp.concatenate([_dot_nt(q2, kall_ref[win, _head_lanes(g, var)]) for var in range(2)], axis=0)
            sc = jnp.concatenate([_dot_nt(q2, ck_ref[:, _head_lanes(g, var)]) for var in range(2)], axis=0)
            pieces = [jnp.where(ok_prev, sw[:, 0:BLOCK], NEG_INF), sw[:, BLOCK:2 * BLOCK],
                      jnp.where(ok_next, sw[:, 2 * BLOCK:3 * BLOCK], NEG_INF), sc]
            es, rinv = _sink_softmax(pieces, _group_sinks(sink_ref, j, g, BLOCK))
            ew = jnp.concatenate(es[:3], axis=-1).astype(BF16)
            ec = es[3].astype(BF16)
            pv = []
            for var in range(2):
                sub = slice(var * 2 * BLOCK, (var + 1) * 2 * BLOCK)
                lanes = _head_lanes(g, var)
                pv.append((_dot(ew[sub, :], vall_ref[win, lanes]) + _dot(ec[sub, :], cv_ref[:, lanes]))
                          * rinv[sub, :])
            _store_group(a_ref, rows, g, pv, BLOCK)
    y = _dot(a_ref[...], wo_ref[...])
    o_ref[...] = x_ref[...] + _mod_row(mod_ref, 1 + i // geo.bps, 2) * y


def _smp_attn_call(x, layer, j, mods, sink, q, kx, vx, ckx, cvx, w_o, sides=()):
    geo = ROWS_NARROW
    prev, nxt = geo.halo_specs(KV_EXT, BLOCK, offset=geo.ncb)
    ctx_spec = pl.BlockSpec((None, None, PAST_LEN, KV_EXT), lambda i: (i // geo.bps, j, 0, 0))
    kwin = geo.tm + 2 * BLOCK
    return _hosted_call(
        functools.partial(_smp_attn_kernel, geo, j), geo.nb - geo.ncb,
        in_specs=[
            pl.BlockSpec(memory_space=pltpu.SMEM),
            geo.row_spec(D_MODEL, geo.ncb),
            _mod_spec(layer),
            geo.row_spec(Q_DIM, geo.ncb),
            geo.row_spec(KV_EXT, geo.ncb), prev, nxt,
            geo.row_spec(KV_EXT, geo.ncb), prev, nxt,
            ctx_spec, ctx_spec,
            _resident((Q_DIM, D_MODEL)),
        ],
        out_specs=[geo.row_spec(D_MODEL, geo.ncb)],
        out_shape=[jax.ShapeDtypeStruct((N_ROWS, D_MODEL), F32)],
        scratch_shapes=[pltpu.VMEM((geo.tm, Q_DIM), BF16), pltpu.VMEM((kwin, KV_EXT), BF16),
                        pltpu.VMEM((kwin, KV_EXT), BF16)],
        sides=sides, name="smp_attn", aliases={1: 0},
    )(sink, x, mods, q, kx, kx, kx, vx, vx, vx, ckx, cvx, w_o, *[s.src for s in sides])


def _ext_heads(t):
    lead = t.shape[:-1]
    h = t.reshape(lead + (N_KV_HEADS, 1, HEAD_DIM))
    z = jnp.zeros_like(h)
    e = jnp.concatenate([jnp.concatenate([h, z], axis=-1), jnp.concatenate([z, h], axis=-1)], axis=-2)
    return e.reshape(lead + (KV_EXT,))


N_FT = D_FF // TN


def _ffn_kernel(geo, split_out, x_ref, xp_ref, xn_ref, g_ref, mod_ref, win_ref, cw_ref, cb_ref, wo_ref, *refs):
    i = pl.program_id(0)
    tm = geo.tm
    ext = tm + 2 * HALO
    hext_ref, gbuf_ref = refs[-2:]
    cond = geo.cond_idx(i)
    g = g_ref[...]
    sh = _mod_row(mod_ref, cond, 3)
    sc = _mod_row(mod_ref, cond, 4)
    x = x_ref[...]
    hext_ref[0:tm, :] = _norm_mod(x, g, sc, sh).astype(BF16)
    hext_ref[tm:tm + HALO, :] = _norm_mod(xn_ref[...], g, sc, sh).astype(BF16)
    hext_ref[tm + HALO:ext, :] = _norm_mod(xp_ref[...], g, sc, sh).astype(BF16)
    seq = geo.seq_len(i)
    pos = (i * tm + lax.broadcasted_iota(jnp.int32, (tm, TN), 0)) & (seq - 1)
    has_prev = pos != 0
    has_next = pos != seq - 1
    for j in range(N_FT):
        sl = slice(j * TN, (j + 1) * TN)
        a = _dot(hext_ref[...], win_ref[:, sl])
        u = _dot(hext_ref[0:tm, :], win_ref[:, D_FF + j * TN:D_FF + (j + 1) * TN])
        a_prev = jnp.where(has_prev, pltpu.roll(a, 1, 0)[:tm, :], 0.0)
        a_next = jnp.where(has_next, pltpu.roll(a, ext - 1, 0)[:tm, :], 0.0)
        a = a_prev * cw_ref[0:1, sl] + a[:tm, :] * cw_ref[1:2, sl] + a_next * cw_ref[2:3, sl] + cb_ref[:, sl]
        gbuf_ref[:, sl] = (_gelu(a) * u).astype(BF16)
    y = _dot(gbuf_ref[...], wo_ref[...])
    res = x + _mod_row(mod_ref, cond, 5) * y
    if split_out:
        oc_ref, os_ref = refs[:2]

        @pl.when(i < geo.ncb)
        def _():
            oc_ref[...] = res

        @pl.when(i >= geo.ncb)
        def _():
            os_ref[...] = res
    else:
        refs[0][...] = res


def _ffn_call(x, layer, mods, g_ffn, w_in, conv_w, conv_b, w_out, split_out, sides=()):
    geo = ROWS_WIDE
    prev, nxt = geo.halo_specs(D_MODEL, HALO)
    if split_out:
        out_specs = [geo.ctx_rows_spec(D_MODEL), geo.smp_rows_spec(D_MODEL)]
        out_shape = [jax.ShapeDtypeStruct((N_CTX_ROWS, D_MODEL), F32),
                     jax.ShapeDtypeStruct((N_SMP_ROWS, D_MODEL), F32)]
    else:
        out_specs = [geo.row_spec(D_MODEL)]
        out_shape = [jax.ShapeDtypeStruct((N_ROWS, D_MODEL), F32)]
    return _hosted_call(
        functools.partial(_ffn_kernel, geo, split_out), geo.nb,
        in_specs=[
            geo.row_spec(D_MODEL), prev, nxt,
            _resident((1, D_MODEL), layer),
            _mod_spec(layer),
            _resident((D_MODEL, 2 * D_FF)),
            _resident((CONV_W, D_FF), layer),
            _resident((1, D_FF), layer),
            _resident((D_FF, D_MODEL)),
        ],
        out_specs=out_specs,
        out_shape=out_shape,
        scratch_shapes=[pltpu.VMEM((geo.tm + 2 * HALO, D_MODEL), BF16), pltpu.VMEM((geo.tm, D_FF), BF16)],
        sides=sides, name="ffn",
    )(x, x, x, g_ffn, mods, w_in, conv_w, conv_b, w_out, *[s.src for s in sides])


def kernel(x_prompt, x_sample, cache_k, cache_v, c, c_ctx, w_ada, b_ada, g_mix, g_ffn, w_ffn_in, ffn_conv_w,
           ffn_conv_b, w_ffn_out, a_w_in, a_g_v, a_w_s, a_b_s, a_w_out, p_w, p_b, p_scale, c_w_qkv, c_g_q,
           c_g_k, c_sink, c_w_o):
    x = (x_prompt.reshape(N_CTX_ROWS, D_MODEL), x_sample.reshape(N_SMP_ROWS, D_MODEL))

    cond8 = jnp.concatenate([c_ctx[None, :], c, jnp.zeros((SUBLANES - N_COND, D_MODEL), F32)], axis=0)
    mods = _ada_call(cond8, w_ada, b_ada)

    g_mix, g_ffn = g_mix[:, None, :], g_ffn[:, None, :]
    ffn_conv_b = ffn_conv_b[:, None, :]
    a_w_s, a_g_v, a_b_s = a_w_s.astype(BF16), a_g_v[:, None, :], a_b_s[:, :, :, None]
    p_w, p_b, p_scale = p_w.astype(BF16), p_b[:, None, :], p_scale[:, None, :]
    gq = jnp.tile(c_g_q, (1, TN // HEAD_DIM))[:, None, :]
    gk = jnp.tile(c_g_k, (1, TN // HEAD_DIM))[:, None, :]
    bd = jnp.asarray(np.kron(np.eye(TN // HEAD_DIM), np.ones((HEAD_DIM, HEAD_DIM))), dtype=BF16)
    sink = c_sink.reshape(N_C_LAYERS * N_HEADS)
    ckx = _ext_heads(cache_k.reshape(DEC_BATCH, N_C_LAYERS, PAST_LEN, KV_DIM)).astype(BF16)
    cvx = _ext_heads(cache_v.reshape(DEC_BATCH, N_C_LAYERS, PAST_LEN, KV_DIM)).astype(BF16)

    def ffn_sides(layer, n_in, axis_in, n_out):
        return (_Side(w_ffn_in, layer, axis_in, n_in), _Side(w_ffn_out, layer, 0, n_out))

    assert DEPTH == 4 and N_MIXERS == 3, "the hosting plan below is written for gMLP, pool, attention, gMLP"
    gw = (a_w_in[0].astype(BF16), a_w_out[0].astype(BF16))
    x, *fw = _gmlp_call(x, 0, 0, mods, g_mix, gw[0], a_g_v, a_w_s, a_b_s, gw[1],
                        sides=ffn_sides(0, 2 * N_FT, 1, 2 * N_FT))
    x, *fw = _ffn_call(x, 0, mods, g_ffn, fw[0], ffn_conv_w, ffn_conv_b, fw[1], False,
                       sides=ffn_sides(1, N_FT, 1, N_FT))
    x, w_qkv, w_o = _pool_call(x, 1, 0, mods, g_mix, p_w, p_b, p_scale,
                               sides=(_Side(c_w_qkv, 0, 1, 12), _Side(c_w_o, 0, 0, 8)))
    x, *fw = _ffn_call(x, 1, mods, g_ffn, fw[0], ffn_conv_w, ffn_conv_b, fw[1], False,
                       sides=ffn_sides(2, N_FT, 1, N_FT))
    q, kx, vx, kn, v = _qkv_call(x, 2, 0, mods, g_mix, w_qkv, gq, gk, bd)
    new_k = kn.reshape(BATCH, 1, SEQ, N_KV_HEADS, HEAD_DIM)
    new_v = v.reshape(BATCH, 1, SEQ, N_KV_HEADS, HEAD_DIM)
    x = _ctx_attn_call(x, 2, 0, mods, sink, q, kx, vx, w_o)
    x, *fw_last = _smp_attn_call(x, 2, 0, mods, sink, q, kx, vx, ckx, cvx, w_o, sides=ffn_sides(3, 8, 0, 8))
    x, *gw = _ffn_call(x, 2, mods, g_ffn, fw[0], ffn_conv_w, ffn_conv_b, fw[1], False,
                       sides=(_Side(a_w_in, 1, 1, 8), _Side(a_w_out, 1, 0, 8)))
    (x,) = _gmlp_call(x, 3, 1, mods, g_mix, gw[0], a_g_v, a_w_s, a_b_s, gw[1])
    x = _ffn_call(x, 3, mods, g_ffn, fw_last[0], ffn_conv_w, ffn_conv_b, fw_last[1], True)

    y_prompt = x[0].reshape(BATCH, SEQ, D_MODEL)
    y_sample = x[1].reshape(DEC_BATCH, DEC_SEQ, D_MODEL)
    return (y_prompt, y_sample, new_k, new_v)
```

```python
import functools
from typing import NamedTuple

import numpy as np
import jax
import jax.numpy as jnp
from jax import lax
from jax.experimental import pallas as pl
from jax.experimental.pallas import tpu as pltpu

D_MODEL = 1024
BATCH = 32
SEQ = 256
DEPTH = 4
DEC_BATCH = 2
DEC_SEQ = 2048
PAST_LEN = 256
GRID_W = 64
N_MIXERS = 3
CHUNK = 128
D_GMLP = 2 * D_MODEL
N_GROUPS_A = 8
D_GROUP_A = D_GMLP // N_GROUPS_A
POOL_WINDOWS = (2, 4, 8, 16)
N_POOL_GROUPS = 4
D_POOL_GROUP = D_MODEL // N_POOL_GROUPS
N_HEADS = 16
N_C_LAYERS = DEPTH // N_MIXERS
N_KV_HEADS = 4
HEAD_DIM = 64
Q_PER_KV = N_HEADS // N_KV_HEADS
WINDOW = 128
BLOCK = 128
ROPE_THETA = 10000.0
D_FF = 2816
CONV_W = 3
EPS = 1e-6
NEG_INF = -1e30
LOG2E = 1.4426950408889634

F32 = jnp.float32
BF16 = jnp.bfloat16

N_CTX_ROWS = BATCH * SEQ
N_SMP_ROWS = DEC_BATCH * DEC_SEQ
N_ROWS = N_CTX_ROWS + N_SMP_ROWS
N_COND = 1 + DEC_BATCH
N_MOD = 6
HALO = 16
SUBLANES = 8
LANES = 128
TN = 256
Q_DIM = N_HEADS * HEAD_DIM
KV_DIM = N_KV_HEADS * HEAD_DIM
KV_EXT = N_KV_HEADS * 2 * LANES
VMEM_CAP = 56 * 1024 * 1024
VMEM_TEMPS = 2 * 1024 * 1024


class _Rows:
    def __init__(self, tm):
        self.tm = tm
        self.nb = N_ROWS // tm
        self.ncb = N_CTX_ROWS // tm
        self.bps = DEC_SEQ // tm

    def cond_idx(self, i):
        return jnp.where(i < self.ncb, 0, 1 + (i - self.ncb) // self.bps)

    def seq_len(self, i):
        return jnp.where(i < self.ncb, SEQ, DEC_SEQ)

    def row_spec(self, width, offset=0):
        return pl.BlockSpec((self.tm, width), lambda i: (i + offset, 0))

    def ctx_rows_spec(self, width):
        return pl.BlockSpec((self.tm, width), lambda i: (jnp.minimum(i, self.ncb - 1), 0))

    def smp_rows_spec(self, width):
        return pl.BlockSpec((self.tm, width), lambda i: (jnp.maximum(i - self.ncb, 0), 0))

    def halo_specs(self, width, rows, offset=0):
        per = self.tm // rows
        last = N_ROWS // rows - 1
        prev = pl.BlockSpec((rows, width), lambda i: (jnp.maximum((i + offset) * per - 1, 0), 0))
        nxt = pl.BlockSpec((rows, width), lambda i: (jnp.minimum((i + offset + 1) * per, last), 0))
        return prev, nxt


ROWS_WIDE = _Rows(1024)
ROWS_NARROW = _Rows(512)


def _gelu(x):
    return 0.5 * x * (1.0 + jnp.tanh(0.7978845608028654 * (x + 0.044715 * (x * x * x))))


def _norm_mod(x, g, sc, sh):
    ms = jnp.mean(x * x, axis=-1, keepdims=True)
    return x * lax.rsqrt(ms + EPS) * g * (1.0 + sc) + sh


def _mod_row(mod_ref, cond, k):
    return mod_ref[pl.ds(cond, 1), k * D_MODEL:(k + 1) * D_MODEL]


def _dot(a, b):
    return jnp.dot(a, b, preferred_element_type=F32)


def _dot_nt(a, b):
    return lax.dot_general(a, b, (((1,), (1,)), ((), ())), preferred_element_type=F32)


def _padded_bytes(shape, dtype):
    item = jnp.dtype(dtype).itemsize
    dims = [1 if d is None else d for d in shape]
    dims[-1] = -(-dims[-1] // LANES) * LANES
    if len(dims) > 1:
        rows = SUBLANES * 4 // item
        dims[-2] = -(-dims[-2] // rows) * rows
    return item * int(np.prod(dims))


def _vmem_limit(specs, dtypes, scratch_shapes):
    total = 0
    for spec, dtype in zip(specs, dtypes):
        if spec.block_shape is not None:
            buffers = 2 if spec.pipeline_mode is None else spec.pipeline_mode.buffer_count
            total += buffers * _padded_bytes(spec.block_shape, dtype)
    total += sum(_padded_bytes(s.shape, s.dtype) for s in scratch_shapes)
    assert total + VMEM_TEMPS <= VMEM_CAP, f"declared VMEM buffers too large: {total} bytes"
    return VMEM_CAP


def _resident(shape, layer=None):
    nd = len(shape)
    if layer is None:
        return pl.BlockSpec(shape, lambda *_: (0,) * nd, pipeline_mode=pl.Buffered(1))
    return pl.BlockSpec((None,) + tuple(shape), lambda *_: (layer,) + (0,) * nd, pipeline_mode=pl.Buffered(1))


def _mod_spec(layer):
    return _resident((SUBLANES, N_MOD * D_MODEL), layer)


class _Side(NamedTuple):
    src: jax.Array
    layer: int
    axis: int
    n: int


def _side_plumbing(sides):
    in_specs, out_specs, out_shape = [], [], []
    for s in sides:
        _, r, c = s.src.shape
        blk = (r // s.n, c) if s.axis == 0 else (r, c // s.n)

        def idx(i, s=s):
            k = jnp.minimum(i, s.n - 1)
            return (k, 0) if s.axis == 0 else (0, k)

        in_specs.append(pl.BlockSpec((None,) + blk, lambda i, s=s, idx=idx: (s.layer,) + idx(i)))
        out_specs.append(pl.BlockSpec(blk, idx))
        out_shape.append(jax.ShapeDtypeStruct((r, c), BF16))
    return in_specs, out_specs, out_shape


def _with_sides(body, n_in, n_out, n_side):
    def kernel(*refs):
        ins, rest = refs[:n_in], refs[n_in:]
        side_in, rest = rest[:n_side], rest[n_side:]
        outs, rest = rest[:n_out], rest[n_out:]
        side_out, scratch = rest[:n_side], rest[n_side:]
        for src, dst in zip(side_in, side_out):
            dst[...] = src[...].astype(BF16)
        body(*ins, *outs, *scratch)
    return kernel


def _hosted_call(body, geo_steps, in_specs, out_specs, out_shape, scratch_shapes, sides, name, aliases=None):
    s_in, s_out, s_shape = _side_plumbing(sides)
    assert all(s.n <= geo_steps for s in sides)
    in_specs, out_specs, out_shape = list(in_specs) + s_in, list(out_specs) + s_out, list(out_shape) + s_shape

    def run(*operands):
        dtypes = [o.dtype for o in operands] + [o.dtype for o in out_shape]
        return pl.pallas_call(
            _with_sides(body, len(in_specs) - len(sides), len(out_specs) - len(sides), len(sides)),
            grid=(geo_steps,),
            in_specs=in_specs,
            out_specs=out_specs,
            out_shape=out_shape,
            input_output_aliases=aliases or {},
            scratch_shapes=scratch_shapes,
            compiler_params=pltpu.CompilerParams(
                dimension_semantics=("arbitrary",),
                vmem_limit_bytes=_vmem_limit(in_specs + out_specs, dtypes, scratch_shapes)),
            name=name,
        )(*operands)

    return run


ADA_TN = 1536


def _ada_kernel(c_ref, w_ref, b_ref, o_ref):
    c = c_ref[...]
    s = (c * (1.0 / (1.0 + jnp.exp(-c)))).astype(BF16)
    o_ref[...] = _dot(s, w_ref[...].astype(BF16)) + b_ref[...]


def _ada_call(cond8, w_ada, b_ada):
    nt = N_MOD * D_MODEL // ADA_TN
    return pl.pallas_call(
        _ada_kernel,
        grid=(DEPTH, nt),
        in_specs=[
            pl.BlockSpec((SUBLANES, D_MODEL), lambda l, j: (0, 0)),
            pl.BlockSpec((None, D_MODEL, ADA_TN), lambda l, j: (l, 0, j)),
            pl.BlockSpec((None, 1, ADA_TN), lambda l, j: (l, 0, j)),
        ],
        out_specs=pl.BlockSpec((None, SUBLANES, ADA_TN), lambda l, j: (l, 0, j)),
        out_shape=jax.ShapeDtypeStruct((DEPTH, SUBLANES, N_MOD * D_MODEL), F32),
        compiler_params=pltpu.CompilerParams(dimension_semantics=("arbitrary", "arbitrary"),
                                             vmem_limit_bytes=2 * _padded_bytes((D_MODEL, ADA_TN), F32) + VMEM_TEMPS),
        name="ada",
    )(cond8, w_ada, b_ada.reshape(DEPTH, 1, N_MOD * D_MODEL))


def _gmlp_kernel(geo, split_in, *refs):
    i = pl.program_id(0)
    tm = geo.tm
    if split_in:
        xc_ref, xs_ref, *refs = refs
        x = jnp.where(i < geo.ncb, xc_ref[...], xs_ref[...])
    else:
        x_ref, *refs = refs
        x = x_ref[...]
    g_ref, mod_ref, win_ref, gv_ref, ws_ref, bs_ref, wout_ref, o_ref, h_ref, u_ref, v_ref, p_ref = refs
    cond = geo.cond_idx(i)
    h_ref[...] = _norm_mod(x, g_ref[...], _mod_row(mod_ref, cond, 1), _mod_row(mod_ref, cond, 0)).astype(BF16)
    ssq = jnp.zeros((tm, 1), F32)
    for j in range(D_GMLP // TN):
        sl = slice(j * TN, (j + 1) * TN)
        u_ref[:, sl] = _gelu(_dot(h_ref[...], win_ref[:, sl]))
        v = _gelu(_dot(h_ref[...], win_ref[:, D_GMLP + j * TN:D_GMLP + (j + 1) * TN]))
        v_ref[:, sl] = v
        ssq = ssq + jnp.sum(v * v, axis=-1, keepdims=True)
    r = lax.rsqrt(ssq * (1.0 / D_GMLP) + EPS)
    for g in range(N_GROUPS_A):
        sl = slice(g * D_GROUP_A, (g + 1) * D_GROUP_A)
        vn = (v_ref[:, sl] * r * gv_ref[:, sl]).astype(BF16)
        for c in range(tm // CHUNK):
            rows = slice(c * CHUNK, (c + 1) * CHUNK)
            sv = _dot(ws_ref[g], vn[rows, :]) + bs_ref[g]
            p_ref[rows, sl] = (u_ref[rows, sl] * sv).astype(BF16)
    y = _dot(p_ref[...], wout_ref[...])
    o_ref[...] = x + _mod_row(mod_ref, cond, 2) * y


def _gmlp_call(x, layer, j, mods, g_mix, w_in, g_v, w_s, b_s, w_out, sides=()):
    split_in = isinstance(x, tuple)
    geo = ROWS_NARROW if split_in else ROWS_WIDE
    if split_in:
        x_specs = [geo.ctx_rows_spec(D_MODEL), geo.smp_rows_spec(D_MODEL)]
        xs = x
    else:
        x_specs = [geo.row_spec(D_MODEL)]
        xs = (x,)
    return _hosted_call(
        functools.partial(_gmlp_kernel, geo, split_in), geo.nb,
        in_specs=x_specs + [
            _resident((1, D_MODEL), layer),
            _mod_spec(layer),
            _resident((D_MODEL, 2 * D_GMLP)),
            _resident((1, D_GMLP), j),
            _resident((N_GROUPS_A, CHUNK, CHUNK), j),
            _resident((N_GROUPS_A, CHUNK, 1), j),
            _resident((D_GMLP, D_MODEL)),
        ],
        out_specs=[geo.row_spec(D_MODEL)],
        out_shape=[jax.ShapeDtypeStruct((N_ROWS, D_MODEL), F32)],
        scratch_shapes=[
            pltpu.VMEM((geo.tm, D_MODEL), BF16),
            pltpu.VMEM((geo.tm, D_GMLP), F32),
            pltpu.VMEM((geo.tm, D_GMLP), F32),
            pltpu.VMEM((geo.tm, D_GMLP), BF16),
        ],
        sides=sides, name="gmlp",
    )(*xs, g_mix, mods, w_in, g_v, w_s, b_s, w_out, *[s.src for s in sides])


def _pool_kernel(geo, x_ref, xp_ref, xn_ref, g_ref, mod_ref, pw_ref, pb_ref, ps_ref, o_ref, hext_ref):
    i = pl.program_id(0)
    tm = geo.tm
    ext = tm + 2 * HALO
    cond = geo.cond_idx(i)
    g = g_ref[...]
    sc = _mod_row(mod_ref, cond, 1)
    sh = _mod_row(mod_ref, cond, 0)
    gate = _mod_row(mod_ref, cond, 2)
    x = x_ref[...]
    hext_ref[0:tm, :] = _norm_mod(x, g, sc, sh)
    hext_ref[tm:tm + HALO, :] = _norm_mod(xn_ref[...], g, sc, sh)
    hext_ref[tm + HALO:ext, :] = _norm_mod(xp_ref[...], g, sc, sh)
    seq = geo.seq_len(i)
    e = lax.broadcasted_iota(jnp.int32, (ext, D_POOL_GROUP), 0)
    pos = (i * tm + jnp.where(e < tm + HALO, e, e - ext)) & (seq - 1)
    p = (i * tm + lax.broadcasted_iota(jnp.int32, (tm, D_POOL_GROUP), 0)) & (seq - 1)
    for grp in range(N_POOL_GROUPS):
        sl = slice(grp * D_POOL_GROUP, (grp + 1) * D_POOL_GROUP)
        half = POOL_WINDOWS[grp] // 2
        he = hext_ref[:, sl]
        fwd = he
        bwd = jnp.where(pos >= 1, pltpu.roll(he, 1, 0), 0.0)
        span = 1
        while span < half:
            fwd = fwd + jnp.where(pos < seq - span, pltpu.roll(fwd, ext - span, 0), 0.0)
            bwd = bwd + jnp.where(pos >= span, pltpu.roll(bwd, span, 0), 0.0)
            span *= 2
        cnt = (jnp.minimum(p + half - 1, seq - 1) - jnp.maximum(p - half, 0) + 1).astype(F32)
        d = (fwd[:tm, :] + bwd[:tm, :]) / cnt - he[:tm, :]
        y = (_dot(d.astype(BF16), pw_ref[grp]) + pb_ref[:, sl]) * ps_ref[:, sl]
        o_ref[:, sl] = x[:, sl] + gate[:, sl] * y


def _pool_call(x, layer, j, mods, g_mix, p_w, p_b, p_scale, sides=()):
    geo = ROWS_WIDE
    prev, nxt = geo.halo_specs(D_MODEL, HALO)
    return _hosted_call(
        functools.partial(_pool_kernel, geo), geo.nb,
        in_specs=[
            geo.row_spec(D_MODEL), prev, nxt,
            _resident((1, D_MODEL), layer),
            _mod_spec(layer),
            _resident((N_POOL_GROUPS, D_POOL_GROUP, D_POOL_GROUP), j),
            _resident((1, D_MODEL), j),
            _resident((1, D_MODEL), j),
        ],
        out_specs=[geo.row_spec(D_MODEL)],
        out_shape=[jax.ShapeDtypeStruct((N_ROWS, D_MODEL), F32)],
        scratch_shapes=[pltpu.VMEM((geo.tm + 2 * HALO, D_MODEL), F32)],
        sides=sides, name="pool",
    )(x, x, x, g_mix, mods, p_w, p_b, p_scale, *[s.src for s in sides])


def _rope_tables(tm):
    n_freq = HEAD_DIM // 4
    inv = ROPE_THETA ** (-np.arange(n_freq, dtype=np.float64) / n_freq)
    t = np.arange(DEC_SEQ)
    lane = np.arange(LANES) % HEAD_DIM
    quarter, f = lane // n_freq, lane % n_freq
    posn = np.where(quarter[None, :] < 2, (t // GRID_W)[:, None], (t % GRID_W)[:, None])
    ang = posn * inv[f][None, :]
    cos, sin = np.cos(ang), np.sin(ang)
    odd = (quarter % 2 == 1)[None, :]
    tab = np.concatenate([cos, np.where(odd, sin, 0.0), np.where(odd, 0.0, -sin)], axis=1)
    ident = np.concatenate([np.ones((tm, LANES)), np.zeros((tm, 2 * LANES))], axis=1)
    return np.concatenate([ident, tab], axis=0).astype(np.float32)


def _head_sumsq(t, bd_ref):
    sq = t * t
    hi = sq.astype(BF16)
    lo = (sq - hi.astype(F32)).astype(BF16)
    return _dot(hi, bd_ref[...]) + _dot(lo, bd_ref[...])


def _rope(t, tab_ref):
    q = HEAD_DIM // 4
    return (t * tab_ref[:, 0:LANES]
            + pltpu.roll(t, q, 1) * tab_ref[:, LANES:2 * LANES]
            + pltpu.roll(t, LANES - q, 1) * tab_ref[:, 2 * LANES:3 * LANES])


def _kv_ext(t, o_ref):
    lo = lax.broadcasted_iota(jnp.int32, (t.shape[0], LANES), 1) < HEAD_DIM
    for m in range(KV_DIM // LANES):
        a = t[:, m * LANES:(m + 1) * LANES]
        r = pltpu.roll(a, HEAD_DIM, 1)
        base = m * 4 * LANES
        o_ref[:, base:base + LANES] = jnp.where(lo, a, 0.0).astype(o_ref.dtype)
        o_ref[:, base + LANES:base + 2 * LANES] = jnp.where(lo, 0.0, r).astype(o_ref.dtype)
        o_ref[:, base + 2 * LANES:base + 3 * LANES] = jnp.where(lo, r, 0.0).astype(o_ref.dtype)
        o_ref[:, base + 3 * LANES:base + 4 * LANES] = jnp.where(lo, 0.0, a).astype(o_ref.dtype)


def _store_head_rows(o_ref, t):
    for h in range(N_KV_HEADS):
        o_ref[pl.ds(h, t.shape[0], stride=N_KV_HEADS), :] = t[:, h * HEAD_DIM:(h + 1) * HEAD_DIM]


def _qkv_kernel(geo, x_ref, g_ref, mod_ref, w_ref, gq_ref, gk_ref, bd_ref, tab_ref,
                q_ref, kx_ref, vx_ref, kn_ref, v_ref, h_ref, kr_ref):
    i = pl.program_id(0)
    cond = geo.cond_idx(i)
    h_ref[...] = _norm_mod(x_ref[...], g_ref[...], _mod_row(mod_ref, cond, 1),
                           _mod_row(mod_ref, cond, 0)).astype(BF16)
    scale = HEAD_DIM ** -0.5 * LOG2E
    for c in range(Q_DIM // TN):
        sl = slice(c * TN, (c + 1) * TN)
        t = _dot(h_ref[...], w_ref[:, sl])
        t = t * lax.rsqrt(_head_sumsq(t, bd_ref) * (1.0 / HEAD_DIM) + EPS) * (gq_ref[...] * scale)
        for m in range(TN // LANES):
            q_ref[:, c * TN + m * LANES:c * TN + (m + 1) * LANES] = _rope(
                t[:, m * LANES:(m + 1) * LANES], tab_ref).astype(BF16)
    t = _dot(h_ref[...], w_ref[:, Q_DIM:Q_DIM + KV_DIM])
    t = t * lax.rsqrt(_head_sumsq(t, bd_ref) * (1.0 / HEAD_DIM) + EPS) * gk_ref[...]
    is_ctx = i < geo.ncb

    @pl.when(is_ctx)
    def _():
        _store_head_rows(kn_ref, t)

    for m in range(KV_DIM // LANES):
        kr_ref[:, m * LANES:(m + 1) * LANES] = _rope(t[:, m * LANES:(m + 1) * LANES], tab_ref)
    _kv_ext(kr_ref[...], kx_ref)
    v = _dot(h_ref[...], w_ref[:, Q_DIM + KV_DIM:Q_DIM + 2 * KV_DIM])

    @pl.when(is_ctx)
    def _():
        _store_head_rows(v_ref, v)

    _kv_ext(v, vx_ref)


def _qkv_call(x, layer, j, mods, g_mix, w_qkv, g_q, g_k, bd):
    geo = ROWS_WIDE
    tab = jnp.asarray(_rope_tables(geo.tm))
    tab_spec = pl.BlockSpec((geo.tm, 3 * LANES),
                            lambda i: (jnp.where(i < geo.ncb, 0, 1 + (i - geo.ncb) % geo.bps), 0))
    cache_spec = pl.BlockSpec((geo.tm * N_KV_HEADS, HEAD_DIM), lambda i: (jnp.minimum(i, geo.ncb - 1), 0))
    return _hosted_call(
        functools.partial(_qkv_kernel, geo), geo.nb,
        in_specs=[
            geo.row_spec(D_MODEL),
            _resident((1, D_MODEL), layer),
            _mod_spec(layer),
            _resident((D_MODEL, Q_DIM + 2 * KV_DIM)),
            _resident((1, TN), j),
            _resident((1, TN), j),
            _resident((TN, TN)),
            tab_spec,
        ],
        out_specs=[geo.row_spec(Q_DIM), geo.row_spec(KV_EXT), geo.row_spec(KV_EXT), cache_spec, cache_spec],
        out_shape=[
            jax.ShapeDtypeStruct((N_ROWS, Q_DIM), BF16),
            jax.ShapeDtypeStruct((N_ROWS, KV_EXT), BF16),
            jax.ShapeDtypeStruct((N_ROWS, KV_EXT), BF16),
            jax.ShapeDtypeStruct((N_CTX_ROWS * N_KV_HEADS, HEAD_DIM), F32),
            jax.ShapeDtypeStruct((N_CTX_ROWS * N_KV_HEADS, HEAD_DIM), F32),
        ],
        scratch_shapes=[pltpu.VMEM((geo.tm, D_MODEL), BF16), pltpu.VMEM((geo.tm, KV_DIM), F32)],
        sides=(), name="qkv",
    )(x, g_mix, mods, w_qkv, g_q, g_k, bd, tab)


def _head_lanes(g, j):
    base = (g * 2 + j) * LANES
    return slice(base, base + LANES)


def _ctx_attn_kernel(geo, j, sink_ref, x_ref, mod_ref, q_ref, kx_ref, vx_ref, wo_ref, o_ref, a_ref):
    for s in range(geo.tm // SEQ):
        rows = slice(s * SEQ, (s + 1) * SEQ)
        for g in range(N_KV_HEADS):
            for pr in range(2):
                qp = q_ref[rows, _head_lanes(g, pr)]
                acc = jnp.zeros((SEQ, LANES), F32)
                for var in range(2):
                    sink = sink_ref[j * N_HEADS + g * Q_PER_KV + 2 * pr + var] * LOG2E
                    sc = _dot_nt(qp, kx_ref[rows, _head_lanes(g, var)])
                    m = jnp.maximum(jnp.max(sc, axis=-1, keepdims=True), sink)
                    e = jnp.exp2(sc - m)
                    den = jnp.sum(e, axis=-1, keepdims=True) + jnp.exp2(sink - m)
                    acc = acc + _dot(e.astype(BF16), vx_ref[rows, _head_lanes(g, var)]) / den
                a_ref[rows, _head_lanes(g, pr)] = acc.astype(BF16)
    y = _dot(a_ref[...], wo_ref[...])
    o_ref[...] = x_ref[...] + _mod_row(mod_ref, 0, 2) * y


def _ctx_attn_call(x, layer, j, mods, sink, q, kx, vx, w_o, sides=()):
    geo = ROWS_WIDE
    return _hosted_call(
        functools.partial(_ctx_attn_kernel, geo, j), geo.ncb,
        in_specs=[
            pl.BlockSpec(memory_space=pltpu.SMEM),
            geo.row_spec(D_MODEL),
            _mod_spec(layer),
            geo.row_spec(Q_DIM), geo.row_spec(KV_EXT), geo.row_spec(KV_EXT),
            _resident((Q_DIM, D_MODEL)),
        ],
        out_specs=[geo.row_spec(D_MODEL)],
        out_shape=[jax.ShapeDtypeStruct((N_ROWS, D_MODEL), F32)],
        scratch_shapes=[pltpu.VMEM((geo.tm, Q_DIM), BF16)],
        sides=sides, name="ctx_attn", aliases={1: 0},
    )(sink, x, mods, q, kx, vx, w_o, *[s.src for s in sides])


def _group_queries(q_ref, rows, g):
    return jnp.concatenate([q_ref[rows, _head_lanes(g, 0)], q_ref[rows, _head_lanes(g, 1)]], axis=0)


def _group_sinks(sink_ref, j, g, n):
    heads = (0, 2, 1, 3)
    return jnp.concatenate(
        [jnp.full((n, 1), sink_ref[j * N_HEADS + g * Q_PER_KV + h] * LOG2E, F32) for h in heads], axis=0)


def _lane_tiles(pieces):
    return [p[:, k * LANES:(k + 1) * LANES] for p in pieces for k in range(p.shape[1] // LANES)]


def _sink_softmax(pieces, sink_col):
    m = jnp.max(functools.reduce(jnp.maximum, _lane_tiles(pieces)), axis=-1, keepdims=True)
    m = jnp.maximum(m, sink_col)
    es = [jnp.exp2(s - m) for s in pieces]
    den = jnp.sum(functools.reduce(jnp.add, _lane_tiles(es)), axis=-1, keepdims=True) + jnp.exp2(sink_col - m)
    return es, 1.0 / den


def _store_group(a_ref, rows, g, pv, n):
    for pr in range(2):
        sub = slice(pr * n, (pr + 1) * n)
        a_ref[rows, _head_lanes(g, pr)] = (pv[0][sub, :] + pv[1][sub, :]).astype(BF16)


def _smp_attn_kernel(geo, j, sink_ref, x_ref, mod_ref, q_ref, kc_ref, kp_ref, kn_ref, vc_ref, vp_ref, vn_ref,
                     ck_ref, cv_ref, wo_ref, o_ref, a_ref, kall_ref, vall_ref):
    i = pl.program_id(0)
    tm = geo.tm
    n_qb = tm // BLOCK
    kwin = tm + 2 * BLOCK
    kall_ref[0:BLOCK, :] = kp_ref[...]
    kall_ref[BLOCK:BLOCK + tm, :] = kc_ref[...]
    kall_ref[BLOCK + tm:kwin, :] = kn_ref[...]
    vall_ref[0:BLOCK, :] = vp_ref[...]
    vall_ref[BLOCK:BLOCK + tm, :] = vc_ref[...]
    vall_ref[BLOCK + tm:kwin, :] = vn_ref[...]
    first = (i % geo.bps) == 0
    last = (i % geo.bps) == geo.bps - 1
    qi = lax.broadcasted_iota(jnp.int32, (Q_PER_KV * BLOCK, BLOCK), 0) & (BLOCK - 1)
    kj = lax.broadcasted_iota(jnp.int32, (Q_PER_KV * BLOCK, BLOCK), 1)
    for b in range(n_qb):
        rows = slice(b * BLOCK, (b + 1) * BLOCK)
        win = slice(b * BLOCK, (b + 3) * BLOCK)
        ok_prev = kj >= (qi + jnp.where(first, BLOCK, 0) if b == 0 else qi)
        ok_next = kj <= (qi - jnp.where(last, BLOCK, 0) if b == n_qb - 1 else qi)
        for g in range(N_KV_HEADS):
            q2 = _group_queries(q_ref, rows, g)
            sw = jnp.concatenate([_dot_nt(q2, kall_ref[win, _head_lanes(g, var)]) for var in range(2)], axis=0)
            sc = jnp.concatenate([_dot_nt(q2, ck_ref[:, _head_lanes(g, var)]) for var in range(2)], axis=0)
            pieces = [jnp.where(ok_prev, sw[:, 0:BLOCK], NEG_INF), sw[:, BLOCK:2 * BLOCK],
                      jnp.where(ok_next, sw[:, 2 * BLOCK:3 * BLOCK], NEG_INF), sc]
            es, rinv = _sink_softmax(pieces, _group_sinks(sink_ref, j, g, BLOCK))
            ew = jnp.concatenate(es[:3], axis=-1).astype(BF16)
            ec = es[3].astype(BF16)
            pv = []
            for var in range(2):
                sub = slice(var * 2 * BLOCK, (var + 1) * 2 * BLOCK)
                lanes = _head_lanes(g, var)
                pv.append((_dot(ew[sub, :], vall_ref[win, lanes]) + _dot(ec[sub, :], cv_ref[:, lanes]))
                          * rinv[sub, :])
            _store_group(a_ref, rows, g, pv, BLOCK)
    y = _dot(a_ref[...], wo_ref[...])
    o_ref[...] = x_ref[...] + _mod_row(mod_ref, 1 + i // geo.bps, 2) * y


def _smp_attn_call(x, layer, j, mods, sink, q, kx, vx, ckx, cvx, w_o, sides=()):
    geo = ROWS_NARROW
    prev, nxt = geo.halo_specs(KV_EXT, BLOCK, offset=geo.ncb)
    ctx_spec = pl.BlockSpec((None, None, PAST_LEN, KV_EXT), lambda i: (i // geo.bps, j, 0, 0))
    kwin = geo.tm + 2 * BLOCK
    return _hosted_call(
        functools.partial(_smp_attn_kernel, geo, j), geo.nb - geo.ncb,
        in_specs=[
            pl.BlockSpec(memory_space=pltpu.SMEM),
            geo.row_spec(D_MODEL, geo.ncb),
            _mod_spec(layer),
            geo.row_spec(Q_DIM, geo.ncb),
            geo.row_spec(KV_EXT, geo.ncb), prev, nxt,
            geo.row_spec(KV_EXT, geo.ncb), prev, nxt,
            ctx_spec, ctx_spec,
            _resident((Q_DIM, D_MODEL)),
        ],
        out_specs=[geo.row_spec(D_MODEL, geo.ncb)],
        out_shape=[jax.ShapeDtypeStruct((N_ROWS, D_MODEL), F32)],
        scratch_shapes=[pltpu.VMEM((geo.tm, Q_DIM), BF16), pltpu.VMEM((kwin, KV_EXT), BF16),
                        pltpu.VMEM((kwin, KV_EXT), BF16)],
        sides=sides, name="smp_attn", aliases={1: 0},
    )(sink, x, mods, q, kx, kx, kx, vx, vx, vx, ckx, cvx, w_o, *[s.src for s in sides])


def _ext_heads(t):
    lead = t.shape[:-1]
    h = t.reshape(lead + (N_KV_HEADS, 1, HEAD_DIM))
    z = jnp.zeros_like(h)
    e = jnp.concatenate([jnp.concatenate([h, z], axis=-1), jnp.concatenate([z, h], axis=-1)], axis=-2)
    return e.reshape(lead + (KV_EXT,))


N_FT = D_FF // TN


def _ffn_kernel(geo, split_out, x_ref, xp_ref, xn_ref, g_ref, mod_ref, win_ref, cw_ref, cb_ref, wo_ref, *refs):
    i = pl.program_id(0)
    tm = geo.tm
    ext = tm + 2 * HALO
    hext_ref, gbuf_ref = refs[-2:]
    cond = geo.cond_idx(i)
    g = g_ref[...]
    sh = _mod_row(mod_ref, cond, 3)
    sc = _mod_row(mod_ref, cond, 4)
    x = x_ref[...]
    hext_ref[0:tm, :] = _norm_mod(x, g, sc, sh).astype(BF16)
    hext_ref[tm:tm + HALO, :] = _norm_mod(xn_ref[...], g, sc, sh).astype(BF16)
    hext_ref[tm + HALO:ext, :] = _norm_mod(xp_ref[...], g, sc, sh).astype(BF16)
    seq = geo.seq_len(i)
    pos = (i * tm + lax.broadcasted_iota(jnp.int32, (tm, TN), 0)) & (seq - 1)
    has_prev = pos != 0
    has_next = pos != seq - 1
    for j in range(N_FT):
        sl = slice(j * TN, (j + 1) * TN)
        a = _dot(hext_ref[...], win_ref[:, sl])
        u = _dot(hext_ref[0:tm, :], win_ref[:, D_FF + j * TN:D_FF + (j + 1) * TN])
        a_prev = jnp.where(has_prev, pltpu.roll(a, 1, 0)[:tm, :], 0.0)
        a_next = jnp.where(has_next, pltpu.roll(a, ext - 1, 0)[:tm, :], 0.0)
        a = a_prev * cw_ref[0:1, sl] + a[:tm, :] * cw_ref[1:2, sl] + a_next * cw_ref[2:3, sl] + cb_ref[:, sl]
        gbuf_ref[:, sl] = (_gelu(a) * u).astype(BF16)
    y = _dot(gbuf_ref[...], wo_ref[...])
    res = x + _mod_row(mod_ref, cond, 5) * y
    if split_out:
        oc_ref, os_ref = refs[:2]

        @pl.when(i < geo.ncb)
        def _():
            oc_ref[...] = res

        @pl.when(i >= geo.ncb)
        def _():
            os_ref[...] = res
    else:
        refs[0][...] = res


def _ffn_call(x, layer, mods, g_ffn, w_in, conv_w, conv_b, w_out, split_out, sides=()):
    geo = ROWS_WIDE
    prev, nxt = geo.halo_specs(D_MODEL, HALO)
    if split_out:
        out_specs = [geo.ctx_rows_spec(D_MODEL), geo.smp_rows_spec(D_MODEL)]
        out_shape = [jax.ShapeDtypeStruct((N_CTX_ROWS, D_MODEL), F32),
                     jax.ShapeDtypeStruct((N_SMP_ROWS, D_MODEL), F32)]
    else:
        out_specs = [geo.row_spec(D_MODEL)]
        out_shape = [jax.ShapeDtypeStruct((N_ROWS, D_MODEL), F32)]
    return _hosted_call(
        functools.partial(_ffn_kernel, geo, split_out), geo.nb,
        in_specs=[
            geo.row_spec(D_MODEL), prev, nxt,
            _resident((1, D_MODEL), layer),
            _mod_spec(layer),
            _resident((D_MODEL, 2 * D_FF)),
            _resident((CONV_W, D_FF), layer),
            _resident((1, D_FF), layer),
            _resident((D_FF, D_MODEL)),
        ],
        out_specs=out_specs,
        out_shape=out_shape,
        scratch_shapes=[pltpu.VMEM((geo.tm + 2 * HALO, D_MODEL), BF16), pltpu.VMEM((geo.tm, D_FF), BF16)],
        sides=sides, name="ffn",
    )(x, x, x, g_ffn, mods, w_in, conv_w, conv_b, w_out, *[s.src for s in sides])


def kernel(x_prompt, x_sample, cache_k, cache_v, c, c_ctx, w_ada, b_ada, g_mix, g_ffn, w_ffn_in, ffn_conv_w,
           ffn_conv_b, w_ffn_out, a_w_in, a_g_v, a_w_s, a_b_s, a_w_out, p_w, p_b, p_scale, c_w_qkv, c_g_q,
           c_g_k, c_sink, c_w_o):
    x = (x_prompt.reshape(N_CTX_ROWS, D_MODEL), x_sample.reshape(N_SMP_ROWS, D_MODEL))

    cond8 = jnp.concatenate([c_ctx[None, :], c, jnp.zeros((SUBLANES - N_COND, D_MODEL), F32)], axis=0)
    mods = _ada_call(cond8, w_ada, b_ada)

    g_mix, g_ffn = g_mix[:, None, :], g_ffn[:, None, :]
    ffn_conv_b = ffn_conv_b[:, None, :]
    a_w_s, a_g_v, a_b_s = a_w_s.astype(BF16), a_g_v[:, None, :], a_b_s[:, :, :, None]
    p_w, p_b, p_scale = p_w.astype(BF16), p_b[:, None, :], p_scale[:, None, :]
    gq = jnp.tile(c_g_q, (1, TN // HEAD_DIM))[:, None, :]
    gk = jnp.tile(c_g_k, (1, TN // HEAD_DIM))[:, None, :]
    bd = jnp.asarray(np.kron(np.eye(TN // HEAD_DIM), np.ones((HEAD_DIM, HEAD_DIM))), dtype=BF16)
    sink = c_sink.reshape(N_C_LAYERS * N_HEADS)
    ckx = _ext_heads(cache_k.reshape(DEC_BATCH, N_C_LAYERS, PAST_LEN, KV_DIM)).astype(BF16)
    cvx = _ext_heads(cache_v.reshape(DEC_BATCH, N_C_LAYERS, PAST_LEN, KV_DIM)).astype(BF16)

    def ffn_sides(layer, n_in, axis_in, n_out):
        return (_Side(w_ffn_in, layer, axis_in, n_in), _Side(w_ffn_out, layer, 0, n_out))

    assert DEPTH == 4 and N_MIXERS == 3, "the hosting plan below is written for gMLP, pool, attention, gMLP"
    gw = (a_w_in[0].astype(BF16), a_w_out[0].astype(BF16))
    x, *fw = _gmlp_call(x, 0, 0, mods, g_mix, gw[0], a_g_v, a_w_s, a_b_s, gw[1],
                        sides=ffn_sides(0, 2 * N_FT, 1, 2 * N_FT))
    x, *fw = _ffn_call(x, 0, mods, g_ffn, fw[0], ffn_conv_w, ffn_conv_b, fw[1], False,
                       sides=ffn_sides(1, N_FT, 1, N_FT))
    x, w_qkv, w_o, *fw_last = _pool_call(
        x, 1, 0, mods, g_mix, p_w, p_b, p_scale,
        sides=(_Side(c_w_qkv, 0, 1, 12), _Side(c_w_o, 0, 0, 8)) + ffn_sides(3, N_FT, 1, N_FT))
    x, *fw = _ffn_call(x, 1, mods, g_ffn, fw[0], ffn_conv_w, ffn_conv_b, fw[1], False,
                       sides=ffn_sides(2, N_FT, 1, N_FT))
    q, kx, vx, kn, v = _qkv_call(x, 2, 0, mods, g_mix, w_qkv, gq, gk, bd)
    new_k = kn.reshape(BATCH, 1, SEQ, N_KV_HEADS, HEAD_DIM)
    new_v = v.reshape(BATCH, 1, SEQ, N_KV_HEADS, HEAD_DIM)
    (x,) = _ctx_attn_call(x, 2, 0, mods, sink, q, kx, vx, w_o)
    (x,) = _smp_attn_call(x, 2, 0, mods, sink, q, kx, vx, ckx, cvx, w_o)
    x, *gw = _ffn_call(x, 2, mods, g_ffn, fw[0], ffn_conv_w, ffn_conv_b, fw[1], False,
                       sides=(_Side(a_w_in, 1, 1, 8), _Side(a_w_out, 1, 0, 8)))
    (x,) = _gmlp_call(x, 3, 1, mods, g_mix, gw[0], a_g_v, a_w_s, a_b_s, gw[1])
    x = _ffn_call(x, 3, mods, g_ffn, fw_last[0], ffn_conv_w, ffn_conv_b, fw_last[1], True)

    y_prompt = x[0].reshape(BATCH, SEQ, D_MODEL)
    y_sample = x[1].reshape(DEC_BATCH, DEC_SEQ, D_MODEL)
    return (y_prompt, y_sample, new_k, new_v)
```

```python
import functools
from typing import NamedTuple

import numpy as np
import jax
import jax.numpy as jnp
from jax import lax
from jax.experimental import pallas as pl
from jax.experimental.pallas import tpu as pltpu

D_MODEL = 1024
BATCH = 32
SEQ = 256
DEPTH = 4
DEC_BATCH = 2
DEC_SEQ = 2048
PAST_LEN = 256
GRID_W = 64
N_MIXERS = 3
CHUNK = 128
D_GMLP = 2 * D_MODEL
N_GROUPS_A = 8
D_GROUP_A = D_GMLP // N_GROUPS_A
POOL_WINDOWS = (2, 4, 8, 16)
N_POOL_GROUPS = 4
D_POOL_GROUP = D_MODEL // N_POOL_GROUPS
N_HEADS = 16
N_C_LAYERS = DEPTH // N_MIXERS
N_KV_HEADS = 4
HEAD_DIM = 64
Q_PER_KV = N_HEADS // N_KV_HEADS
WINDOW = 128
BLOCK = 128
ROPE_THETA = 10000.0
D_FF = 2816
CONV_W = 3
EPS = 1e-6
NEG_INF = -1e30
LOG2E = 1.4426950408889634

F32 = jnp.float32
BF16 = jnp.bfloat16

N_CTX_ROWS = BATCH * SEQ
N_SMP_ROWS = DEC_BATCH * DEC_SEQ
N_ROWS = N_CTX_ROWS + N_SMP_ROWS
N_COND = 1 + DEC_BATCH
N_MOD = 6
HALO = 16
SUBLANES = 8
LANES = 128
TN = 256
Q_DIM = N_HEADS * HEAD_DIM
KV_DIM = N_KV_HEADS * HEAD_DIM
KV_EXT = N_KV_HEADS * 2 * LANES
VMEM_CAP = 56 * 1024 * 1024
VMEM_TEMPS = 2 * 1024 * 1024


class _Rows:
    def __init__(self, tm):
        self.tm = tm
        self.nb = N_ROWS // tm
        self.ncb = N_CTX_ROWS // tm
        self.bps = DEC_SEQ // tm

    def cond_idx(self, i):
        return jnp.where(i < self.ncb, 0, 1 + (i - self.ncb) // self.bps)

    def seq_len(self, i):
        return jnp.where(i < self.ncb, SEQ, DEC_SEQ)

    def row_spec(self, width, offset=0):
        return pl.BlockSpec((self.tm, width), lambda i: (i + offset, 0))

    def ctx_rows_spec(self, width):
        return pl.BlockSpec((self.tm, width), lambda i: (jnp.minimum(i, self.ncb - 1), 0))

    def smp_rows_spec(self, width):
        return pl.BlockSpec((self.tm, width), lambda i: (jnp.maximum(i - self.ncb, 0), 0))

    def halo_specs(self, width, rows, offset=0):
        per = self.tm // rows
        last = N_ROWS // rows - 1
        prev = pl.BlockSpec((rows, width), lambda i: (jnp.maximum((i + offset) * per - 1, 0), 0))
        nxt = pl.BlockSpec((rows, width), lambda i: (jnp.minimum((i + offset + 1) * per, last), 0))
        return prev, nxt


ROWS_WIDE = _Rows(1024)
ROWS_NARROW = _Rows(512)


def _gelu(x):
    return 0.5 * x * (1.0 + jnp.tanh(0.7978845608028654 * (x + 0.044715 * (x * x * x))))


def _norm_mod(x, g, sc, sh):
    ms = jnp.mean(x * x, axis=-1, keepdims=True)
    return x * lax.rsqrt(ms + EPS) * (g * (1.0 + sc)) + sh


def _mod_row(mod_ref, cond, k):
    return mod_ref[pl.ds(cond, 1), k * D_MODEL:(k + 1) * D_MODEL]


def _dot(a, b):
    return jnp.dot(a, b, preferred_element_type=F32)


def _dot_nt(a, b):
    return lax.dot_general(a, b, (((1,), (1,)), ((), ())), preferred_element_type=F32)


def _padded_bytes(shape, dtype):
    item = jnp.dtype(dtype).itemsize
    dims = [1 if d is None else d for d in shape]
    dims[-1] = -(-dims[-1] // LANES) * LANES
    if len(dims) > 1:
        rows = SUBLANES * 4 // item
        dims[-2] = -(-dims[-2] // rows) * rows
    return item * int(np.prod(dims))


def _vmem_limit(specs, dtypes, scratch_shapes):
    total = 0
    for spec, dtype in zip(specs, dtypes):
        if spec.block_shape is not None:
            buffers = 2 if spec.pipeline_mode is None else spec.pipeline_mode.buffer_count
            total += buffers * _padded_bytes(spec.block_shape, dtype)
    total += sum(_padded_bytes(s.shape, s.dtype) for s in scratch_shapes)
    assert total + VMEM_TEMPS <= VMEM_CAP, f"declared VMEM buffers too large: {total} bytes"
    return VMEM_CAP


def _resident(shape, layer=None):
    nd = len(shape)
    if layer is None:
        return pl.BlockSpec(shape, lambda *_: (0,) * nd, pipeline_mode=pl.Buffered(1))
    return pl.BlockSpec((None,) + tuple(shape), lambda *_: (layer,) + (0,) * nd, pipeline_mode=pl.Buffered(1))


def _mod_spec(layer):
    return _resident((SUBLANES, N_MOD * D_MODEL), layer)


class _Side(NamedTuple):
    src: jax.Array
    layer: int
    axis: int
    n: int


def _side_plumbing(sides):
    in_specs, out_specs, out_shape = [], [], []
    for s in sides:
        _, r, c = s.src.shape
        blk = (r // s.n, c) if s.axis == 0 else (r, c // s.n)

        def idx(i, s=s):
            k = jnp.minimum(i, s.n - 1)
            return (k, 0) if s.axis == 0 else (0, k)

        in_specs.append(pl.BlockSpec((None,) + blk, lambda i, s=s, idx=idx: (s.layer,) + idx(i)))
        out_specs.append(pl.BlockSpec(blk, idx))
        out_shape.append(jax.ShapeDtypeStruct((r, c), BF16))
    return in_specs, out_specs, out_shape


def _with_sides(body, n_in, n_out, n_side):
    def kernel(*refs):
        ins, rest = refs[:n_in], refs[n_in:]
        side_in, rest = rest[:n_side], rest[n_side:]
        outs, rest = rest[:n_out], rest[n_out:]
        side_out, scratch = rest[:n_side], rest[n_side:]
        for src, dst in zip(side_in, side_out):
            dst[...] = src[...].astype(BF16)
        body(*ins, *outs, *scratch)
    return kernel


def _hosted_call(body, geo_steps, in_specs, out_specs, out_shape, scratch_shapes, sides, name, aliases=None):
    s_in, s_out, s_shape = _side_plumbing(sides)
    assert all(s.n <= geo_steps for s in sides)
    in_specs, out_specs, out_shape = list(in_specs) + s_in, list(out_specs) + s_out, list(out_shape) + s_shape

    def run(*operands):
        dtypes = [o.dtype for o in operands] + [o.dtype for o in out_shape]
        return pl.pallas_call(
            _with_sides(body, len(in_specs) - len(sides), len(out_specs) - len(sides), len(sides)),
            grid=(geo_steps,),
            in_specs=in_specs,
            out_specs=out_specs,
            out_shape=out_shape,
            input_output_aliases=aliases or {},
            scratch_shapes=scratch_shapes,
            compiler_params=pltpu.CompilerParams(
                dimension_semantics=("arbitrary",),
                vmem_limit_bytes=_vmem_limit(in_specs + out_specs, dtypes, scratch_shapes)),
            name=name,
        )(*operands)

    return run


ADA_TN = 1536


def _ada_kernel(c_ref, w_ref, b_ref, o_ref):
    c = c_ref[...]
    s = (c * (1.0 / (1.0 + jnp.exp(-c)))).astype(BF16)
    o_ref[...] = _dot(s, w_ref[...].astype(BF16)) + b_ref[...]


def _ada_call(cond8, w_ada, b_ada):
    nt = N_MOD * D_MODEL // ADA_TN
    return pl.pallas_call(
        _ada_kernel,
        grid=(DEPTH, nt),
        in_specs=[
            pl.BlockSpec((SUBLANES, D_MODEL), lambda l, j: (0, 0)),
            pl.BlockSpec((None, D_MODEL, ADA_TN), lambda l, j: (l, 0, j)),
            pl.BlockSpec((None, 1, ADA_TN), lambda l, j: (l, 0, j)),
        ],
        out_specs=pl.BlockSpec((None, SUBLANES, ADA_TN), lambda l, j: (l, 0, j)),
        out_shape=jax.ShapeDtypeStruct((DEPTH, SUBLANES, N_MOD * D_MODEL), F32),
        compiler_params=pltpu.CompilerParams(dimension_semantics=("arbitrary", "arbitrary"),
                                             vmem_limit_bytes=2 * _padded_bytes((D_MODEL, ADA_TN), F32) + VMEM_TEMPS),
        name="ada",
    )(cond8, w_ada, b_ada.reshape(DEPTH, 1, N_MOD * D_MODEL))


def _gmlp_kernel(geo, split_in, *refs):
    i = pl.program_id(0)
    tm = geo.tm
    if split_in:
        xc_ref, xs_ref, *refs = refs
        x = jnp.where(i < geo.ncb, xc_ref[...], xs_ref[...])
    else:
        x_ref, *refs = refs
        x = x_ref[...]
    g_ref, mod_ref, win_ref, gv_ref, ws_ref, bs_ref, wout_ref, o_ref, h_ref, u_ref, v_ref, p_ref = refs
    cond = geo.cond_idx(i)
    h_ref[...] = _norm_mod(x, g_ref[...], _mod_row(mod_ref, cond, 1), _mod_row(mod_ref, cond, 0)).astype(BF16)
    ssq = jnp.zeros((tm, LANES), F32)
    for j in range(D_GMLP // TN):
        sl = slice(j * TN, (j + 1) * TN)
        u_ref[:, sl] = _gelu(_dot(h_ref[...], win_ref[:, sl]))
        v = _gelu(_dot(h_ref[...], win_ref[:, D_GMLP + j * TN:D_GMLP + (j + 1) * TN]))
        v_ref[:, sl] = v
        for k in range(TN // LANES):
            vk = v[:, k * LANES:(k + 1) * LANES]
            ssq = ssq + vk * vk
    r = lax.rsqrt(jnp.sum(ssq, axis=-1, keepdims=True) * (1.0 / D_GMLP) + EPS)
    for g in range(N_GROUPS_A):
        sl = slice(g * D_GROUP_A, (g + 1) * D_GROUP_A)
        vn = (v_ref[:, sl] * r * gv_ref[:, sl]).astype(BF16)
        for c in range(tm // CHUNK):
            rows = slice(c * CHUNK, (c + 1) * CHUNK)
            sv = _dot(ws_ref[g], vn[rows, :]) + bs_ref[g]
            p_ref[rows, sl] = (u_ref[rows, sl] * sv).astype(BF16)
    y = _dot(p_ref[...], wout_ref[...])
    o_ref[...] = x + _mod_row(mod_ref, cond, 2) * y


def _gmlp_call(x, layer, j, mods, g_mix, w_in, g_v, w_s, b_s, w_out, sides=()):
    split_in = isinstance(x, tuple)
    geo = ROWS_NARROW if split_in else ROWS_WIDE
    if split_in:
        x_specs = [geo.ctx_rows_spec(D_MODEL), geo.smp_rows_spec(D_MODEL)]
        xs = x
    else:
        x_specs = [geo.row_spec(D_MODEL)]
        xs = (x,)
    return _hosted_call(
        functools.partial(_gmlp_kernel, geo, split_in), geo.nb,
        in_specs=x_specs + [
            _resident((1, D_MODEL), layer),
            _mod_spec(layer),
            _resident((D_MODEL, 2 * D_GMLP)),
            _resident((1, D_GMLP), j),
            _resident((N_GROUPS_A, CHUNK, CHUNK), j),
            _resident((N_GROUPS_A, CHUNK, 1), j),
            _resident((D_GMLP, D_MODEL)),
        ],
        out_specs=[geo.row_spec(D_MODEL)],
        out_shape=[jax.ShapeDtypeStruct((N_ROWS, D_MODEL), F32)],
        scratch_shapes=[
            pltpu.VMEM((geo.tm, D_MODEL), BF16),
            pltpu.VMEM((geo.tm, D_GMLP), F32),
            pltpu.VMEM((geo.tm, D_GMLP), F32),
            pltpu.VMEM((geo.tm, D_GMLP), BF16),
        ],
        sides=sides, name="gmlp",
    )(*xs, g_mix, mods, w_in, g_v, w_s, b_s, w_out, *[s.src for s in sides])


def _pool_kernel(geo, x_ref, xp_ref, xn_ref, g_ref, mod_ref, pw_ref, pb_ref, ps_ref, o_ref, hext_ref):
    i = pl.program_id(0)
    tm = geo.tm
    ext = tm + 2 * HALO
    cond = geo.cond_idx(i)
    g = g_ref[...]
    sc = _mod_row(mod_ref, cond, 1)
    sh = _mod_row(mod_ref, cond, 0)
    gate = _mod_row(mod_ref, cond, 2)
    x = x_ref[...]
    hext_ref[0:tm, :] = _norm_mod(x, g, sc, sh)
    hext_ref[tm:tm + HALO, :] = _norm_mod(xn_ref[...], g, sc, sh)
    hext_ref[tm + HALO:ext, :] = _norm_mod(xp_ref[...], g, sc, sh)
    seq = geo.seq_len(i)
    e = lax.broadcasted_iota(jnp.int32, (ext, D_POOL_GROUP), 0)
    pos = (i * tm + jnp.where(e < tm + HALO, e, e - ext)) & (seq - 1)
    p = (i * tm + lax.broadcasted_iota(jnp.int32, (tm, D_POOL_GROUP), 0)) & (seq - 1)
    for grp in range(N_POOL_GROUPS):
        sl = slice(grp * D_POOL_GROUP, (grp + 1) * D_POOL_GROUP)
        half = POOL_WINDOWS[grp] // 2
        he = hext_ref[:, sl]
        fwd = he
        bwd = jnp.where(pos >= 1, pltpu.roll(he, 1, 0), 0.0)
        span = 1
        while span < half:
            fwd = fwd + jnp.where(pos < seq - span, pltpu.roll(fwd, ext - span, 0), 0.0)
            bwd = bwd + jnp.where(pos >= span, pltpu.roll(bwd, span, 0), 0.0)
            span *= 2
        cnt = (jnp.minimum(p + half - 1, seq - 1) - jnp.maximum(p - half, 0) + 1).astype(F32)
        d = (fwd[:tm, :] + bwd[:tm, :]) / cnt - he[:tm, :]
        y = (_dot(d.astype(BF16), pw_ref[grp]) + pb_ref[:, sl]) * ps_ref[:, sl]
        o_ref[:, sl] = x[:, sl] + gate[:, sl] * y


def _pool_call(x, layer, j, mods, g_mix, p_w, p_b, p_scale, sides=()):
    geo = ROWS_WIDE
    prev, nxt = geo.halo_specs(D_MODEL, HALO)
    return _hosted_call(
        functools.partial(_pool_kernel, geo), geo.nb,
        in_specs=[
            geo.row_spec(D_MODEL), prev, nxt,
            _resident((1, D_MODEL), layer),
            _mod_spec(layer),
            _resident((N_POOL_GROUPS, D_POOL_GROUP, D_POOL_GROUP), j),
            _resident((1, D_MODEL), j),
            _resident((1, D_MODEL), j),
        ],
        out_specs=[geo.row_spec(D_MODEL)],
        out_shape=[jax.ShapeDtypeStruct((N_ROWS, D_MODEL), F32)],
        scratch_shapes=[pltpu.VMEM((geo.tm + 2 * HALO, D_MODEL), F32)],
        sides=sides, name="pool",
    )(x, x, x, g_mix, mods, p_w, p_b, p_scale, *[s.src for s in sides])


def _rope_tables(tm):
    n_freq = HEAD_DIM // 4
    inv = ROPE_THETA ** (-np.arange(n_freq, dtype=np.float64) / n_freq)
    t = np.arange(DEC_SEQ)
    lane = np.arange(LANES) % HEAD_DIM
    quarter, f = lane // n_freq, lane % n_freq
    posn = np.where(quarter[None, :] < 2, (t // GRID_W)[:, None], (t % GRID_W)[:, None])
    ang = posn * inv[f][None, :]
    cos, sin = np.cos(ang), np.sin(ang)
    odd = (quarter % 2 == 1)[None, :]
    tab = np.concatenate([cos, np.where(odd, sin, -sin)], axis=1)
    ident = np.concatenate([np.ones((tm, LANES)), np.zeros((tm, LANES))], axis=1)
    return np.concatenate([ident, tab], axis=0).astype(np.float32)


def _head_meansq(t, bd_ref):
    sq = t * t
    hi = sq.astype(BF16)
    lo = (sq - hi.astype(F32)).astype(BF16)
    return _dot(hi, bd_ref[...]) + _dot(lo, bd_ref[...])


def _rope(t, tab_ref):
    q = HEAD_DIM // 4
    second = (lax.broadcasted_iota(jnp.int32, (1, LANES), 1) & q) != 0
    partner = jnp.where(second, pltpu.roll(t, q, 1), pltpu.roll(t, LANES - q, 1))
    return t * tab_ref[:, 0:LANES] + partner * tab_ref[:, LANES:2 * LANES]


def _kv_ext(t, o_ref):
    lo = lax.broadcasted_iota(jnp.int32, (t.shape[0], LANES), 1) < HEAD_DIM
    for m in range(KV_DIM // LANES):
        a = t[:, m * LANES:(m + 1) * LANES]
        r = pltpu.roll(a, HEAD_DIM, 1)
        base = m * 4 * LANES
        o_ref[:, base:base + LANES] = jnp.where(lo, a, 0.0).astype(o_ref.dtype)
        o_ref[:, base + LANES:base + 2 * LANES] = jnp.where(lo, 0.0, r).astype(o_ref.dtype)
        o_ref[:, base + 2 * LANES:base + 3 * LANES] = jnp.where(lo, r, 0.0).astype(o_ref.dtype)
        o_ref[:, base + 3 * LANES:base + 4 * LANES] = jnp.where(lo, 0.0, a).astype(o_ref.dtype)


def _store_head_rows(o_ref, t):
    for h in range(N_KV_HEADS):
        o_ref[pl.ds(h, t.shape[0], stride=N_KV_HEADS), :] = t[:, h * HEAD_DIM:(h + 1) * HEAD_DIM]


def _qkv_kernel(geo, x_ref, g_ref, mod_ref, w_ref, gq_ref, gk_ref, bd_ref, tab_ref,
                q_ref, kx_ref, vx_ref, kn_ref, v_ref, h_ref, kr_ref, kn_s, v_s):
    i = pl.program_id(0)
    cond = geo.cond_idx(i)
    h_ref[...] = _norm_mod(x_ref[...], g_ref[...], _mod_row(mod_ref, cond, 1),
                           _mod_row(mod_ref, cond, 0)).astype(BF16)
    scale = HEAD_DIM ** -0.5 * LOG2E
    n_q = Q_DIM // TN
    ts = [_dot(h_ref[...], w_ref[:, c * TN:(c + 1) * TN]) for c in range(n_q + 2)]
    for c in range(n_q):
        t = ts[c]
        t = t * lax.rsqrt(_head_meansq(t, bd_ref) + EPS) * (gq_ref[...] * scale)
        for m in range(TN // LANES):
            q_ref[:, c * TN + m * LANES:c * TN + (m + 1) * LANES] = _rope(
                t[:, m * LANES:(m + 1) * LANES], tab_ref).astype(BF16)
    t = ts[n_q]
    t = t * lax.rsqrt(_head_meansq(t, bd_ref) + EPS) * gk_ref[...]
    kn_s[...] = t
    for m in range(KV_DIM // LANES):
        kr_ref[:, m * LANES:(m + 1) * LANES] = _rope(t[:, m * LANES:(m + 1) * LANES], tab_ref)
    _kv_ext(kr_ref[...], kx_ref)
    v = ts[n_q + 1]
    v_s[...] = v
    _kv_ext(v, vx_ref)

    @pl.when(i < geo.ncb)
    def _():
        _store_head_rows(kn_ref, kn_s[...])
        _store_head_rows(v_ref, v_s[...])


def _qkv_call(x, layer, j, mods, g_mix, w_qkv, g_q, g_k, bd):
    geo = ROWS_WIDE
    tab = jnp.asarray(_rope_tables(geo.tm))
    tab_spec = pl.BlockSpec((geo.tm, 2 * LANES),
                            lambda i: (jnp.where(i < geo.ncb, 0, 1 + (i - geo.ncb) % geo.bps), 0))
    cache_spec = pl.BlockSpec((geo.tm * N_KV_HEADS, HEAD_DIM), lambda i: (jnp.minimum(i, geo.ncb - 1), 0))
    return _hosted_call(
        functools.partial(_qkv_kernel, geo), geo.nb,
        in_specs=[
            geo.row_spec(D_MODEL),
            _resident((1, D_MODEL), layer),
            _mod_spec(layer),
            _resident((D_MODEL, Q_DIM + 2 * KV_DIM)),
            _resident((1, TN), j),
            _resident((1, TN), j),
            _resident((TN, TN)),
            tab_spec,
        ],
        out_specs=[geo.row_spec(Q_DIM), geo.row_spec(KV_EXT), geo.row_spec(KV_EXT), cache_spec, cache_spec],
        out_shape=[
            jax.ShapeDtypeStruct((N_ROWS, Q_DIM), BF16),
            jax.ShapeDtypeStruct((N_ROWS, KV_EXT), BF16),
            jax.ShapeDtypeStruct((N_ROWS, KV_EXT), BF16),
            jax.ShapeDtypeStruct((N_CTX_ROWS * N_KV_HEADS, HEAD_DIM), F32),
            jax.ShapeDtypeStruct((N_CTX_ROWS * N_KV_HEADS, HEAD_DIM), F32),
        ],
        scratch_shapes=[pltpu.VMEM((geo.tm, D_MODEL), BF16)] + [pltpu.VMEM((geo.tm, KV_DIM), F32)] * 3,
        sides=(), name="qkv",
    )(x, g_mix, mods, w_qkv, g_q, g_k, bd, tab)


def _head_lanes(g, j):
    base = (g * 2 + j) * LANES
    return slice(base, base + LANES)


def _ctx_attn_kernel(geo, j, sink_ref, x_ref, mod_ref, q_ref, kx_ref, vx_ref, wo_ref, o_ref, a_ref):
    for s in range(geo.tm // SEQ):
        rows = slice(s * SEQ, (s + 1) * SEQ)
        for g in range(N_KV_HEADS):
            for pr in range(2):
                qp = q_ref[rows, _head_lanes(g, pr)]
                acc = jnp.zeros((SEQ, LANES), F32)
                for var in range(2):
                    sink = sink_ref[j * N_HEADS + g * Q_PER_KV + 2 * pr + var] * LOG2E
                    sc = _dot_nt(qp, kx_ref[rows, _head_lanes(g, var)])
                    m = jnp.maximum(jnp.max(sc, axis=-1, keepdims=True), sink)
                    e = jnp.exp2(sc - m)
                    den = jnp.sum(e, axis=-1, keepdims=True) + jnp.exp2(sink - m)
                    acc = acc + _dot(e.astype(BF16), vx_ref[rows, _head_lanes(g, var)]) / den
                a_ref[rows, _head_lanes(g, pr)] = acc.astype(BF16)
    y = _dot(a_ref[...], wo_ref[...])
    o_ref[...] = x_ref[...] + _mod_row(mod_ref, 0, 2) * y


def _ctx_attn_call(x, layer, j, mods, sink, q, kx, vx, w_o, sides=()):
    geo = ROWS_WIDE
    return _hosted_call(
        functools.partial(_ctx_attn_kernel, geo, j), geo.ncb,
        in_specs=[
            pl.BlockSpec(memory_space=pltpu.SMEM),
            geo.row_spec(D_MODEL),
            _mod_spec(layer),
            geo.row_spec(Q_DIM), geo.row_spec(KV_EXT), geo.row_spec(KV_EXT),
            _resident((Q_DIM, D_MODEL)),
        ],
        out_specs=[geo.row_spec(D_MODEL)],
        out_shape=[jax.ShapeDtypeStruct((N_ROWS, D_MODEL), F32)],
        scratch_shapes=[pltpu.VMEM((geo.tm, Q_DIM), BF16)],
        sides=sides, name="ctx_attn", aliases={1: 0},
    )(sink, x, mods, q, kx, vx, w_o, *[s.src for s in sides])


def _group_queries(q_ref, rows, g):
    return jnp.concatenate([q_ref[rows, _head_lanes(g, 0)], q_ref[rows, _head_lanes(g, 1)]], axis=0)


def _group_sinks(sink_ref, j, g, n):
    heads = (0, 2, 1, 3)
    return jnp.concatenate(
        [jnp.full((n, 1), sink_ref[j * N_HEADS + g * Q_PER_KV + h] * LOG2E, F32) for h in heads], axis=0)


def _lane_tiles(pieces):
    return [p[:, k * LANES:(k + 1) * LANES] for p in pieces for k in range(p.shape[1] // LANES)]


def _sink_softmax(pieces, sink_col):
    m = jnp.max(functools.reduce(jnp.maximum, _lane_tiles(pieces)), axis=-1, keepdims=True)
    m = jnp.maximum(m, sink_col)
    es = [jnp.exp2(s - m) for s in pieces]
    den = jnp.sum(functools.reduce(jnp.add, _lane_tiles(es)), axis=-1, keepdims=True) + jnp.exp2(sink_col - m)
    return es, 1.0 / den


def _store_group(a_ref, rows, g, pv, n):
    for pr in range(2):
        sub = slice(pr * n, (pr + 1) * n)
        a_ref[rows, _head_lanes(g, pr)] = (pv[0][sub, :] + pv[1][sub, :]).astype(BF16)


def _smp_attn_kernel(geo, j, sink_ref, x_ref, mod_ref, q_ref, kc_ref, kp_ref, kn_ref, vc_ref, vp_ref, vn_ref,
                     ck_ref, cv_ref, wo_ref, o_ref, a_ref, kall_ref, vall_ref):
    i = pl.program_id(0)
    tm = geo.tm
    n_qb = tm // BLOCK
    kwin = tm + 2 * BLOCK
    kall_ref[0:BLOCK, :] = kp_ref[...]
    kall_ref[BLOCK:BLOCK + tm, :] = kc_ref[...]
    kall_ref[BLOCK + tm:kwin, :] = kn_ref[...]
    vall_ref[0:BLOCK, :] = vp_ref[...]
    vall_ref[BLOCK:BLOCK + tm, :] = vc_ref[...]
    vall_ref[BLOCK + tm:kwin, :] = vn_ref[...]
    first = (i % geo.bps) == 0
    last = (i % geo.bps) == geo.bps - 1
    qi = lax.broadcasted_iota(jnp.int32, (Q_PER_KV * BLOCK, BLOCK), 0) & (BLOCK - 1)
    kj = lax.broadcasted_iota(jnp.int32, (Q_PER_KV * BLOCK, BLOCK), 1)
    for b in range(n_qb):
        rows = slice(b * BLOCK, (b + 1) * BLOCK)
        win = slice(b * BLOCK, (b + 3) * BLOCK)
        ok_prev = kj >= (qi + jnp.where(first, BLOCK, 0) if b == 0 else qi)
        ok_next = kj <= (qi - jnp.where(last, BLOCK, 0) if b == n_qb - 1 else qi)
        for g in range(N_KV_HEADS):
            q2 = _group_queries(q_ref, rows, g)
            sw = jnp.concatenate([_dot_nt(q2, kall_ref[win, _head_lanes(g, var)]) for var in range(2)], axis=0)
            sc = jnp.concatenate([_dot_nt(q2, ck_ref[:, _head_lanes(g, var)]) for var in range(2)], axis=0)
            pieces = [jnp.where(ok_prev, sw[:, 0:BLOCK], NEG_INF), sw[:, BLOCK:2 * BLOCK],
                      jnp.where(ok_next, sw[:, 2 * BLOCK:3 * BLOCK], NEG_INF), sc]
            es, rinv = _sink_softmax(pieces, _group_sinks(sink_ref, j, g, BLOCK))
            ew = jnp.concatenate(es[:3], axis=-1).astype(BF16)
            ec = es[3].astype(BF16)
            pv = []
            for var in range(2):
                sub = slice(var * 2 * BLOCK, (var + 1) * 2 * BLOCK)
                lanes = _head_lanes(g, var)
                pv.append((_dot(ew[sub, :], vall_ref[win, lanes]) + _dot(ec[sub, :], cv_ref[:, lanes]))
                          * rinv[sub, :])
            _store_group(a_ref, rows, g, pv, BLOCK)
    y = _dot(a_ref[...], wo_ref[...])
    o_ref[...] = x_ref[...] + _mod_row(mod_ref, 1 + i // geo.bps, 2) * y


def _smp_attn_call(x, layer, j, mods, sink, q, kx, vx, ckx, cvx, w_o, sides=()):
    geo = ROWS_NARROW
    prev, nxt = geo.halo_specs(KV_EXT, BLOCK, offset=geo.ncb)
    ctx_spec = pl.BlockSpec((None, None, PAST_LEN, KV_EXT), lambda i: (i // geo.bps, j, 0, 0))
    kwin = geo.tm + 2 * BLOCK
    return _hosted_call(
        functools.partial(_smp_attn_kernel, geo, j), geo.nb - geo.ncb,
        in_specs=[
            pl.BlockSpec(memory_space=pltpu.SMEM),
            geo.row_spec(D_MODEL, geo.ncb),
            _mod_spec(layer),
            geo.row_spec(Q_DIM, geo.ncb),
            geo.row_spec(KV_EXT, geo.ncb), prev, nxt,
            geo.row_spec(KV_EXT, geo.ncb), prev, nxt,
            ctx_spec, ctx_spec,
            _resident((Q_DIM, D_MODEL)),
        ],
        out_specs=[geo.row_spec(D_MODEL, geo.ncb)],
        out_shape=[jax.ShapeDtypeStruct((N_ROWS, D_MODEL), F32)],
        scratch_shapes=[pltpu.VMEM((geo.tm, Q_DIM), BF16), pltpu.VMEM((kwin, KV_EXT), BF16),
                        pltpu.VMEM((kwin, KV_EXT), BF16)],
        sides=sides, name="smp_attn", aliases={1: 0},
    )(sink, x, mods, q, kx, kx, kx, vx, vx, vx, ckx, cvx, w_o, *[s.src for s in sides])


def _ext_heads(t):
    lead = t.shape[:-1]
    h = t.reshape(lead + (N_KV_HEADS, 1, HEAD_DIM))
    z = jnp.zeros_like(h)
    e = jnp.concatenate([jnp.concatenate([h, z], axis=-1), jnp.concatenate([z, h], axis=-1)], axis=-2)
    return e.reshape(lead + (KV_EXT,))


N_FT = D_FF // TN


def _ffn_kernel(geo, split_out, x_ref, xp_ref, xn_ref, g_ref, mod_ref, win_ref, cw_ref, cb_ref, wo_ref, *refs):
    i = pl.program_id(0)
    tm = geo.tm
    ext = tm + 2 * HALO
    hext_ref, gbuf_ref = refs[-2:]
    cond = geo.cond_idx(i)
    g = g_ref[...]
    sh = _mod_row(mod_ref, cond, 3)
    sc = _mod_row(mod_ref, cond, 4)
    x = x_ref[...]
    hext_ref[0:tm, :] = _norm_mod(x, g, sc, sh).astype(BF16)
    hext_ref[tm:tm + HALO, :] = _norm_mod(xn_ref[...], g, sc, sh).astype(BF16)
    hext_ref[tm + HALO:ext, :] = _norm_mod(xp_ref[...], g, sc, sh).astype(BF16)
    seq = geo.seq_len(i)
    pos = (i * tm + lax.broadcasted_iota(jnp.int32, (tm, TN), 0)) & (seq - 1)
    has_prev = pos != 0
    has_next = pos != seq - 1
    for j in range(N_FT):
        sl = slice(j * TN, (j + 1) * TN)
        a = _dot(hext_ref[...], win_ref[:, sl])
        u = _dot(hext_ref[0:tm, :], win_ref[:, D_FF + j * TN:D_FF + (j + 1) * TN])
        a_prev = jnp.where(has_prev, pltpu.roll(a, 1, 0)[:tm, :], 0.0)
        a_next = jnp.where(has_next, pltpu.roll(a, ext - 1, 0)[:tm, :], 0.0)
        a = a_prev * cw_ref[0:1, sl] + a[:tm, :] * cw_ref[1:2, sl] + a_next * cw_ref[2:3, sl] + cb_ref[:, sl]
        gbuf_ref[:, sl] = (_gelu(a) * u).astype(BF16)
    y = _dot(gbuf_ref[...], wo_ref[...])
    res = x + _mod_row(mod_ref, cond, 5) * y
    if split_out:
        oc_ref, os_ref = refs[:2]

        @pl.when(i < geo.ncb)
        def _():
            oc_ref[...] = res

        @pl.when(i >= geo.ncb)
        def _():
            os_ref[...] = res
    else:
        refs[0][...] = res


def _ffn_call(x, layer, mods, g_ffn, w_in, conv_w, conv_b, w_out, split_out, sides=()):
    geo = ROWS_WIDE
    prev, nxt = geo.halo_specs(D_MODEL, HALO)
    if split_out:
        out_specs = [geo.ctx_rows_spec(D_MODEL), geo.smp_rows_spec(D_MODEL)]
        out_shape = [jax.ShapeDtypeStruct((N_CTX_ROWS, D_MODEL), F32),
                     jax.ShapeDtypeStruct((N_SMP_ROWS, D_MODEL), F32)]
    else:
        out_specs = [geo.row_spec(D_MODEL)]
        out_shape = [jax.ShapeDtypeStruct((N_ROWS, D_MODEL), F32)]
    return _hosted_call(
        functools.partial(_ffn_kernel, geo, split_out), geo.nb,
        in_specs=[
            geo.row_spec(D_MODEL), prev, nxt,
            _resident((1, D_MODEL), layer),
            _mod_spec(layer),
            _resident((D_MODEL, 2 * D_FF)),
            _resident((CONV_W, D_FF), layer),
            _resident((1, D_FF), layer),
            _resident((D_FF, D_MODEL)),
        ],
        out_specs=out_specs,
        out_shape=out_shape,
        scratch_shapes=[pltpu.VMEM((geo.tm + 2 * HALO, D_MODEL), BF16), pltpu.VMEM((geo.tm, D_FF), BF16)],
        sides=sides, name="ffn",
    )(x, x, x, g_ffn, mods, w_in, conv_w, conv_b, w_out, *[s.src for s in sides])


def kernel(x_prompt, x_sample, cache_k, cache_v, c, c_ctx, w_ada, b_ada, g_mix, g_ffn, w_ffn_in, ffn_conv_w,
           ffn_conv_b, w_ffn_out, a_w_in, a_g_v, a_w_s, a_b_s, a_w_out, p_w, p_b, p_scale, c_w_qkv, c_g_q,
           c_g_k, c_sink, c_w_o):
    x = (x_prompt.reshape(N_CTX_ROWS, D_MODEL), x_sample.reshape(N_SMP_ROWS, D_MODEL))

    cond8 = jnp.concatenate([c_ctx[None, :], c, jnp.zeros((SUBLANES - N_COND, D_MODEL), F32)], axis=0)
    mods = _ada_call(cond8, w_ada, b_ada)

    g_mix, g_ffn = g_mix[:, None, :], g_ffn[:, None, :]
    ffn_conv_b = ffn_conv_b[:, None, :]
    a_w_s, a_g_v, a_b_s = a_w_s.astype(BF16), a_g_v[:, None, :], a_b_s[:, :, :, None]
    p_w, p_b, p_scale = p_w.astype(BF16), p_b[:, None, :], p_scale[:, None, :]
    gq = jnp.tile(c_g_q, (1, TN // HEAD_DIM))[:, None, :]
    gk = jnp.tile(c_g_k, (1, TN // HEAD_DIM))[:, None, :]
    bd = jnp.asarray(np.kron(np.eye(TN // HEAD_DIM), np.full((HEAD_DIM, HEAD_DIM), 1.0 / HEAD_DIM)), dtype=BF16)
    sink = c_sink.reshape(N_C_LAYERS * N_HEADS)
    ckx = _ext_heads(cache_k.reshape(DEC_BATCH, N_C_LAYERS, PAST_LEN, KV_DIM)).astype(BF16)
    cvx = _ext_heads(cache_v.reshape(DEC_BATCH, N_C_LAYERS, PAST_LEN, KV_DIM)).astype(BF16)

    def ffn_sides(layer, n_in, axis_in, n_out):
        return (_Side(w_ffn_in, layer, axis_in, n_in), _Side(w_ffn_out, layer, 0, n_out))

    assert DEPTH == 4 and N_MIXERS == 3, "the hosting plan below is written for gMLP, pool, attention, gMLP"
    gw = (a_w_in[0].astype(BF16), a_w_out[0].astype(BF16))
    x, *fw = _gmlp_call(x, 0, 0, mods, g_mix, gw[0], a_g_v, a_w_s, a_b_s, gw[1],
                        sides=ffn_sides(0, 2 * N_FT, 1, 2 * N_FT))
    x, *fw = _ffn_call(x, 0, mods, g_ffn, fw[0], ffn_conv_w, ffn_conv_b, fw[1], False,
                       sides=ffn_sides(1, N_FT, 1, N_FT))
    x, w_qkv, w_o, *fw_last = _pool_call(
        x, 1, 0, mods, g_mix, p_w, p_b, p_scale,
        sides=(_Side(c_w_qkv, 0, 1, 12), _Side(c_w_o, 0, 0, 8)) + ffn_sides(3, N_FT, 1, N_FT))
    x, *fw = _ffn_call(x, 1, mods, g_ffn, fw[0], ffn_conv_w, ffn_conv_b, fw[1], False,
                       sides=ffn_sides(2, N_FT, 1, N_FT))
    q, kx, vx, kn, v = _qkv_call(x, 2, 0, mods, g_mix, w_qkv, gq, gk, bd)
    new_k = kn.reshape(BATCH, 1, SEQ, N_KV_HEADS, HEAD_DIM)
    new_v = v.reshape(BATCH, 1, SEQ, N_KV_HEADS, HEAD_DIM)
    (x,) = _ctx_attn_call(x, 2, 0, mods, sink, q, kx, vx, w_o)
    (x,) = _smp_attn_call(x, 2, 0, mods, sink, q, kx, vx, ckx, cvx, w_o)
    x, *gw = _ffn_call(x, 2, mods, g_ffn, fw[0], ffn_conv_w, ffn_conv_b, fw[1], False,
                       sides=(_Side(a_w_in, 1, 1, 8), _Side(a_w_out, 1, 0, 8)))
    (x,) = _gmlp_call(x, 3, 1, mods, g_mix, gw[0], a_g_v, a_w_s, a_b_s, gw[1])
    x = _ffn_call(x, 3, mods, g_ffn, fw_last[0], ffn_conv_w, ffn_conv_b, fw_last[1], True)

    y_prompt = x[0].reshape(BATCH, SEQ, D_MODEL)
    y_sample = x[1].reshape(DEC_BATCH, DEC_SEQ, D_MODEL)
    return (y_prompt, y_sample, new_k, new_v)
```

```python
import functools
from typing import NamedTuple

import numpy as np
import jax
import jax.numpy as jnp
from jax import lax
from jax.experimental import pallas as pl
from jax.experimental.pallas import tpu as pltpu

D_MODEL = 1024
BATCH = 32
SEQ = 256
DEPTH = 4
DEC_BATCH = 2
DEC_SEQ = 2048
PAST_LEN = 256
GRID_W = 64
N_MIXERS = 3
CHUNK = 128
D_GMLP = 2 * D_MODEL
N_GROUPS_A = 8
D_GROUP_A = D_GMLP // N_GROUPS_A
POOL_WINDOWS = (2, 4, 8, 16)
N_POOL_GROUPS = 4
D_POOL_GROUP = D_MODEL // N_POOL_GROUPS
N_HEADS = 16
N_C_LAYERS = DEPTH // N_MIXERS
N_KV_HEADS = 4
HEAD_DIM = 64
Q_PER_KV = N_HEADS // N_KV_HEADS
WINDOW = 128
BLOCK = 128
ROPE_THETA = 10000.0
D_FF = 2816
CONV_W = 3
EPS = 1e-6
NEG_INF = -1e30
LOG2E = 1.4426950408889634

F32 = jnp.float32
BF16 = jnp.bfloat16

N_CTX_ROWS = BATCH * SEQ
N_SMP_ROWS = DEC_BATCH * DEC_SEQ
N_ROWS = N_CTX_ROWS + N_SMP_ROWS
N_COND = 1 + DEC_BATCH
N_MOD = 6
HALO = 16
SUBLANES = 8
LANES = 128
TN = 256
Q_DIM = N_HEADS * HEAD_DIM
KV_DIM = N_KV_HEADS * HEAD_DIM
KV_EXT = N_KV_HEADS * 2 * LANES
VMEM_CAP = 56 * 1024 * 1024
VMEM_TEMPS = 2 * 1024 * 1024


class _Rows:
    def __init__(self, tm):
        self.tm = tm
        self.nb = N_ROWS // tm
        self.ncb = N_CTX_ROWS // tm
        self.bps = DEC_SEQ // tm

    def cond_idx(self, i):
        return jnp.where(i < self.ncb, 0, 1 + (i - self.ncb) // self.bps)

    def seq_len(self, i):
        return jnp.where(i < self.ncb, SEQ, DEC_SEQ)

    def row_spec(self, width, offset=0):
        return pl.BlockSpec((self.tm, width), lambda i: (i + offset, 0))

    def ctx_rows_spec(self, width):
        return pl.BlockSpec((self.tm, width), lambda i: (jnp.minimum(i, self.ncb - 1), 0))

    def smp_rows_spec(self, width):
        return pl.BlockSpec((self.tm, width), lambda i: (jnp.maximum(i - self.ncb, 0), 0))

    def halo_specs(self, width, rows, offset=0):
        per = self.tm // rows
        last = N_ROWS // rows - 1
        prev = pl.BlockSpec((rows, width), lambda i: (jnp.maximum((i + offset) * per - 1, 0), 0))
        nxt = pl.BlockSpec((rows, width), lambda i: (jnp.minimum((i + offset + 1) * per, last), 0))
        return prev, nxt


ROWS_WIDE = _Rows(1024)
ROWS_NARROW = _Rows(512)


def _gelu(x):
    return 0.5 * x * (1.0 + jnp.tanh(0.7978845608028654 * (x + 0.044715 * (x * x * x))))


def _norm_mod(x, g, sc, sh):
    ms = jnp.mean(x * x, axis=-1, keepdims=True)
    return x * lax.rsqrt(ms + EPS) * (g * (1.0 + sc)) + sh


def _mod_row(mod_ref, cond, k):
    return mod_ref[pl.ds(cond, 1), k * D_MODEL:(k + 1) * D_MODEL]


def _dot(a, b):
    return jnp.dot(a, b, preferred_element_type=F32)


def _dot_nt(a, b):
    return lax.dot_general(a, b, (((1,), (1,)), ((), ())), preferred_element_type=F32)


def _padded_bytes(shape, dtype):
    item = jnp.dtype(dtype).itemsize
    dims = [1 if d is None else d for d in shape]
    dims[-1] = -(-dims[-1] // LANES) * LANES
    if len(dims) > 1:
        rows = SUBLANES * 4 // item
        dims[-2] = -(-dims[-2] // rows) * rows
    return item * int(np.prod(dims))


def _vmem_limit(specs, dtypes, scratch_shapes):
    total = 0
    for spec, dtype in zip(specs, dtypes):
        if spec.block_shape is not None:
            buffers = 2 if spec.pipeline_mode is None else spec.pipeline_mode.buffer_count
            total += buffers * _padded_bytes(spec.block_shape, dtype)
    total += sum(_padded_bytes(s.shape, s.dtype) for s in scratch_shapes)
    assert total + VMEM_TEMPS <= VMEM_CAP, f"declared VMEM buffers too large: {total} bytes"
    return VMEM_CAP


def _resident(shape, layer=None):
    nd = len(shape)
    if layer is None:
        return pl.BlockSpec(shape, lambda *_: (0,) * nd, pipeline_mode=pl.Buffered(1))
    return pl.BlockSpec((None,) + tuple(shape), lambda *_: (layer,) + (0,) * nd, pipeline_mode=pl.Buffered(1))


def _mod_spec(layer):
    return _resident((SUBLANES, N_MOD * D_MODEL), layer)


class _Side(NamedTuple):
    src: jax.Array
    layer: int
    axis: int
    n: int


def _side_plumbing(sides):
    in_specs, out_specs, out_shape = [], [], []
    for s in sides:
        _, r, c = s.src.shape
        blk = (r // s.n, c) if s.axis == 0 else (r, c // s.n)

        def idx(i, s=s):
            k = jnp.minimum(i, s.n - 1)
            return (k, 0) if s.axis == 0 else (0, k)

        in_specs.append(pl.BlockSpec((None,) + blk, lambda i, s=s, idx=idx: (s.layer,) + idx(i)))
        out_specs.append(pl.BlockSpec(blk, idx))
        out_shape.append(jax.ShapeDtypeStruct((r, c), BF16))
    return in_specs, out_specs, out_shape


def _with_sides(body, n_in, n_out, n_side):
    def kernel(*refs):
        ins, rest = refs[:n_in], refs[n_in:]
        side_in, rest = rest[:n_side], rest[n_side:]
        outs, rest = rest[:n_out], rest[n_out:]
        side_out, scratch = rest[:n_side], rest[n_side:]
        for src, dst in zip(side_in, side_out):
            dst[...] = src[...].astype(BF16)
        body(*ins, *outs, *scratch)
    return kernel


def _hosted_call(body, geo_steps, in_specs, out_specs, out_shape, scratch_shapes, sides, name, aliases=None):
    s_in, s_out, s_shape = _side_plumbing(sides)
    assert all(s.n <= geo_steps for s in sides)
    in_specs, out_specs, out_shape = list(in_specs) + s_in, list(out_specs) + s_out, list(out_shape) + s_shape

    def run(*operands):
        dtypes = [o.dtype for o in operands] + [o.dtype for o in out_shape]
        return pl.pallas_call(
            _with_sides(body, len(in_specs) - len(sides), len(out_specs) - len(sides), len(sides)),
            grid=(geo_steps,),
            in_specs=in_specs,
            out_specs=out_specs,
            out_shape=out_shape,
            input_output_aliases=aliases or {},
            scratch_shapes=scratch_shapes,
            compiler_params=pltpu.CompilerParams(
                dimension_semantics=("arbitrary",),
                vmem_limit_bytes=_vmem_limit(in_specs + out_specs, dtypes, scratch_shapes)),
            name=name,
        )(*operands)

    return run


ADA_TN = 1536


def _ada_kernel(c_ref, w_ref, b_ref, o_ref):
    c = c_ref[...]
    s = (c * (1.0 / (1.0 + jnp.exp(-c)))).astype(BF16)
    o_ref[...] = _dot(s, w_ref[...].astype(BF16)) + b_ref[...]


def _ada_call(cond8, w_ada, b_ada):
    nt = N_MOD * D_MODEL // ADA_TN
    return pl.pallas_call(
        _ada_kernel,
        grid=(DEPTH, nt),
        in_specs=[
            pl.BlockSpec((SUBLANES, D_MODEL), lambda l, j: (0, 0)),
            pl.BlockSpec((None, D_MODEL, ADA_TN), lambda l, j: (l, 0, j)),
            pl.BlockSpec((None, 1, ADA_TN), lambda l, j: (l, 0, j)),
        ],
        out_specs=pl.BlockSpec((None, SUBLANES, ADA_TN), lambda l, j: (l, 0, j)),
        out_shape=jax.ShapeDtypeStruct((DEPTH, SUBLANES, N_MOD * D_MODEL), F32),
        compiler_params=pltpu.CompilerParams(dimension_semantics=("arbitrary", "arbitrary"),
                                             vmem_limit_bytes=2 * _padded_bytes((D_MODEL, ADA_TN), F32) + VMEM_TEMPS),
        name="ada",
    )(cond8, w_ada, b_ada.reshape(DEPTH, 1, N_MOD * D_MODEL))


def _gmlp_kernel(geo, split_in, *refs):
    i = pl.program_id(0)
    tm = geo.tm
    if split_in:
        xc_ref, xs_ref, *refs = refs
        x = jnp.where(i < geo.ncb, xc_ref[...], xs_ref[...])
    else:
        x_ref, *refs = refs
        x = x_ref[...]
    g_ref, mod_ref, win_ref, gv_ref, ws_ref, bs_ref, wout_ref, o_ref, h_ref, u_ref, v_ref, p_ref = refs
    cond = geo.cond_idx(i)
    h_ref[...] = _norm_mod(x, g_ref[...], _mod_row(mod_ref, cond, 1), _mod_row(mod_ref, cond, 0)).astype(BF16)
    ssq = jnp.zeros((tm, LANES), F32)
    for j in range(D_GMLP // TN):
        sl = slice(j * TN, (j + 1) * TN)
        u_ref[:, sl] = _gelu(_dot(h_ref[...], win_ref[:, sl]))
        v = _gelu(_dot(h_ref[...], win_ref[:, D_GMLP + j * TN:D_GMLP + (j + 1) * TN]))
        v_ref[:, sl] = v
        for k in range(TN // LANES):
            vk = v[:, k * LANES:(k + 1) * LANES]
            ssq = ssq + vk * vk
    r = lax.rsqrt(jnp.sum(ssq, axis=-1, keepdims=True) * (1.0 / D_GMLP) + EPS)
    for g in range(N_GROUPS_A):
        sl = slice(g * D_GROUP_A, (g + 1) * D_GROUP_A)
        vn = (v_ref[:, sl] * r * gv_ref[:, sl]).astype(BF16)
        for c in range(tm // CHUNK):
            rows = slice(c * CHUNK, (c + 1) * CHUNK)
            sv = _dot(ws_ref[g], vn[rows, :]) + bs_ref[g]
            p_ref[rows, sl] = (u_ref[rows, sl] * sv).astype(BF16)
    y = _dot(p_ref[...], wout_ref[...])
    o_ref[...] = x + _mod_row(mod_ref, cond, 2) * y


def _gmlp_call(x, layer, j, mods, g_mix, w_in, g_v, w_s, b_s, w_out, sides=()):
    split_in = isinstance(x, tuple)
    geo = ROWS_NARROW if split_in else ROWS_WIDE
    if split_in:
        x_specs = [geo.ctx_rows_spec(D_MODEL), geo.smp_rows_spec(D_MODEL)]
        xs = x
    else:
        x_specs = [geo.row_spec(D_MODEL)]
        xs = (x,)
    return _hosted_call(
        functools.partial(_gmlp_kernel, geo, split_in), geo.nb,
        in_specs=x_specs + [
            _resident((1, D_MODEL), layer),
            _mod_spec(layer),
            _resident((D_MODEL, 2 * D_GMLP)),
            _resident((1, D_GMLP), j),
            _resident((N_GROUPS_A, CHUNK, CHUNK), j),
            _resident((N_GROUPS_A, CHUNK, 1), j),
            _resident((D_GMLP, D_MODEL)),
        ],
        out_specs=[geo.row_spec(D_MODEL)],
        out_shape=[jax.ShapeDtypeStruct((N_ROWS, D_MODEL), F32)],
        scratch_shapes=[
            pltpu.VMEM((geo.tm, D_MODEL), BF16),
            pltpu.VMEM((geo.tm, D_GMLP), F32),
            pltpu.VMEM((geo.tm, D_GMLP), F32),
            pltpu.VMEM((geo.tm, D_GMLP), BF16),
        ],
        sides=sides, name="gmlp",
    )(*xs, g_mix, mods, w_in, g_v, w_s, b_s, w_out, *[s.src for s in sides])


def _pool_kernel(geo, x_ref, xp_ref, xn_ref, g_ref, mod_ref, pw_ref, pb_ref, ps_ref, o_ref, hext_ref):
    i = pl.program_id(0)
    tm = geo.tm
    ext = tm + 2 * HALO
    cond = geo.cond_idx(i)
    g = g_ref[...]
    sc = _mod_row(mod_ref, cond, 1)
    sh = _mod_row(mod_ref, cond, 0)
    gate = _mod_row(mod_ref, cond, 2)
    x = x_ref[...]
    hext_ref[0:tm, :] = _norm_mod(x, g, sc, sh)
    hext_ref[tm:tm + HALO, :] = _norm_mod(xn_ref[...], g, sc, sh)
    hext_ref[tm + HALO:ext, :] = _norm_mod(xp_ref[...], g, sc, sh)
    seq = geo.seq_len(i)
    e = lax.broadcasted_iota(jnp.int32, (ext, D_POOL_GROUP), 0)
    pos = (i * tm + jnp.where(e < tm + HALO, e, e - ext)) & (seq - 1)
    p = (i * tm + lax.broadcasted_iota(jnp.int32, (tm, D_POOL_GROUP), 0)) & (seq - 1)
    for grp in range(N_POOL_GROUPS):
        sl = slice(grp * D_POOL_GROUP, (grp + 1) * D_POOL_GROUP)
        half = POOL_WINDOWS[grp] // 2
        he = hext_ref[:, sl]
        fwd = he
        bwd = jnp.where(pos >= 1, pltpu.roll(he, 1, 0), 0.0)
        span = 1
        while span < half:
            fwd = fwd + jnp.where(pos < seq - span, pltpu.roll(fwd, ext - span, 0), 0.0)
            bwd = bwd + jnp.where(pos >= span, pltpu.roll(bwd, span, 0), 0.0)
            span *= 2
        cnt = (jnp.minimum(p + half - 1, seq - 1) - jnp.maximum(p - half, 0) + 1).astype(F32)
        d = (fwd[:tm, :] + bwd[:tm, :]) / cnt - he[:tm, :]
        y = (_dot(d.astype(BF16), pw_ref[grp]) + pb_ref[:, sl]) * ps_ref[:, sl]
        o_ref[:, sl] = x[:, sl] + gate[:, sl] * y


def _pool_call(x, layer, j, mods, g_mix, p_w, p_b, p_scale, sides=()):
    geo = ROWS_WIDE
    prev, nxt = geo.halo_specs(D_MODEL, HALO)
    return _hosted_call(
        functools.partial(_pool_kernel, geo), geo.nb,
        in_specs=[
            geo.row_spec(D_MODEL), prev, nxt,
            _resident((1, D_MODEL), layer),
            _mod_spec(layer),
            _resident((N_POOL_GROUPS, D_POOL_GROUP, D_POOL_GROUP), j),
            _resident((1, D_MODEL), j),
            _resident((1, D_MODEL), j),
        ],
        out_specs=[geo.row_spec(D_MODEL)],
        out_shape=[jax.ShapeDtypeStruct((N_ROWS, D_MODEL), F32)],
        scratch_shapes=[pltpu.VMEM((geo.tm + 2 * HALO, D_MODEL), F32)],
        sides=sides, name="pool",
    )(x, x, x, g_mix, mods, p_w, p_b, p_scale, *[s.src for s in sides])


def _rope_tables(tm):
    n_freq = HEAD_DIM // 4
    inv = ROPE_THETA ** (-np.arange(n_freq, dtype=np.float64) / n_freq)
    t = np.arange(DEC_SEQ)
    lane = np.arange(LANES) % HEAD_DIM
    quarter, f = lane // n_freq, lane % n_freq
    posn = np.where(quarter[None, :] < 2, (t // GRID_W)[:, None], (t % GRID_W)[:, None])
    ang = posn * inv[f][None, :]
    cos, sin = np.cos(ang), np.sin(ang)
    odd = (quarter % 2 == 1)[None, :]
    tab = np.concatenate([cos, np.where(odd, sin, -sin)], axis=1)
    ident = np.concatenate([np.ones((tm, LANES)), np.zeros((tm, LANES))], axis=1)
    return np.concatenate([ident, tab], axis=0).astype(np.float32)


def _head_meansq(t, bd_ref):
    sq = t * t
    hi = sq.astype(BF16)
    lo = (sq - hi.astype(F32)).astype(BF16)
    return _dot(hi, bd_ref[...]) + _dot(lo, bd_ref[...])


def _rope(t, tab_ref):
    q = HEAD_DIM // 4
    second = (lax.broadcasted_iota(jnp.int32, (1, LANES), 1) & q) != 0
    partner = jnp.where(second, pltpu.roll(t, q, 1), pltpu.roll(t, LANES - q, 1))
    return t * tab_ref[:, 0:LANES] + partner * tab_ref[:, LANES:2 * LANES]


def _kv_ext(t, o_ref):
    lo = lax.broadcasted_iota(jnp.int32, (t.shape[0], LANES), 1) < HEAD_DIM
    for m in range(KV_DIM // LANES):
        a = t[:, m * LANES:(m + 1) * LANES]
        r = pltpu.roll(a, HEAD_DIM, 1)
        base = m * 4 * LANES
        o_ref[:, base:base + LANES] = jnp.where(lo, a, 0.0).astype(o_ref.dtype)
        o_ref[:, base + LANES:base + 2 * LANES] = jnp.where(lo, 0.0, r).astype(o_ref.dtype)
        o_ref[:, base + 2 * LANES:base + 3 * LANES] = jnp.where(lo, r, 0.0).astype(o_ref.dtype)
        o_ref[:, base + 3 * LANES:base + 4 * LANES] = jnp.where(lo, 0.0, a).astype(o_ref.dtype)


def _store_head_rows(o_ref, t):
    for h in range(N_KV_HEADS):
        o_ref[pl.ds(h, t.shape[0], stride=N_KV_HEADS), :] = t[:, h * HEAD_DIM:(h + 1) * HEAD_DIM]


def _qkv_kernel(geo, x_ref, g_ref, mod_ref, w_ref, gq_ref, gk_ref, bd_ref, tab_ref,
                q_ref, kx_ref, vx_ref, kn_ref, v_ref, h_ref, kr_ref, kn_s, v_s):
    i = pl.program_id(0)
    cond = geo.cond_idx(i)
    h_ref[...] = _norm_mod(x_ref[...], g_ref[...], _mod_row(mod_ref, cond, 1),
                           _mod_row(mod_ref, cond, 0)).astype(BF16)
    scale = HEAD_DIM ** -0.5 * LOG2E
    n_q = Q_DIM // TN
    ts = [_dot(h_ref[...], w_ref[:, c * TN:(c + 1) * TN]) for c in range(n_q + 2)]
    for c in range(n_q):
        t = ts[c]
        t = t * lax.rsqrt(_head_meansq(t, bd_ref) + EPS) * (gq_ref[...] * scale)
        for m in range(TN // LANES):
            q_ref[:, c * TN + m * LANES:c * TN + (m + 1) * LANES] = _rope(
                t[:, m * LANES:(m + 1) * LANES], tab_ref).astype(BF16)
    t = ts[n_q]
    t = t * lax.rsqrt(_head_meansq(t, bd_ref) + EPS) * gk_ref[...]
    kn_s[...] = t
    for m in range(KV_DIM // LANES):
        kr_ref[:, m * LANES:(m + 1) * LANES] = _rope(t[:, m * LANES:(m + 1) * LANES], tab_ref)
    _kv_ext(kr_ref[...], kx_ref)
    v = ts[n_q + 1]
    v_s[...] = v
    _kv_ext(v, vx_ref)

    @pl.when(i < geo.ncb)
    def _():
        _store_head_rows(kn_ref, kn_s[...])
        _store_head_rows(v_ref, v_s[...])


def _qkv_call(x, layer, j, mods, g_mix, w_qkv, g_q, g_k, bd):
    geo = ROWS_WIDE
    tab = jnp.asarray(_rope_tables(geo.tm))
    tab_spec = pl.BlockSpec((geo.tm, 2 * LANES),
                            lambda i: (jnp.where(i < geo.ncb, 0, 1 + (i - geo.ncb) % geo.bps), 0))
    cache_spec = pl.BlockSpec((geo.tm * N_KV_HEADS, HEAD_DIM), lambda i: (jnp.minimum(i, geo.ncb - 1), 0))
    return _hosted_call(
        functools.partial(_qkv_kernel, geo), geo.nb,
        in_specs=[
            geo.row_spec(D_MODEL),
            _resident((1, D_MODEL), layer),
            _mod_spec(layer),
            _resident((D_MODEL, Q_DIM + 2 * KV_DIM)),
            _resident((1, TN), j),
            _resident((1, TN), j),
            _resident((TN, TN)),
            tab_spec,
        ],
        out_specs=[geo.row_spec(Q_DIM), geo.row_spec(KV_EXT), geo.row_spec(KV_EXT), cache_spec, cache_spec],
        out_shape=[
            jax.ShapeDtypeStruct((N_ROWS, Q_DIM), BF16),
            jax.ShapeDtypeStruct((N_ROWS, KV_EXT), BF16),
            jax.ShapeDtypeStruct((N_ROWS, KV_EXT), BF16),
            jax.ShapeDtypeStruct((N_CTX_ROWS * N_KV_HEADS, HEAD_DIM), F32),
            jax.ShapeDtypeStruct((N_CTX_ROWS * N_KV_HEADS, HEAD_DIM), F32),
        ],
        scratch_shapes=[pltpu.VMEM((geo.tm, D_MODEL), BF16)] + [pltpu.VMEM((geo.tm, KV_DIM), F32)] * 3,
        sides=(), name="qkv",
    )(x, g_mix, mods, w_qkv, g_q, g_k, bd, tab)


def _head_lanes(g, j):
    base = (g * 2 + j) * LANES
    return slice(base, base + LANES)


def _ctx_attn_kernel(geo, j, sink_ref, x_ref, mod_ref, q_ref, kx_ref, vx_ref, wo_ref, o_ref, a_ref):
    for s in range(geo.tm // SEQ):
        rows = slice(s * SEQ, (s + 1) * SEQ)
        for g in range(N_KV_HEADS):
            for pr in range(2):
                qp = q_ref[rows, _head_lanes(g, pr)]
                acc = jnp.zeros((SEQ, LANES), F32)
                for var in range(2):
                    sink = sink_ref[j * N_HEADS + g * Q_PER_KV + 2 * pr + var] * LOG2E
                    sc = _dot_nt(qp, kx_ref[rows, _head_lanes(g, var)])
                    m = jnp.maximum(jnp.max(sc, axis=-1, keepdims=True), sink)
                    e = jnp.exp2(sc - m)
                    den = jnp.sum(e, axis=-1, keepdims=True) + jnp.exp2(sink - m)
                    acc = acc + _dot(e.astype(BF16), vx_ref[rows, _head_lanes(g, var)]) / den
                a_ref[rows, _head_lanes(g, pr)] = acc.astype(BF16)
    y = _dot(a_ref[...], wo_ref[...])
    o_ref[...] = x_ref[...] + _mod_row(mod_ref, 0, 2) * y


def _ctx_attn_call(x, layer, j, mods, sink, q, kx, vx, w_o, sides=()):
    geo = ROWS_WIDE
    return _hosted_call(
        functools.partial(_ctx_attn_kernel, geo, j), geo.ncb,
        in_specs=[
            pl.BlockSpec(memory_space=pltpu.SMEM),
            geo.row_spec(D_MODEL),
            _mod_spec(layer),
            geo.row_spec(Q_DIM), geo.row_spec(KV_EXT), geo.row_spec(KV_EXT),
            _resident((Q_DIM, D_MODEL)),
        ],
        out_specs=[geo.row_spec(D_MODEL)],
        out_shape=[jax.ShapeDtypeStruct((N_ROWS, D_MODEL), F32)],
        scratch_shapes=[pltpu.VMEM((geo.tm, Q_DIM), BF16)],
        sides=sides, name="ctx_attn", aliases={1: 0},
    )(sink, x, mods, q, kx, vx, w_o, *[s.src for s in sides])


def _group_queries(q_ref, rows, g):
    return jnp.concatenate([q_ref[rows, _head_lanes(g, 0)], q_ref[rows, _head_lanes(g, 1)]], axis=0)


def _group_sinks(sink_ref, j, g, n):
    heads = (0, 2, 1, 3)
    return jnp.concatenate(
        [jnp.full((n, 1), sink_ref[j * N_HEADS + g * Q_PER_KV + h] * LOG2E, F32) for h in heads], axis=0)


def _lane_tiles(pieces):
    return [p[:, k * LANES:(k + 1) * LANES] for p in pieces for k in range(p.shape[1] // LANES)]


def _sink_softmax(pieces, sink_col):
    m = jnp.max(functools.reduce(jnp.maximum, _lane_tiles(pieces)), axis=-1, keepdims=True)
    m = jnp.maximum(m, sink_col)
    es = [jnp.exp2(s - m) for s in pieces]
    den = jnp.sum(functools.reduce(jnp.add, _lane_tiles(es)), axis=-1, keepdims=True) + jnp.exp2(sink_col - m)
    return es, 1.0 / den


def _store_group(a_ref, rows, g, pv, n):
    for pr in range(2):
        sub = slice(pr * n, (pr + 1) * n)
        a_ref[rows, _head_lanes(g, pr)] = (pv[0][sub, :] + pv[1][sub, :]).astype(BF16)


def _smp_attn_kernel(geo, j, sink_ref, x_ref, mod_ref, q_ref, kc_ref, kp_ref, kn_ref, vc_ref, vp_ref, vn_ref,
                     ck_ref, cv_ref, wo_ref, o_ref, a_ref, kall_ref, vall_ref):
    i = pl.program_id(0)
    tm = geo.tm
    n_qb = tm // BLOCK
    kwin = tm + 2 * BLOCK
    kall_ref[0:BLOCK, :] = kp_ref[...]
    kall_ref[BLOCK:BLOCK + tm, :] = kc_ref[...]
    kall_ref[BLOCK + tm:kwin, :] = kn_ref[...]
    vall_ref[0:BLOCK, :] = vp_ref[...]
    vall_ref[BLOCK:BLOCK + tm, :] = vc_ref[...]
    vall_ref[BLOCK + tm:kwin, :] = vn_ref[...]
    first = (i % geo.bps) == 0
    last = (i % geo.bps) == geo.bps - 1
    qi = lax.broadcasted_iota(jnp.int32, (Q_PER_KV * BLOCK, BLOCK), 0) & (BLOCK - 1)
    kj = lax.broadcasted_iota(jnp.int32, (Q_PER_KV * BLOCK, BLOCK), 1)
    def scores(b, g):
        rows = slice(b * BLOCK, (b + 1) * BLOCK)
        win = slice(b * BLOCK, (b + 3) * BLOCK)
        q2 = _group_queries(q_ref, rows, g)
        sw = jnp.concatenate([_dot_nt(q2, kall_ref[win, _head_lanes(g, var)]) for var in range(2)], axis=0)
        sc = jnp.concatenate([_dot_nt(q2, ck_ref[:, _head_lanes(g, var)]) for var in range(2)], axis=0)
        return sw, sc

    def probabilities(b, g, sw, sc):
        ok_prev = kj >= (qi + jnp.where(first, BLOCK, 0) if b == 0 else qi)
        ok_next = kj <= (qi - jnp.where(last, BLOCK, 0) if b == n_qb - 1 else qi)
        pieces = [jnp.where(ok_prev, sw[:, 0:BLOCK], NEG_INF), sw[:, BLOCK:2 * BLOCK],
                  jnp.where(ok_next, sw[:, 2 * BLOCK:3 * BLOCK], NEG_INF), sc]
        es, rinv = _sink_softmax(pieces, _group_sinks(sink_ref, j, g, BLOCK))
        return jnp.concatenate(es[:3], axis=-1).astype(BF16), es[3].astype(BF16), rinv

    def weighted_values(b, g, ew, ec, rinv):
        rows = slice(b * BLOCK, (b + 1) * BLOCK)
        win = slice(b * BLOCK, (b + 3) * BLOCK)
        pv = []
        for var in range(2):
            sub = slice(var * 2 * BLOCK, (var + 1) * 2 * BLOCK)
            lanes = _head_lanes(g, var)
            pv.append((_dot(ew[sub, :], vall_ref[win, lanes]) + _dot(ec[sub, :], cv_ref[:, lanes]))
                      * rinv[sub, :])
        _store_group(a_ref, rows, g, pv, BLOCK)

    todo = [(b, g) for b in range(n_qb) for g in range(N_KV_HEADS)]
    ahead = scores(*todo[0])
    behind = None
    for k, (b, g) in enumerate(todo):
        sw, sc = ahead
        if k + 1 < len(todo):
            ahead = scores(*todo[k + 1])
        probs = probabilities(b, g, sw, sc)
        if behind is not None:
            weighted_values(*behind)
        behind = (b, g) + probs
    weighted_values(*behind)
    y = _dot(a_ref[...], wo_ref[...])
    o_ref[...] = x_ref[...] + _mod_row(mod_ref, 1 + i // geo.bps, 2) * y


def _smp_attn_call(x, layer, j, mods, sink, q, kx, vx, ckx, cvx, w_o, sides=()):
    geo = ROWS_NARROW
    prev, nxt = geo.halo_specs(KV_EXT, BLOCK, offset=geo.ncb)
    ctx_spec = pl.BlockSpec((None, None, PAST_LEN, KV_EXT), lambda i: (i // geo.bps, j, 0, 0))
    kwin = geo.tm + 2 * BLOCK
    return _hosted_call(
        functools.partial(_smp_attn_kernel, geo, j), geo.nb - geo.ncb,
        in_specs=[
            pl.BlockSpec(memory_space=pltpu.SMEM),
            geo.row_spec(D_MODEL, geo.ncb),
            _mod_spec(layer),
            geo.row_spec(Q_DIM, geo.ncb),
            geo.row_spec(KV_EXT, geo.ncb), prev, nxt,
            geo.row_spec(KV_EXT, geo.ncb), prev, nxt,
            ctx_spec, ctx_spec,
            _resident((Q_DIM, D_MODEL)),
        ],
        out_specs=[geo.row_spec(D_MODEL, geo.ncb)],
        out_shape=[jax.ShapeDtypeStruct((N_ROWS, D_MODEL), F32)],
        scratch_shapes=[pltpu.VMEM((geo.tm, Q_DIM), BF16), pltpu.VMEM((kwin, KV_EXT), BF16),
                        pltpu.VMEM((kwin, KV_EXT), BF16)],
        sides=sides, name="smp_attn", aliases={1: 0},
    )(sink, x, mods, q, kx, kx, kx, vx, vx, vx, ckx, cvx, w_o, *[s.src for s in sides])


def _ext_heads(t):
    lead = t.shape[:-1]
    h = t.reshape(lead + (N_KV_HEADS, 1, HEAD_DIM))
    z = jnp.zeros_like(h)
    e = jnp.concatenate([jnp.concatenate([h, z], axis=-1), jnp.concatenate([z, h], axis=-1)], axis=-2)
    return e.reshape(lead + (KV_EXT,))


N_FT = D_FF // TN


def _ffn_kernel(geo, split_out, x_ref, xp_ref, xn_ref, g_ref, mod_ref, win_ref, cw_ref, cb_ref, wo_ref, *refs):
    i = pl.program_id(0)
    tm = geo.tm
    ext = tm + 2 * HALO
    hext_ref, gbuf_ref = refs[-2:]
    cond = geo.cond_idx(i)
    g = g_ref[...]
    sh = _mod_row(mod_ref, cond, 3)
    sc = _mod_row(mod_ref, cond, 4)
    x = x_ref[...]
    hext_ref[0:tm, :] = _norm_mod(x, g, sc, sh).astype(BF16)
    hext_ref[tm:tm + HALO, :] = _norm_mod(xn_ref[...], g, sc, sh).astype(BF16)
    hext_ref[tm + HALO:ext, :] = _norm_mod(xp_ref[...], g, sc, sh).astype(BF16)
    seq = geo.seq_len(i)
    pos = (i * tm + lax.broadcasted_iota(jnp.int32, (tm, TN), 0)) & (seq - 1)
    has_prev = pos != 0
    has_next = pos != seq - 1
    for j in range(N_FT):
        sl = slice(j * TN, (j + 1) * TN)
        a = _dot(hext_ref[...], win_ref[:, sl])
        u = _dot(hext_ref[0:tm, :], win_ref[:, D_FF + j * TN:D_FF + (j + 1) * TN])
        a_prev = jnp.where(has_prev, pltpu.roll(a, 1, 0)[:tm, :], 0.0)
        a_next = jnp.where(has_next, pltpu.roll(a, ext - 1, 0)[:tm, :], 0.0)
        a = a_prev * cw_ref[0:1, sl] + a[:tm, :] * cw_ref[1:2, sl] + a_next * cw_ref[2:3, sl] + cb_ref[:, sl]
        gbuf_ref[:, sl] = (_gelu(a) * u).astype(BF16)
    y = _dot(gbuf_ref[...], wo_ref[...])
    res = x + _mod_row(mod_ref, cond, 5) * y
    if split_out:
        oc_ref, os_ref = refs[:2]

        @pl.when(i < geo.ncb)
        def _():
            oc_ref[...] = res

        @pl.when(i >= geo.ncb)
        def _():
            os_ref[...] = res
    else:
        refs[0][...] = res


def _ffn_call(x, layer, mods, g_ffn, w_in, conv_w, conv_b, w_out, split_out, sides=()):
    geo = ROWS_WIDE
    prev, nxt = geo.halo_specs(D_MODEL, HALO)
    if split_out:
        out_specs = [geo.ctx_rows_spec(D_MODEL), geo.smp_rows_spec(D_MODEL)]
        out_shape = [jax.ShapeDtypeStruct((N_CTX_ROWS, D_MODEL), F32),
                     jax.ShapeDtypeStruct((N_SMP_ROWS, D_MODEL), F32)]
    else:
        out_specs = [geo.row_spec(D_MODEL)]
        out_shape = [jax.ShapeDtypeStruct((N_ROWS, D_MODEL), F32)]
    return _hosted_call(
        functools.partial(_ffn_kernel, geo, split_out), geo.nb,
        in_specs=[
            geo.row_spec(D_MODEL), prev, nxt,
            _resident((1, D_MODEL), layer),
            _mod_spec(layer),
            _resident((D_MODEL, 2 * D_FF)),
            _resident((CONV_W, D_FF), layer),
            _resident((1, D_FF), layer),
            _resident((D_FF, D_MODEL)),
        ],
        out_specs=out_specs,
        out_shape=out_shape,
        scratch_shapes=[pltpu.VMEM((geo.tm + 2 * HALO, D_MODEL), BF16), pltpu.VMEM((geo.tm, D_FF), BF16)],
        sides=sides, name="ffn",
    )(x, x, x, g_ffn, mods, w_in, conv_w, conv_b, w_out, *[s.src for s in sides])


def kernel(x_prompt, x_sample, cache_k, cache_v, c, c_ctx, w_ada, b_ada, g_mix, g_ffn, w_ffn_in, ffn_conv_w,
           ffn_conv_b, w_ffn_out, a_w_in, a_g_v, a_w_s, a_b_s, a_w_out, p_w, p_b, p_scale, c_w_qkv, c_g_q,
           c_g_k, c_sink, c_w_o):
    x = (x_prompt.reshape(N_CTX_ROWS, D_MODEL), x_sample.reshape(N_SMP_ROWS, D_MODEL))

    cond8 = jnp.concatenate([c_ctx[None, :], c, jnp.zeros((SUBLANES - N_COND, D_MODEL), F32)], axis=0)
    mods = _ada_call(cond8, w_ada, b_ada)

    g_mix, g_ffn = g_mix[:, None, :], g_ffn[:, None, :]
    ffn_conv_b = ffn_conv_b[:, None, :]
    a_w_s, a_g_v, a_b_s = a_w_s.astype(BF16), a_g_v[:, None, :], a_b_s[:, :, :, None]
    p_w, p_b, p_scale = p_w.astype(BF16), p_b[:, None, :], p_scale[:, None, :]
    gq = jnp.tile(c_g_q, (1, TN // HEAD_DIM))[:, None, :]
    gk = jnp.tile(c_g_k, (1, TN // HEAD_DIM))[:, None, :]
    bd = jnp.asarray(np.kron(np.eye(TN // HEAD_DIM), np.full((HEAD_DIM, HEAD_DIM), 1.0 / HEAD_DIM)), dtype=BF16)
    sink = c_sink.reshape(N_C_LAYERS * N_HEADS)
    ckx = _ext_heads(cache_k.reshape(DEC_BATCH, N_C_LAYERS, PAST_LEN, KV_DIM)).astype(BF16)
    cvx = _ext_heads(cache_v.reshape(DEC_BATCH, N_C_LAYERS, PAST_LEN, KV_DIM)).astype(BF16)

    def ffn_sides(layer, n_in, axis_in, n_out):
        return (_Side(w_ffn_in, layer, axis_in, n_in), _Side(w_ffn_out, layer, 0, n_out))

    assert DEPTH == 4 and N_MIXERS == 3, "the hosting plan below is written for gMLP, pool, attention, gMLP"
    gw = (a_w_in[0].astype(BF16), a_w_out[0].astype(BF16))
    x, *fw = _gmlp_call(x, 0, 0, mods, g_mix, gw[0], a_g_v, a_w_s, a_b_s, gw[1],
                        sides=ffn_sides(0, 2 * N_FT, 1, 2 * N_FT))
    x, *fw = _ffn_call(x, 0, mods, g_ffn, fw[0], ffn_conv_w, ffn_conv_b, fw[1], False,
                       sides=ffn_sides(1, N_FT, 1, N_FT))
    x, w_qkv, w_o, *fw_last = _pool_call(
        x, 1, 0, mods, g_mix, p_w, p_b, p_scale,
        sides=(_Side(c_w_qkv, 0, 1, 12), _Side(c_w_o, 0, 0, 8)) + ffn_sides(3, N_FT, 1, N_FT))
    x, *fw = _ffn_call(x, 1, mods, g_ffn, fw[0], ffn_conv_w, ffn_conv_b, fw[1], False,
                       sides=ffn_sides(2, N_FT, 1, N_FT))
    q, kx, vx, kn, v = _qkv_call(x, 2, 0, mods, g_mix, w_qkv, gq, gk, bd)
    new_k = kn.reshape(BATCH, 1, SEQ, N_KV_HEADS, HEAD_DIM)
    new_v = v.reshape(BATCH, 1, SEQ, N_KV_HEADS, HEAD_DIM)
    (x,) = _ctx_attn_call(x, 2, 0, mods, sink, q, kx, vx, w_o)
    (x,) = _smp_attn_call(x, 2, 0, mods, sink, q, kx, vx, ckx, cvx, w_o)
    x, *gw = _ffn_call(x, 2, mods, g_ffn, fw[0], ffn_conv_w, ffn_conv_b, fw[1], False,
                       sides=(_Side(a_w_in, 1, 1, 8), _Side(a_w_out, 1, 0, 8)))
    (x,) = _gmlp_call(x, 3, 1, mods, g_mix, gw[0], a_g_v, a_w_s, a_b_s, gw[1])
    x = _ffn_call(x, 3, mods, g_ffn, fw_last[0], ffn_conv_w, ffn_conv_b, fw_last[1], True)

    y_prompt = x[0].reshape(BATCH, SEQ, D_MODEL)
    y_sample = x[1].reshape(DEC_BATCH, DEC_SEQ, D_MODEL)
    return (y_prompt, y_sample, new_k, new_v)
```

```python
import functools
from typing import NamedTuple

import numpy as np
import jax
import jax.numpy as jnp
from jax import lax
from jax.experimental import pallas as pl
from jax.experimental.pallas import tpu as pltpu

D_MODEL = 1024
BATCH = 32
SEQ = 256
DEPTH = 4
DEC_BATCH = 2
DEC_SEQ = 2048
PAST_LEN = 256
GRID_W = 64
N_MIXERS = 3
CHUNK = 128
D_GMLP = 2 * D_MODEL
N_GROUPS_A = 8
D_GROUP_A = D_GMLP // N_GROUPS_A
POOL_WINDOWS = (2, 4, 8, 16)
N_POOL_GROUPS = 4
D_POOL_GROUP = D_MODEL // N_POOL_GROUPS
N_HEADS = 16
N_C_LAYERS = DEPTH // N_MIXERS
N_KV_HEADS = 4
HEAD_DIM = 64
Q_PER_KV = N_HEADS // N_KV_HEADS
WINDOW = 128
BLOCK = 128
ROPE_THETA = 10000.0
D_FF = 2816
CONV_W = 3
EPS = 1e-6
NEG_INF = -1e30
LOG2E = 1.4426950408889634

F32 = jnp.float32
BF16 = jnp.bfloat16

N_CTX_ROWS = BATCH * SEQ
N_SMP_ROWS = DEC_BATCH * DEC_SEQ
N_ROWS = N_CTX_ROWS + N_SMP_ROWS
N_COND = 1 + DEC_BATCH
N_MOD = 6
HALO = 16
SUBLANES = 8
LANES = 128
TN = 256
QKV_ROWS = 256
Q_DIM = N_HEADS * HEAD_DIM
KV_DIM = N_KV_HEADS * HEAD_DIM
KV_EXT = N_KV_HEADS * 2 * LANES
VMEM_CAP = 56 * 1024 * 1024
VMEM_TEMPS = 2 * 1024 * 1024


class _Rows:
    def __init__(self, tm):
        self.tm = tm
        self.nb = N_ROWS // tm
        self.ncb = N_CTX_ROWS // tm
        self.bps = DEC_SEQ // tm

    def cond_idx(self, i):
        return jnp.where(i < self.ncb, 0, 1 + (i - self.ncb) // self.bps)

    def seq_len(self, i):
        return jnp.where(i < self.ncb, SEQ, DEC_SEQ)

    def row_spec(self, width, offset=0):
        return pl.BlockSpec((self.tm, width), lambda i: (i + offset, 0))

    def ctx_rows_spec(self, width):
        return pl.BlockSpec((self.tm, width), lambda i: (jnp.minimum(i, self.ncb - 1), 0))

    def smp_rows_spec(self, width):
        return pl.BlockSpec((self.tm, width), lambda i: (jnp.maximum(i - self.ncb, 0), 0))

    def halo_specs(self, width, rows, offset=0):
        per = self.tm // rows
        last = N_ROWS // rows - 1
        prev = pl.BlockSpec((rows, width), lambda i: (jnp.maximum((i + offset) * per - 1, 0), 0))
        nxt = pl.BlockSpec((rows, width), lambda i: (jnp.minimum((i + offset + 1) * per, last), 0))
        return prev, nxt


ROWS_WIDE = _Rows(1024)
ROWS_NARROW = _Rows(512)


def _gelu(x):
    return 0.5 * x * (1.0 + jnp.tanh(0.7978845608028654 * (x + 0.044715 * (x * x * x))))


def _norm_mod(x, g, sc, sh):
    ms = jnp.mean(x * x, axis=-1, keepdims=True)
    return x * lax.rsqrt(ms + EPS) * (g * (1.0 + sc)) + sh


def _mod_row(mod_ref, cond, k):
    return mod_ref[pl.ds(cond, 1), k * D_MODEL:(k + 1) * D_MODEL]


def _dot(a, b):
    return jnp.dot(a, b, preferred_element_type=F32)


def _dot_nt(a, b):
    return lax.dot_general(a, b, (((1,), (1,)), ((), ())), preferred_element_type=F32)


def _padded_bytes(shape, dtype):
    item = jnp.dtype(dtype).itemsize
    dims = [1 if d is None else d for d in shape]
    dims[-1] = -(-dims[-1] // LANES) * LANES
    if len(dims) > 1:
        rows = SUBLANES * 4 // item
        dims[-2] = -(-dims[-2] // rows) * rows
    return item * int(np.prod(dims))


def _vmem_limit(specs, dtypes, scratch_shapes):
    total = 0
    for spec, dtype in zip(specs, dtypes):
        if spec.block_shape is not None:
            buffers = 2 if spec.pipeline_mode is None else spec.pipeline_mode.buffer_count
            total += buffers * _padded_bytes(spec.block_shape, dtype)
    total += sum(_padded_bytes(s.shape, s.dtype) for s in scratch_shapes)
    assert total + VMEM_TEMPS <= VMEM_CAP, f"declared VMEM buffers too large: {total} bytes"
    return VMEM_CAP


def _resident(shape, layer=None):
    nd = len(shape)
    if layer is None:
        return pl.BlockSpec(shape, lambda *_: (0,) * nd, pipeline_mode=pl.Buffered(1))
    return pl.BlockSpec((None,) + tuple(shape), lambda *_: (layer,) + (0,) * nd, pipeline_mode=pl.Buffered(1))


def _mod_spec(layer):
    return _resident((SUBLANES, N_MOD * D_MODEL), layer)


class _Side(NamedTuple):
    src: jax.Array
    layer: int
    axis: int
    n: int


def _side_plumbing(sides):
    in_specs, out_specs, out_shape = [], [], []
    for s in sides:
        _, r, c = s.src.shape
        blk = (r // s.n, c) if s.axis == 0 else (r, c // s.n)

        def idx(i, s=s):
            k = jnp.minimum(i, s.n - 1)
            return (k, 0) if s.axis == 0 else (0, k)

        in_specs.append(pl.BlockSpec((None,) + blk, lambda i, s=s, idx=idx: (s.layer,) + idx(i)))
        out_specs.append(pl.BlockSpec(blk, idx))
        out_shape.append(jax.ShapeDtypeStruct((r, c), BF16))
    return in_specs, out_specs, out_shape


def _with_sides(body, n_in, n_out, n_side):
    def kernel(*refs):
        ins, rest = refs[:n_in], refs[n_in:]
        side_in, rest = rest[:n_side], rest[n_side:]
        outs, rest = rest[:n_out], rest[n_out:]
        side_out, scratch = rest[:n_side], rest[n_side:]
        for src, dst in zip(side_in, side_out):
            dst[...] = src[...].astype(BF16)
        body(*ins, *outs, *scratch)
    return kernel


def _hosted_call(body, geo_steps, in_specs, out_specs, out_shape, scratch_shapes, sides, name, aliases=None):
    s_in, s_out, s_shape = _side_plumbing(sides)
    assert all(s.n <= geo_steps for s in sides)
    in_specs, out_specs, out_shape = list(in_specs) + s_in, list(out_specs) + s_out, list(out_shape) + s_shape

    def run(*operands):
        dtypes = [o.dtype for o in operands] + [o.dtype for o in out_shape]
        return pl.pallas_call(
            _with_sides(body, len(in_specs) - len(sides), len(out_specs) - len(sides), len(sides)),
            grid=(geo_steps,),
            in_specs=in_specs,
            out_specs=out_specs,
            out_shape=out_shape,
            input_output_aliases=aliases or {},
            scratch_shapes=scratch_shapes,
            compiler_params=pltpu.CompilerParams(
                dimension_semantics=("arbitrary",),
                vmem_limit_bytes=_vmem_limit(in_specs + out_specs, dtypes, scratch_shapes)),
            name=name,
        )(*operands)

    return run


ADA_TN = 1536


def _ada_kernel(c_ref, w_ref, b_ref, o_ref):
    c = c_ref[...]
    s = (c * (1.0 / (1.0 + jnp.exp(-c)))).astype(BF16)
    o_ref[...] = _dot(s, w_ref[...].astype(BF16)) + b_ref[...]


def _ada_call(cond8, w_ada, b_ada):
    nt = N_MOD * D_MODEL // ADA_TN
    return pl.pallas_call(
        _ada_kernel,
        grid=(DEPTH, nt),
        in_specs=[
            pl.BlockSpec((SUBLANES, D_MODEL), lambda l, j: (0, 0)),
            pl.BlockSpec((None, D_MODEL, ADA_TN), lambda l, j: (l, 0, j)),
            pl.BlockSpec((None, 1, ADA_TN), lambda l, j: (l, 0, j)),
        ],
        out_specs=pl.BlockSpec((None, SUBLANES, ADA_TN), lambda l, j: (l, 0, j)),
        out_shape=jax.ShapeDtypeStruct((DEPTH, SUBLANES, N_MOD * D_MODEL), F32),
        compiler_params=pltpu.CompilerParams(dimension_semantics=("arbitrary", "arbitrary"),
                                             vmem_limit_bytes=2 * _padded_bytes((D_MODEL, ADA_TN), F32) + VMEM_TEMPS),
        name="ada",
    )(cond8, w_ada, b_ada.reshape(DEPTH, 1, N_MOD * D_MODEL))


def _gmlp_kernel(geo, split_in, *refs):
    i = pl.program_id(0)
    tm = geo.tm
    if split_in:
        xc_ref, xs_ref, *refs = refs
        x = jnp.where(i < geo.ncb, xc_ref[...], xs_ref[...])
    else:
        x_ref, *refs = refs
        x = x_ref[...]
    g_ref, mod_ref, win_ref, gv_ref, ws_ref, bs_ref, wout_ref, o_ref, h_ref, u_ref, v_ref, p_ref = refs
    cond = geo.cond_idx(i)
    h_ref[...] = _norm_mod(x, g_ref[...], _mod_row(mod_ref, cond, 1), _mod_row(mod_ref, cond, 0)).astype(BF16)
    ssq = jnp.zeros((tm, LANES), F32)
    for j in range(D_GMLP // TN):
        sl = slice(j * TN, (j + 1) * TN)
        u_ref[:, sl] = _gelu(_dot(h_ref[...], win_ref[:, sl]))
        v = _gelu(_dot(h_ref[...], win_ref[:, D_GMLP + j * TN:D_GMLP + (j + 1) * TN]))
        v_ref[:, sl] = v
        for k in range(TN // LANES):
            vk = v[:, k * LANES:(k + 1) * LANES]
            ssq = ssq + vk * vk
    r = lax.rsqrt(jnp.sum(ssq, axis=-1, keepdims=True) * (1.0 / D_GMLP) + EPS)
    for g in range(N_GROUPS_A):
        sl = slice(g * D_GROUP_A, (g + 1) * D_GROUP_A)
        vn = (v_ref[:, sl] * r * gv_ref[:, sl]).astype(BF16)
        for c in range(tm // CHUNK):
            rows = slice(c * CHUNK, (c + 1) * CHUNK)
            sv = _dot(ws_ref[g], vn[rows, :]) + bs_ref[g]
            p_ref[rows, sl] = (u_ref[rows, sl] * sv).astype(BF16)
    y = _dot(p_ref[...], wout_ref[...])
    o_ref[...] = x + _mod_row(mod_ref, cond, 2) * y


def _gmlp_call(x, layer, j, mods, g_mix, w_in, g_v, w_s, b_s, w_out, sides=()):
    split_in = isinstance(x, tuple)
    geo = ROWS_NARROW if split_in else ROWS_WIDE
    if split_in:
        x_specs = [geo.ctx_rows_spec(D_MODEL), geo.smp_rows_spec(D_MODEL)]
        xs = x
    else:
        x_specs = [geo.row_spec(D_MODEL)]
        xs = (x,)
    return _hosted_call(
        functools.partial(_gmlp_kernel, geo, split_in), geo.nb,
        in_specs=x_specs + [
            _resident((1, D_MODEL), layer),
            _mod_spec(layer),
            _resident((D_MODEL, 2 * D_GMLP)),
            _resident((1, D_GMLP), j),
            _resident((N_GROUPS_A, CHUNK, CHUNK), j),
            _resident((N_GROUPS_A, CHUNK, 1), j),
            _resident((D_GMLP, D_MODEL)),
        ],
        out_specs=[geo.row_spec(D_MODEL)],
        out_shape=[jax.ShapeDtypeStruct((N_ROWS, D_MODEL), F32)],
        scratch_shapes=[
            pltpu.VMEM((geo.tm, D_MODEL), BF16),
            pltpu.VMEM((geo.tm, D_GMLP), F32),
            pltpu.VMEM((geo.tm, D_GMLP), F32),
            pltpu.VMEM((geo.tm, D_GMLP), BF16),
        ],
        sides=sides, name="gmlp",
    )(*xs, g_mix, mods, w_in, g_v, w_s, b_s, w_out, *[s.src for s in sides])


def _pool_kernel(geo, x_ref, xp_ref, xn_ref, g_ref, mod_ref, pw_ref, pb_ref, ps_ref, o_ref, hext_ref):
    i = pl.program_id(0)
    tm = geo.tm
    ext = tm + 2 * HALO
    cond = geo.cond_idx(i)
    g = g_ref[...]
    sc = _mod_row(mod_ref, cond, 1)
    sh = _mod_row(mod_ref, cond, 0)
    gate = _mod_row(mod_ref, cond, 2)
    x = x_ref[...]
    hext_ref[0:tm, :] = _norm_mod(x, g, sc, sh)
    hext_ref[tm:tm + HALO, :] = _norm_mod(xn_ref[...], g, sc, sh)
    hext_ref[tm + HALO:ext, :] = _norm_mod(xp_ref[...], g, sc, sh)
    seq = geo.seq_len(i)
    e = lax.broadcasted_iota(jnp.int32, (ext, D_POOL_GROUP), 0)
    pos = (i * tm + jnp.where(e < tm + HALO, e, e - ext)) & (seq - 1)
    p = (i * tm + lax.broadcasted_iota(jnp.int32, (tm, D_POOL_GROUP), 0)) & (seq - 1)
    for grp in range(N_POOL_GROUPS):
        sl = slice(grp * D_POOL_GROUP, (grp + 1) * D_POOL_GROUP)
        half = POOL_WINDOWS[grp] // 2
        he = hext_ref[:, sl]
        fwd = he
        bwd = jnp.where(pos >= 1, pltpu.roll(he, 1, 0), 0.0)
        span = 1
        while span < half:
            fwd = fwd + jnp.where(pos < seq - span, pltpu.roll(fwd, ext - span, 0), 0.0)
            bwd = bwd + jnp.where(pos >= span, pltpu.roll(bwd, span, 0), 0.0)
            span *= 2
        cnt = (jnp.minimum(p + half - 1, seq - 1) - jnp.maximum(p - half, 0) + 1).astype(F32)
        d = (fwd[:tm, :] + bwd[:tm, :]) / cnt - he[:tm, :]
        y = (_dot(d.astype(BF16), pw_ref[grp]) + pb_ref[:, sl]) * ps_ref[:, sl]
        o_ref[:, sl] = x[:, sl] + gate[:, sl] * y


def _pool_call(x, layer, j, mods, g_mix, p_w, p_b, p_scale, sides=()):
    geo = ROWS_WIDE
    prev, nxt = geo.halo_specs(D_MODEL, HALO)
    return _hosted_call(
        functools.partial(_pool_kernel, geo), geo.nb,
        in_specs=[
            geo.row_spec(D_MODEL), prev, nxt,
            _resident((1, D_MODEL), layer),
            _mod_spec(layer),
            _resident((N_POOL_GROUPS, D_POOL_GROUP, D_POOL_GROUP), j),
            _resident((1, D_MODEL), j),
            _resident((1, D_MODEL), j),
        ],
        out_specs=[geo.row_spec(D_MODEL)],
        out_shape=[jax.ShapeDtypeStruct((N_ROWS, D_MODEL), F32)],
        scratch_shapes=[pltpu.VMEM((geo.tm + 2 * HALO, D_MODEL), F32)],
        sides=sides, name="pool",
    )(x, x, x, g_mix, mods, p_w, p_b, p_scale, *[s.src for s in sides])


def _rope_tables(tm):
    n_freq = HEAD_DIM // 4
    inv = ROPE_THETA ** (-np.arange(n_freq, dtype=np.float64) / n_freq)
    t = np.arange(DEC_SEQ)
    lane = np.arange(LANES) % HEAD_DIM
    quarter, f = lane // n_freq, lane % n_freq
    posn = np.where(quarter[None, :] < 2, (t // GRID_W)[:, None], (t % GRID_W)[:, None])
    ang = posn * inv[f][None, :]
    cos, sin = np.cos(ang), np.sin(ang)
    odd = (quarter % 2 == 1)[None, :]
    tab = np.concatenate([cos, np.where(odd, sin, -sin)], axis=1)
    ident = np.concatenate([np.ones((tm, LANES)), np.zeros((tm, LANES))], axis=1)
    return np.concatenate([ident, tab], axis=0).astype(np.float32)


def _head_meansq(t, bd_ref):
    sq = t * t
    hi = sq.astype(BF16)
    lo = (sq - hi.astype(F32)).astype(BF16)
    return _dot(hi, bd_ref[...]) + _dot(lo, bd_ref[...])


def _rope(t, tab_ref):
    q = HEAD_DIM // 4
    second = (lax.broadcasted_iota(jnp.int32, (1, LANES), 1) & q) != 0
    partner = jnp.where(second, pltpu.roll(t, q, 1), pltpu.roll(t, LANES - q, 1))
    return t * tab_ref[:, 0:LANES] + partner * tab_ref[:, LANES:2 * LANES]


def _kv_ext(t, o_ref):
    lo = lax.broadcasted_iota(jnp.int32, (t.shape[0], LANES), 1) < HEAD_DIM
    for m in range(KV_DIM // LANES):
        a = t[:, m * LANES:(m + 1) * LANES]
        r = pltpu.roll(a, HEAD_DIM, 1)
        base = m * 4 * LANES
        o_ref[:, base:base + LANES] = jnp.where(lo, a, 0.0).astype(o_ref.dtype)
        o_ref[:, base + LANES:base + 2 * LANES] = jnp.where(lo, 0.0, r).astype(o_ref.dtype)
        o_ref[:, base + 2 * LANES:base + 3 * LANES] = jnp.where(lo, r, 0.0).astype(o_ref.dtype)
        o_ref[:, base + 3 * LANES:base + 4 * LANES] = jnp.where(lo, 0.0, a).astype(o_ref.dtype)


def _store_head_rows(o_ref, t):
    for h in range(N_KV_HEADS):
        o_ref[pl.ds(h, t.shape[0], stride=N_KV_HEADS), :] = t[:, h * HEAD_DIM:(h + 1) * HEAD_DIM]


def _qkv_kernel(geo, x_ref, g_ref, mod_ref, w_ref, gq_ref, gk_ref, bd_ref, tab_ref,
                q_ref, kx_ref, vx_ref, kn_ref, v_ref, h_ref, kn_s, v_s):
    i = pl.program_id(0)
    cond = geo.cond_idx(i)
    h_ref[...] = _norm_mod(x_ref[...], g_ref[...], _mod_row(mod_ref, cond, 1),
                           _mod_row(mod_ref, cond, 0)).astype(BF16)
    scale = HEAD_DIM ** -0.5 * LOG2E
    n_q = Q_DIM // TN
    for r0 in range(0, geo.tm, QKV_ROWS):
        rs = slice(r0, r0 + QKV_ROWS)
        tab = tab_ref.at[rs, :]
        hr = h_ref[rs, :]
        ts = [_dot(hr, w_ref[:, c * TN:(c + 1) * TN]) for c in range(n_q + 2)]
        for c in range(n_q):
            t = ts[c]
            t = t * lax.rsqrt(_head_meansq(t, bd_ref) + EPS) * (gq_ref[...] * scale)
            for m in range(TN // LANES):
                q_ref[rs, c * TN + m * LANES:c * TN + (m + 1) * LANES] = _rope(
                    t[:, m * LANES:(m + 1) * LANES], tab).astype(BF16)
        t = ts[n_q]
        t = t * lax.rsqrt(_head_meansq(t, bd_ref) + EPS) * gk_ref[...]
        kn_s[rs, :] = t
        kr = jnp.concatenate([_rope(t[:, m * LANES:(m + 1) * LANES], tab) for m in range(KV_DIM // LANES)], axis=1)
        _kv_ext(kr, kx_ref.at[rs, :])
        v = ts[n_q + 1]
        v_s[rs, :] = v
        _kv_ext(v, vx_ref.at[rs, :])

    @pl.when(i < geo.ncb)
    def _():
        _store_head_rows(kn_ref, kn_s[...])
        _store_head_rows(v_ref, v_s[...])


def _qkv_call(x, layer, j, mods, g_mix, w_qkv, g_q, g_k, bd):
    geo = ROWS_WIDE
    tab = jnp.asarray(_rope_tables(geo.tm))
    tab_spec = pl.BlockSpec((geo.tm, 2 * LANES),
                            lambda i: (jnp.where(i < geo.ncb, 0, 1 + (i - geo.ncb) % geo.bps), 0))
    cache_spec = pl.BlockSpec((geo.tm * N_KV_HEADS, HEAD_DIM), lambda i: (jnp.minimum(i, geo.ncb - 1), 0))
    return _hosted_call(
        functools.partial(_qkv_kernel, geo), geo.nb,
        in_specs=[
            geo.row_spec(D_MODEL),
            _resident((1, D_MODEL), layer),
            _mod_spec(layer),
            _resident((D_MODEL, Q_DIM + 2 * KV_DIM)),
            _resident((1, TN), j),
            _resident((1, TN), j),
            _resident((TN, TN)),
            tab_spec,
        ],
        out_specs=[geo.row_spec(Q_DIM), geo.row_spec(KV_EXT), geo.row_spec(KV_EXT), cache_spec, cache_spec],
        out_shape=[
            jax.ShapeDtypeStruct((N_ROWS, Q_DIM), BF16),
            jax.ShapeDtypeStruct((N_ROWS, KV_EXT), BF16),
            jax.ShapeDtypeStruct((N_ROWS, KV_EXT), BF16),
            jax.ShapeDtypeStruct((N_CTX_ROWS * N_KV_HEADS, HEAD_DIM), F32),
            jax.ShapeDtypeStruct((N_CTX_ROWS * N_KV_HEADS, HEAD_DIM), F32),
        ],
        scratch_shapes=[pltpu.VMEM((geo.tm, D_MODEL), BF16)] + [pltpu.VMEM((geo.tm, KV_DIM), F32)] * 2,
        sides=(), name="qkv",
    )(x, g_mix, mods, w_qkv, g_q, g_k, bd, tab)


def _head_lanes(g, j):
    base = (g * 2 + j) * LANES
    return slice(base, base + LANES)


def _ctx_attn_kernel(geo, j, sink_ref, x_ref, mod_ref, q_ref, kx_ref, vx_ref, wo_ref, o_ref, a_ref):
    for s in range(geo.tm // SEQ):
        rows = slice(s * SEQ, (s + 1) * SEQ)
        for g in range(N_KV_HEADS):
            for pr in range(2):
                qp = q_ref[rows, _head_lanes(g, pr)]
                acc = jnp.zeros((SEQ, LANES), F32)
                for var in range(2):
                    sink = sink_ref[j * N_HEADS + g * Q_PER_KV + 2 * pr + var] * LOG2E
                    sc = _dot_nt(qp, kx_ref[rows, _head_lanes(g, var)])
                    m = jnp.maximum(jnp.max(sc, axis=-1, keepdims=True), sink)
                    e = jnp.exp2(sc - m)
                    den = jnp.sum(e, axis=-1, keepdims=True) + jnp.exp2(sink - m)
                    acc = acc + _dot(e.astype(BF16), vx_ref[rows, _head_lanes(g, var)]) / den
                a_ref[rows, _head_lanes(g, pr)] = acc.astype(BF16)
    y = _dot(a_ref[...], wo_ref[...])
    o_ref[...] = x_ref[...] + _mod_row(mod_ref, 0, 2) * y


def _ctx_attn_call(x, layer, j, mods, sink, q, kx, vx, w_o, sides=()):
    geo = ROWS_WIDE
    return _hosted_call(
        functools.partial(_ctx_attn_kernel, geo, j), geo.ncb,
        in_specs=[
            pl.BlockSpec(memory_space=pltpu.SMEM),
            geo.row_spec(D_MODEL),
            _mod_spec(layer),
            geo.row_spec(Q_DIM), geo.row_spec(KV_EXT), geo.row_spec(KV_EXT),
            _resident((Q_DIM, D_MODEL)),
        ],
        out_specs=[geo.row_spec(D_MODEL)],
        out_shape=[jax.ShapeDtypeStruct((N_ROWS, D_MODEL), F32)],
        scratch_shapes=[pltpu.VMEM((geo.tm, Q_DIM), BF16)],
        sides=sides, name="ctx_attn", aliases={1: 0},
    )(sink, x, mods, q, kx, vx, w_o, *[s.src for s in sides])


def _group_queries(q_ref, rows, g):
    return jnp.concatenate([q_ref[rows, _head_lanes(g, 0)], q_ref[rows, _head_lanes(g, 1)]], axis=0)


def _group_sinks(sink_ref, j, g, n):
    heads = (0, 2, 1, 3)
    return jnp.concatenate(
        [jnp.full((n, 1), sink_ref[j * N_HEADS + g * Q_PER_KV + h] * LOG2E, F32) for h in heads], axis=0)


def _lane_tiles(pieces):
    return [p[:, k * LANES:(k + 1) * LANES] for p in pieces for k in range(p.shape[1] // LANES)]


def _sink_softmax(pieces, sink_col):
    m = jnp.max(functools.reduce(jnp.maximum, _lane_tiles(pieces)), axis=-1, keepdims=True)
    m = jnp.maximum(m, sink_col)
    es = [jnp.exp2(s - m) for s in pieces]
    den = jnp.sum(functools.reduce(jnp.add, _lane_tiles(es)), axis=-1, keepdims=True) + jnp.exp2(sink_col - m)
    return es, 1.0 / den


def _store_group(a_ref, rows, g, pv, n):
    for pr in range(2):
        sub = slice(pr * n, (pr + 1) * n)
        a_ref[rows, _head_lanes(g, pr)] = (pv[0][sub, :] + pv[1][sub, :]).astype(BF16)


def _smp_attn_kernel(geo, j, sink_ref, x_ref, mod_ref, q_ref, kc_ref, kp_ref, kn_ref, vc_ref, vp_ref, vn_ref,
                     ck_ref, cv_ref, wo_ref, o_ref, a_ref, kall_ref, vall_ref):
    i = pl.program_id(0)
    tm = geo.tm
    n_qb = tm // BLOCK
    kwin = tm + 2 * BLOCK
    kall_ref[0:BLOCK, :] = kp_ref[...]
    kall_ref[BLOCK:BLOCK + tm, :] = kc_ref[...]
    kall_ref[BLOCK + tm:kwin, :] = kn_ref[...]
    vall_ref[0:BLOCK, :] = vp_ref[...]
    vall_ref[BLOCK:BLOCK + tm, :] = vc_ref[...]
    vall_ref[BLOCK + tm:kwin, :] = vn_ref[...]
    first = (i % geo.bps) == 0
    last = (i % geo.bps) == geo.bps - 1
    qi = lax.broadcasted_iota(jnp.int32, (Q_PER_KV * BLOCK, BLOCK), 0) & (BLOCK - 1)
    kj = lax.broadcasted_iota(jnp.int32, (Q_PER_KV * BLOCK, BLOCK), 1)
    def scores(b, g):
        rows = slice(b * BLOCK, (b + 1) * BLOCK)
        win = slice(b * BLOCK, (b + 3) * BLOCK)
        q2 = _group_queries(q_ref, rows, g)
        sw = jnp.concatenate([_dot_nt(q2, kall_ref[win, _head_lanes(g, var)]) for var in range(2)], axis=0)
        sc = jnp.concatenate([_dot_nt(q2, ck_ref[:, _head_lanes(g, var)]) for var in range(2)], axis=0)
        return sw, sc

    def probabilities(b, g, sw, sc):
        ok_prev = kj >= (qi + jnp.where(first, BLOCK, 0) if b == 0 else qi)
        ok_next = kj <= (qi - jnp.where(last, BLOCK, 0) if b == n_qb - 1 else qi)
        pieces = [jnp.where(ok_prev, sw[:, 0:BLOCK], NEG_INF), sw[:, BLOCK:2 * BLOCK],
                  jnp.where(ok_next, sw[:, 2 * BLOCK:3 * BLOCK], NEG_INF), sc]
        es, rinv = _sink_softmax(pieces, _group_sinks(sink_ref, j, g, BLOCK))
        return jnp.concatenate(es[:3], axis=-1).astype(BF16), es[3].astype(BF16), rinv

    def weighted_values(b, g, ew, ec, rinv):
        rows = slice(b * BLOCK, (b + 1) * BLOCK)
        win = slice(b * BLOCK, (b + 3) * BLOCK)
        pv = []
        for var in range(2):
            sub = slice(var * 2 * BLOCK, (var + 1) * 2 * BLOCK)
            lanes = _head_lanes(g, var)
            pv.append((_dot(ew[sub, :], vall_ref[win, lanes]) + _dot(ec[sub, :], cv_ref[:, lanes]))
                      * rinv[sub, :])
        _store_group(a_ref, rows, g, pv, BLOCK)

    todo = [(b, g) for b in range(n_qb) for g in range(N_KV_HEADS)]
    ahead = scores(*todo[0])
    behind = None
    for k, (b, g) in enumerate(todo):
        sw, sc = ahead
        if k + 1 < len(todo):
            ahead = scores(*todo[k + 1])
        probs = probabilities(b, g, sw, sc)
        if behind is not None:
            weighted_values(*behind)
        behind = (b, g) + probs
    weighted_values(*behind)
    y = _dot(a_ref[...], wo_ref[...])
    o_ref[...] = x_ref[...] + _mod_row(mod_ref, 1 + i // geo.bps, 2) * y


def _smp_attn_call(x, layer, j, mods, sink, q, kx, vx, ckx, cvx, w_o, sides=()):
    geo = ROWS_NARROW
    prev, nxt = geo.halo_specs(KV_EXT, BLOCK, offset=geo.ncb)
    ctx_spec = pl.BlockSpec((None, None, PAST_LEN, KV_EXT), lambda i: (i // geo.bps, j, 0, 0))
    kwin = geo.tm + 2 * BLOCK
    return _hosted_call(
        functools.partial(_smp_attn_kernel, geo, j), geo.nb - geo.ncb,
        in_specs=[
            pl.BlockSpec(memory_space=pltpu.SMEM),
            geo.row_spec(D_MODEL, geo.ncb),
            _mod_spec(layer),
            geo.row_spec(Q_DIM, geo.ncb),
            geo.row_spec(KV_EXT, geo.ncb), prev, nxt,
            geo.row_spec(KV_EXT, geo.ncb), prev, nxt,
            ctx_spec, ctx_spec,
            _resident((Q_DIM, D_MODEL)),
        ],
        out_specs=[geo.row_spec(D_MODEL, geo.ncb)],
        out_shape=[jax.ShapeDtypeStruct((N_ROWS, D_MODEL), F32)],
        scratch_shapes=[pltpu.VMEM((geo.tm, Q_DIM), BF16), pltpu.VMEM((kwin, KV_EXT), BF16),
                        pltpu.VMEM((kwin, KV_EXT), BF16)],
        sides=sides, name="smp_attn", aliases={1: 0},
    )(sink, x, mods, q, kx, kx, kx, vx, vx, vx, ckx, cvx, w_o, *[s.src for s in sides])


def _ext_heads(t):
    lead = t.shape[:-1]
    h = t.reshape(lead + (N_KV_HEADS, 1, HEAD_DIM))
    z = jnp.zeros_like(h)
    e = jnp.concatenate([jnp.concatenate([h, z], axis=-1), jnp.concatenate([z, h], axis=-1)], axis=-2)
    return e.reshape(lead + (KV_EXT,))


N_FT = D_FF // TN


def _ffn_kernel(geo, split_out, x_ref, xp_ref, xn_ref, g_ref, mod_ref, win_ref, cw_ref, cb_ref, wo_ref, *refs):
    i = pl.program_id(0)
    tm = geo.tm
    ext = tm + 2 * HALO
    hext_ref, gbuf_ref = refs[-2:]
    cond = geo.cond_idx(i)
    g = g_ref[...]
    sh = _mod_row(mod_ref, cond, 3)
    sc = _mod_row(mod_ref, cond, 4)
    x = x_ref[...]
    hext_ref[0:tm, :] = _norm_mod(x, g, sc, sh).astype(BF16)
    hext_ref[tm:tm + HALO, :] = _norm_mod(xn_ref[...], g, sc, sh).astype(BF16)
    hext_ref[tm + HALO:ext, :] = _norm_mod(xp_ref[...], g, sc, sh).astype(BF16)
    seq = geo.seq_len(i)
    pos = (i * tm + lax.broadcasted_iota(jnp.int32, (tm, TN), 0)) & (seq - 1)
    has_prev = pos != 0
    has_next = pos != seq - 1
    for j in range(N_FT):
        sl = slice(j * TN, (j + 1) * TN)
        a = _dot(hext_ref[...], win_ref[:, sl])
        u = _dot(hext_ref[0:tm, :], win_ref[:, D_FF + j * TN:D_FF + (j + 1) * TN])
        a_prev = jnp.where(has_prev, pltpu.roll(a, 1, 0)[:tm, :], 0.0)
        a_next = jnp.where(has_next, pltpu.roll(a, ext - 1, 0)[:tm, :], 0.0)
        a = a_prev * cw_ref[0:1, sl] + a[:tm, :] * cw_ref[1:2, sl] + a_next * cw_ref[2:3, sl] + cb_ref[:, sl]
        gbuf_ref[:, sl] = (_gelu(a) * u).astype(BF16)
    y = _dot(gbuf_ref[...], wo_ref[...])
    res = x + _mod_row(mod_ref, cond, 5) * y
    if split_out:
        oc_ref, os_ref = refs[:2]

        @pl.when(i < geo.ncb)
        def _():
            oc_ref[...] = res

        @pl.when(i >= geo.ncb)
        def _():
            os_ref[...] = res
    else:
        refs[0][...] = res


def _ffn_call(x, layer, mods, g_ffn, w_in, conv_w, conv_b, w_out, split_out, sides=()):
    geo = ROWS_WIDE
    prev, nxt = geo.halo_specs(D_MODEL, HALO)
    if split_out:
        out_specs = [geo.ctx_rows_spec(D_MODEL), geo.smp_rows_spec(D_MODEL)]
        out_shape = [jax.ShapeDtypeStruct((N_CTX_ROWS, D_MODEL), F32),
                     jax.ShapeDtypeStruct((N_SMP_ROWS, D_MODEL), F32)]
    else:
        out_specs = [geo.row_spec(D_MODEL)]
        out_shape = [jax.ShapeDtypeStruct((N_ROWS, D_MODEL), F32)]
    return _hosted_call(
        functools.partial(_ffn_kernel, geo, split_out), geo.nb,
        in_specs=[
            geo.row_spec(D_MODEL), prev, nxt,
            _resident((1, D_MODEL), layer),
            _mod_spec(layer),
            _resident((D_MODEL, 2 * D_FF)),
            _resident((CONV_W, D_FF), layer),
            _resident((1, D_FF), layer),
            _resident((D_FF, D_MODEL)),
        ],
        out_specs=out_specs,
        out_shape=out_shape,
        scratch_shapes=[pltpu.VMEM((geo.tm + 2 * HALO, D_MODEL), BF16), pltpu.VMEM((geo.tm, D_FF), BF16)],
        sides=sides, name="ffn",
    )(x, x, x, g_ffn, mods, w_in, conv_w, conv_b, w_out, *[s.src for s in sides])


def kernel(x_prompt, x_sample, cache_k, cache_v, c, c_ctx, w_ada, b_ada, g_mix, g_ffn, w_ffn_in, ffn_conv_w,
           ffn_conv_b, w_ffn_out, a_w_in, a_g_v, a_w_s, a_b_s, a_w_out, p_w, p_b, p_scale, c_w_qkv, c_g_q,
           c_g_k, c_sink, c_w_o):
    x = (x_prompt.reshape(N_CTX_ROWS, D_MODEL), x_sample.reshape(N_SMP_ROWS, D_MODEL))

    cond8 = jnp.concatenate([c_ctx[None, :], c, jnp.zeros((SUBLANES - N_COND, D_MODEL), F32)], axis=0)
    mods = _ada_call(cond8, w_ada, b_ada)

    g_mix, g_ffn = g_mix[:, None, :], g_ffn[:, None, :]
    ffn_conv_b = ffn_conv_b[:, None, :]
    a_w_s, a_g_v, a_b_s = a_w_s.astype(BF16), a_g_v[:, None, :], a_b_s[:, :, :, None]
    p_w, p_b, p_scale = p_w.astype(BF16), p_b[:, None, :], p_scale[:, None, :]
    gq = jnp.tile(c_g_q, (1, TN // HEAD_DIM))[:, None, :]
    gk = jnp.tile(c_g_k, (1, TN // HEAD_DIM))[:, None, :]
    bd = jnp.asarray(np.kron(np.eye(TN // HEAD_DIM), np.full((HEAD_DIM, HEAD_DIM), 1.0 / HEAD_DIM)), dtype=BF16)
    sink = c_sink.reshape(N_C_LAYERS * N_HEADS)
    ckx = _ext_heads(cache_k.reshape(DEC_BATCH, N_C_LAYERS, PAST_LEN, KV_DIM)).astype(BF16)
    cvx = _ext_heads(cache_v.reshape(DEC_BATCH, N_C_LAYERS, PAST_LEN, KV_DIM)).astype(BF16)

    def ffn_sides(layer, n_in, axis_in, n_out):
        return (_Side(w_ffn_in, layer, axis_in, n_in), _Side(w_ffn_out, layer, 0, n_out))

    assert DEPTH == 4 and N_MIXERS == 3, "the hosting plan below is written for gMLP, pool, attention, gMLP"
    gw = (a_w_in[0].astype(BF16), a_w_out[0].astype(BF16))
    x, *fw = _gmlp_call(x, 0, 0, mods, g_mix, gw[0], a_g_v, a_w_s, a_b_s, gw[1],
                        sides=ffn_sides(0, 2 * N_FT, 1, 2 * N_FT))
    x, *fw = _ffn_call(x, 0, mods, g_ffn, fw[0], ffn_conv_w, ffn_conv_b, fw[1], False,
                       sides=ffn_sides(1, N_FT, 1, N_FT))
    x, w_qkv, w_o, *fw_last = _pool_call(
        x, 1, 0, mods, g_mix, p_w, p_b, p_scale,
        sides=(_Side(c_w_qkv, 0, 1, 12), _Side(c_w_o, 0, 0, 8)) + ffn_sides(3, N_FT, 1, N_FT))
    x, *fw = _ffn_call(x, 1, mods, g_ffn, fw[0], ffn_conv_w, ffn_conv_b, fw[1], False,
                       sides=ffn_sides(2, N_FT, 1, N_FT))
    q, kx, vx, kn, v = _qkv_call(x, 2, 0, mods, g_mix, w_qkv, gq, gk, bd)
    new_k = kn.reshape(BATCH, 1, SEQ, N_KV_HEADS, HEAD_DIM)
    new_v = v.reshape(BATCH, 1, SEQ, N_KV_HEADS, HEAD_DIM)
    (x,) = _ctx_attn_call(x, 2, 0, mods, sink, q, kx, vx, w_o)
    (x,) = _smp_attn_call(x, 2, 0, mods, sink, q, kx, vx, ckx, cvx, w_o)
    x, *gw = _ffn_call(x, 2, mods, g_ffn, fw[0], ffn_conv_w, ffn_conv_b, fw[1], False,
                       sides=(_Side(a_w_in, 1, 1, 8), _Side(a_w_out, 1, 0, 8)))
    (x,) = _gmlp_call(x, 3, 1, mods, g_mix, gw[0], a_g_v, a_w_s, a_b_s, gw[1])
    x = _ffn_call(x, 3, mods, g_ffn, fw_last[0], ffn_conv_w, ffn_conv_b, fw_last[1], True)

    y_prompt = x[0].reshape(BATCH, SEQ, D_MODEL)
    y_sample = x[1].reshape(DEC_BATCH, DEC_SEQ, D_MODEL)
    return (y_prompt, y_sample, new_k, new_v)
```

```python
import functools
from typing import NamedTuple

import numpy as np
import jax
import jax.numpy as jnp
from jax import lax
from jax.experimental import pallas as pl
from jax.experimental.pallas import tpu as pltpu

D_MODEL = 1024
BATCH = 32
SEQ = 256
DEPTH = 4
DEC_BATCH = 2
DEC_SEQ = 2048
PAST_LEN = 256
GRID_W = 64
N_MIXERS = 3
CHUNK = 128
D_GMLP = 2 * D_MODEL
N_GROUPS_A = 8
D_GROUP_A = D_GMLP // N_GROUPS_A
POOL_WINDOWS = (2, 4, 8, 16)
N_POOL_GROUPS = 4
D_POOL_GROUP = D_MODEL // N_POOL_GROUPS
N_HEADS = 16
N_C_LAYERS = DEPTH // N_MIXERS
N_KV_HEADS = 4
HEAD_DIM = 64
Q_PER_KV = N_HEADS // N_KV_HEADS
WINDOW = 128
BLOCK = 128
ROPE_THETA = 10000.0
D_FF = 2816
CONV_W = 3
EPS = 1e-6
NEG_INF = -1e30
LOG2E = 1.4426950408889634

F32 = jnp.float32
BF16 = jnp.bfloat16

N_CTX_ROWS = BATCH * SEQ
N_SMP_ROWS = DEC_BATCH * DEC_SEQ
N_ROWS = N_CTX_ROWS + N_SMP_ROWS
N_COND = 1 + DEC_BATCH
N_MOD = 6
HALO = 16
SUBLANES = 8
LANES = 128
TN = 256
QKV_ROWS = 256
Q_DIM = N_HEADS * HEAD_DIM
KV_DIM = N_KV_HEADS * HEAD_DIM
KV_EXT = N_KV_HEADS * 2 * LANES
VMEM_CAP = 56 * 1024 * 1024
VMEM_TEMPS = 2 * 1024 * 1024


class _Rows:
    def __init__(self, tm):
        self.tm = tm
        self.nb = N_ROWS // tm
        self.ncb = N_CTX_ROWS // tm
        self.bps = DEC_SEQ // tm

    def cond_idx(self, i):
        return jnp.where(i < self.ncb, 0, 1 + (i - self.ncb) // self.bps)

    def seq_len(self, i):
        return jnp.where(i < self.ncb, SEQ, DEC_SEQ)

    def row_spec(self, width, offset=0):
        return pl.BlockSpec((self.tm, width), lambda i: (i + offset, 0))

    def ctx_rows_spec(self, width):
        return pl.BlockSpec((self.tm, width), lambda i: (jnp.minimum(i, self.ncb - 1), 0))

    def smp_rows_spec(self, width):
        return pl.BlockSpec((self.tm, width), lambda i: (jnp.maximum(i - self.ncb, 0), 0))

    def halo_specs(self, width, rows, offset=0):
        per = self.tm // rows
        last = N_ROWS // rows - 1
        prev = pl.BlockSpec((rows, width), lambda i: (jnp.maximum((i + offset) * per - 1, 0), 0))
        nxt = pl.BlockSpec((rows, width), lambda i: (jnp.minimum((i + offset + 1) * per, last), 0))
        return prev, nxt


ROWS_WIDE = _Rows(1024)
ROWS_NARROW = _Rows(512)


def _gelu(x):
    return 0.5 * x * (1.0 + jnp.tanh(0.7978845608028654 * (x + 0.044715 * (x * x * x))))


def _norm_mod(x, g, sc, sh):
    ms = jnp.mean(x * x, axis=-1, keepdims=True)
    return x * lax.rsqrt(ms + EPS) * (g * (1.0 + sc)) + sh


def _mod_row(mod_ref, cond, k):
    return mod_ref[pl.ds(cond, 1), k * D_MODEL:(k + 1) * D_MODEL]


def _dot(a, b):
    return jnp.dot(a, b, preferred_element_type=F32)


def _dot_nt(a, b):
    return lax.dot_general(a, b, (((1,), (1,)), ((), ())), preferred_element_type=F32)


def _padded_bytes(shape, dtype):
    item = jnp.dtype(dtype).itemsize
    dims = [1 if d is None else d for d in shape]
    dims[-1] = -(-dims[-1] // LANES) * LANES
    if len(dims) > 1:
        rows = SUBLANES * 4 // item
        dims[-2] = -(-dims[-2] // rows) * rows
    return item * int(np.prod(dims))


def _vmem_limit(specs, dtypes, scratch_shapes):
    total = 0
    for spec, dtype in zip(specs, dtypes):
        if spec.block_shape is not None:
            buffers = 2 if spec.pipeline_mode is None else spec.pipeline_mode.buffer_count
            total += buffers * _padded_bytes(spec.block_shape, dtype)
    total += sum(_padded_bytes(s.shape, s.dtype) for s in scratch_shapes)
    assert total + VMEM_TEMPS <= VMEM_CAP, f"declared VMEM buffers too large: {total} bytes"
    return VMEM_CAP


def _resident(shape, layer=None):
    nd = len(shape)
    if layer is None:
        return pl.BlockSpec(shape, lambda *_: (0,) * nd, pipeline_mode=pl.Buffered(1))
    return pl.BlockSpec((None,) + tuple(shape), lambda *_: (layer,) + (0,) * nd, pipeline_mode=pl.Buffered(1))


def _mod_spec(layer):
    return _resident((SUBLANES, N_MOD * D_MODEL), layer)


class _Side(NamedTuple):
    src: jax.Array
    layer: int
    axis: int
    n: int


def _side_plumbing(sides):
    in_specs, out_specs, out_shape = [], [], []
    for s in sides:
        _, r, c = s.src.shape
        blk = (r // s.n, c) if s.axis == 0 else (r, c // s.n)

        def idx(i, s=s):
            k = jnp.minimum(i, s.n - 1)
            return (k, 0) if s.axis == 0 else (0, k)

        in_specs.append(pl.BlockSpec((None,) + blk, lambda i, s=s, idx=idx: (s.layer,) + idx(i)))
        out_specs.append(pl.BlockSpec(blk, idx))
        out_shape.append(jax.ShapeDtypeStruct((r, c), BF16))
    return in_specs, out_specs, out_shape


def _with_sides(body, n_in, n_out, n_side):
    def kernel(*refs):
        ins, rest = refs[:n_in], refs[n_in:]
        side_in, rest = rest[:n_side], rest[n_side:]
        outs, rest = rest[:n_out], rest[n_out:]
        side_out, scratch = rest[:n_side], rest[n_side:]
        for src, dst in zip(side_in, side_out):
            dst[...] = src[...].astype(BF16)
        body(*ins, *outs, *scratch)
    return kernel


def _hosted_call(body, geo_steps, in_specs, out_specs, out_shape, scratch_shapes, sides, name, aliases=None):
    s_in, s_out, s_shape = _side_plumbing(sides)
    assert all(s.n <= geo_steps for s in sides)
    in_specs, out_specs, out_shape = list(in_specs) + s_in, list(out_specs) + s_out, list(out_shape) + s_shape

    def run(*operands):
        dtypes = [o.dtype for o in operands] + [o.dtype for o in out_shape]
        return pl.pallas_call(
            _with_sides(body, len(in_specs) - len(sides), len(out_specs) - len(sides), len(sides)),
            grid=(geo_steps,),
            in_specs=in_specs,
            out_specs=out_specs,
            out_shape=out_shape,
            input_output_aliases=aliases or {},
            scratch_shapes=scratch_shapes,
            compiler_params=pltpu.CompilerParams(
                dimension_semantics=("arbitrary",),
                vmem_limit_bytes=_vmem_limit(in_specs + out_specs, dtypes, scratch_shapes)),
            name=name,
        )(*operands)

    return run


ADA_TN = 3072


def _ada_kernel(c_ref, w_ref, b_ref, o_ref):
    c = c_ref[...]
    s = (c * (1.0 / (1.0 + jnp.exp(-c)))).astype(BF16)
    o_ref[...] = _dot(s, w_ref[...].astype(BF16)) + b_ref[...]


def _ada_call(cond8, w_ada, b_ada):
    nt = N_MOD * D_MODEL // ADA_TN
    return pl.pallas_call(
        _ada_kernel,
        grid=(DEPTH, nt),
        in_specs=[
            pl.BlockSpec((SUBLANES, D_MODEL), lambda l, j: (0, 0)),
            pl.BlockSpec((None, D_MODEL, ADA_TN), lambda l, j: (l, 0, j)),
            pl.BlockSpec((None, 1, ADA_TN), lambda l, j: (l, 0, j)),
        ],
        out_specs=pl.BlockSpec((None, SUBLANES, ADA_TN), lambda l, j: (l, 0, j)),
        out_shape=jax.ShapeDtypeStruct((DEPTH, SUBLANES, N_MOD * D_MODEL), F32),
        compiler_params=pltpu.CompilerParams(dimension_semantics=("arbitrary", "arbitrary"),
                                             vmem_limit_bytes=2 * _padded_bytes((D_MODEL, ADA_TN), F32) + VMEM_TEMPS),
        name="ada",
    )(cond8, w_ada, b_ada.reshape(DEPTH, 1, N_MOD * D_MODEL))


def _gmlp_kernel(geo, split_in, *refs):
    i = pl.program_id(0)
    tm = geo.tm
    if split_in:
        xc_ref, xs_ref, *refs = refs
        x = jnp.where(i < geo.ncb, xc_ref[...], xs_ref[...])
    else:
        x_ref, *refs = refs
        x = x_ref[...]
    g_ref, mod_ref, win_ref, gv_ref, ws_ref, bs_ref, wout_ref, o_ref, h_ref, u_ref, v_ref, p_ref = refs
    cond = geo.cond_idx(i)
    h_ref[...] = _norm_mod(x, g_ref[...], _mod_row(mod_ref, cond, 1), _mod_row(mod_ref, cond, 0)).astype(BF16)
    ssq = jnp.zeros((tm, LANES), F32)
    for j in range(D_GMLP // TN):
        sl = slice(j * TN, (j + 1) * TN)
        u_ref[:, sl] = _gelu(_dot(h_ref[...], win_ref[:, sl]))
        v = _gelu(_dot(h_ref[...], win_ref[:, D_GMLP + j * TN:D_GMLP + (j + 1) * TN]))
        v_ref[:, sl] = v
        for k in range(TN // LANES):
            vk = v[:, k * LANES:(k + 1) * LANES]
            ssq = ssq + vk * vk
    r = lax.rsqrt(jnp.sum(ssq, axis=-1, keepdims=True) * (1.0 / D_GMLP) + EPS)
    for g in range(N_GROUPS_A):
        sl = slice(g * D_GROUP_A, (g + 1) * D_GROUP_A)
        vn = (v_ref[:, sl] * r * gv_ref[:, sl]).astype(BF16)
        for c in range(tm // CHUNK):
            rows = slice(c * CHUNK, (c + 1) * CHUNK)
            sv = _dot(ws_ref[g], vn[rows, :]) + bs_ref[g]
            p_ref[rows, sl] = (u_ref[rows, sl] * sv).astype(BF16)
    y = _dot(p_ref[...], wout_ref[...])
    o_ref[...] = x + _mod_row(mod_ref, cond, 2) * y


def _gmlp_call(x, layer, j, mods, g_mix, w_in, g_v, w_s, b_s, w_out, sides=()):
    split_in = isinstance(x, tuple)
    geo = ROWS_NARROW if split_in else ROWS_WIDE
    if split_in:
        x_specs = [geo.ctx_rows_spec(D_MODEL), geo.smp_rows_spec(D_MODEL)]
        xs = x
    else:
        x_specs = [geo.row_spec(D_MODEL)]
        xs = (x,)
    return _hosted_call(
        functools.partial(_gmlp_kernel, geo, split_in), geo.nb,
        in_specs=x_specs + [
            _resident((1, D_MODEL), layer),
            _mod_spec(layer),
            _resident((D_MODEL, 2 * D_GMLP)),
            _resident((1, D_GMLP), j),
            _resident((N_GROUPS_A, CHUNK, CHUNK), j),
            _resident((N_GROUPS_A, CHUNK, 1), j),
            _resident((D_GMLP, D_MODEL)),
        ],
        out_specs=[geo.row_spec(D_MODEL)],
        out_shape=[jax.ShapeDtypeStruct((N_ROWS, D_MODEL), F32)],
        scratch_shapes=[
            pltpu.VMEM((geo.tm, D_MODEL), BF16),
            pltpu.VMEM((geo.tm, D_GMLP), F32),
            pltpu.VMEM((geo.tm, D_GMLP), F32),
            pltpu.VMEM((geo.tm, D_GMLP), BF16),
        ],
        sides=sides, name="gmlp",
    )(*xs, g_mix, mods, w_in, g_v, w_s, b_s, w_out, *[s.src for s in sides])


def _pool_kernel(geo, x_ref, xp_ref, xn_ref, g_ref, mod_ref, pw_ref, pb_ref, ps_ref, o_ref, hext_ref):
    i = pl.program_id(0)
    tm = geo.tm
    ext = tm + 2 * HALO
    cond = geo.cond_idx(i)
    g = g_ref[...]
    sc = _mod_row(mod_ref, cond, 1)
    sh = _mod_row(mod_ref, cond, 0)
    gate = _mod_row(mod_ref, cond, 2)
    x = x_ref[...]
    hext_ref[0:tm, :] = _norm_mod(x, g, sc, sh)
    hext_ref[tm:tm + HALO, :] = _norm_mod(xn_ref[...], g, sc, sh)
    hext_ref[tm + HALO:ext, :] = _norm_mod(xp_ref[...], g, sc, sh)
    seq = geo.seq_len(i)
    e = lax.broadcasted_iota(jnp.int32, (ext, LANES), 0)
    pos = (i * tm + jnp.where(e < tm + HALO, e, e - ext)) & (seq - 1)
    p = (i * tm + lax.broadcasted_iota(jnp.int32, (tm, LANES), 0)) & (seq - 1)
    spans = [1 << k for k in range(max(POOL_WINDOWS).bit_length() - 2)]
    same_fwd = {sp: jnp.where(pos < seq - sp, 1.0, 0.0) for sp in spans}
    same_bwd = {sp: jnp.where(pos >= sp, 1.0, 0.0) for sp in spans}
    for grp in range(N_POOL_GROUPS):
        sl = slice(grp * D_POOL_GROUP, (grp + 1) * D_POOL_GROUP)
        half = POOL_WINDOWS[grp] // 2
        cnt = (jnp.minimum(p + half - 1, seq - 1) - jnp.maximum(p - half, 0) + 1).astype(F32)
        ds = []
        for k in range(D_POOL_GROUP // LANES):
            he = hext_ref[:, grp * D_POOL_GROUP + k * LANES:grp * D_POOL_GROUP + (k + 1) * LANES]
            fwd = he
            bwd = same_bwd[1] * pltpu.roll(he, 1, 0)
            span = 1
            while span < half:
                fwd = fwd + same_fwd[span] * pltpu.roll(fwd, ext - span, 0)
                bwd = bwd + same_bwd[span] * pltpu.roll(bwd, span, 0)
                span *= 2
            ds.append((fwd[:tm, :] + bwd[:tm, :]) / cnt - he[:tm, :])
        d = jnp.concatenate(ds, axis=1)
        y = (_dot(d.astype(BF16), pw_ref[grp]) + pb_ref[:, sl]) * ps_ref[:, sl]
        o_ref[:, sl] = x[:, sl] + gate[:, sl] * y


def _pool_call(x, layer, j, mods, g_mix, p_w, p_b, p_scale, sides=()):
    geo = ROWS_WIDE
    prev, nxt = geo.halo_specs(D_MODEL, HALO)
    return _hosted_call(
        functools.partial(_pool_kernel, geo), geo.nb,
        in_specs=[
            geo.row_spec(D_MODEL), prev, nxt,
            _resident((1, D_MODEL), layer),
            _mod_spec(layer),
            _resident((N_POOL_GROUPS, D_POOL_GROUP, D_POOL_GROUP), j),
            _resident((1, D_MODEL), j),
            _resident((1, D_MODEL), j),
        ],
        out_specs=[geo.row_spec(D_MODEL)],
        out_shape=[jax.ShapeDtypeStruct((N_ROWS, D_MODEL), F32)],
        scratch_shapes=[pltpu.VMEM((geo.tm + 2 * HALO, D_MODEL), F32)],
        sides=sides, name="pool",
    )(x, x, x, g_mix, mods, p_w, p_b, p_scale, *[s.src for s in sides])


def _rope_tables(tm):
    n_freq = HEAD_DIM // 4
    inv = ROPE_THETA ** (-np.arange(n_freq, dtype=np.float64) / n_freq)
    t = np.arange(DEC_SEQ)
    lane = np.arange(LANES) % HEAD_DIM
    quarter, f = lane // n_freq, lane % n_freq
    posn = np.where(quarter[None, :] < 2, (t // GRID_W)[:, None], (t % GRID_W)[:, None])
    ang = posn * inv[f][None, :]
    cos, sin = np.cos(ang), np.sin(ang)
    odd = (quarter % 2 == 1)[None, :]
    tab = np.concatenate([cos, np.where(odd, sin, -sin)], axis=1)
    ident = np.concatenate([np.ones((tm, LANES)), np.zeros((tm, LANES))], axis=1)
    return np.concatenate([ident, tab], axis=0).astype(np.float32)


def _head_meansq(t, bd_ref):
    sq = t * t
    hi = sq.astype(BF16)
    lo = (sq - hi.astype(F32)).astype(BF16)
    return _dot(hi, bd_ref[...]) + _dot(lo, bd_ref[...])


def _rope(t, tab_ref):
    q = HEAD_DIM // 4
    second = (lax.broadcasted_iota(jnp.int32, (1, LANES), 1) & q) != 0
    partner = jnp.where(second, pltpu.roll(t, q, 1), pltpu.roll(t, LANES - q, 1))
    return t * tab_ref[:, 0:LANES] + partner * tab_ref[:, LANES:2 * LANES]


def _kv_ext(t, o_ref):
    lo = lax.broadcasted_iota(jnp.int32, (t.shape[0], LANES), 1) < HEAD_DIM
    for m in range(KV_DIM // LANES):
        a = t[:, m * LANES:(m + 1) * LANES]
        r = pltpu.roll(a, HEAD_DIM, 1)
        base = m * 4 * LANES
        o_ref[:, base:base + LANES] = jnp.where(lo, a, 0.0).astype(o_ref.dtype)
        o_ref[:, base + LANES:base + 2 * LANES] = jnp.where(lo, 0.0, r).astype(o_ref.dtype)
        o_ref[:, base + 2 * LANES:base + 3 * LANES] = jnp.where(lo, r, 0.0).astype(o_ref.dtype)
        o_ref[:, base + 3 * LANES:base + 4 * LANES] = jnp.where(lo, 0.0, a).astype(o_ref.dtype)


def _store_head_rows(o_ref, t):
    for h in range(N_KV_HEADS):
        o_ref[pl.ds(h, t.shape[0], stride=N_KV_HEADS), :] = t[:, h * HEAD_DIM:(h + 1) * HEAD_DIM]


def _qkv_kernel(geo, x_ref, g_ref, mod_ref, w_ref, gq_ref, gk_ref, bd_ref, tab_ref,
                q_ref, kx_ref, vx_ref, kn_ref, v_ref, h_ref, kn_s, v_s):
    i = pl.program_id(0)
    cond = geo.cond_idx(i)
    h_ref[...] = _norm_mod(x_ref[...], g_ref[...], _mod_row(mod_ref, cond, 1),
                           _mod_row(mod_ref, cond, 0)).astype(BF16)
    scale = HEAD_DIM ** -0.5 * LOG2E
    n_q = Q_DIM // TN
    for r0 in range(0, geo.tm, QKV_ROWS):
        rs = slice(r0, r0 + QKV_ROWS)
        tab = tab_ref.at[rs, :]
        hr = h_ref[rs, :]
        ts = [_dot(hr, w_ref[:, c * TN:(c + 1) * TN]) for c in range(n_q + 2)]
        for c in range(n_q):
            t = ts[c]
            t = t * lax.rsqrt(_head_meansq(t, bd_ref) + EPS) * (gq_ref[...] * scale)
            for m in range(TN // LANES):
                q_ref[rs, c * TN + m * LANES:c * TN + (m + 1) * LANES] = _rope(
                    t[:, m * LANES:(m + 1) * LANES], tab).astype(BF16)
        t = ts[n_q]
        t = t * lax.rsqrt(_head_meansq(t, bd_ref) + EPS) * gk_ref[...]
        kn_s[rs, :] = t
        kr = jnp.concatenate([_rope(t[:, m * LANES:(m + 1) * LANES], tab) for m in range(KV_DIM // LANES)], axis=1)
        _kv_ext(kr, kx_ref.at[rs, :])
        v = ts[n_q + 1]
        v_s[rs, :] = v
        _kv_ext(v, vx_ref.at[rs, :])

    @pl.when(i < geo.ncb)
    def _():
        _store_head_rows(kn_ref, kn_s[...])
        _store_head_rows(v_ref, v_s[...])


def _qkv_call(x, layer, j, mods, g_mix, w_qkv, g_q, g_k, bd):
    geo = ROWS_WIDE
    tab = jnp.asarray(_rope_tables(geo.tm))
    tab_spec = pl.BlockSpec((geo.tm, 2 * LANES),
                            lambda i: (jnp.where(i < geo.ncb, 0, 1 + (i - geo.ncb) % geo.bps), 0))
    cache_spec = pl.BlockSpec((geo.tm * N_KV_HEADS, HEAD_DIM), lambda i: (jnp.minimum(i, geo.ncb - 1), 0))
    return _hosted_call(
        functools.partial(_qkv_kernel, geo), geo.nb,
        in_specs=[
            geo.row_spec(D_MODEL),
            _resident((1, D_MODEL), layer),
            _mod_spec(layer),
            _resident((D_MODEL, Q_DIM + 2 * KV_DIM)),
            _resident((1, TN), j),
            _resident((1, TN), j),
            _resident((TN, TN)),
            tab_spec,
        ],
        out_specs=[geo.row_spec(Q_DIM), geo.row_spec(KV_EXT), geo.row_spec(KV_EXT), cache_spec, cache_spec],
        out_shape=[
            jax.ShapeDtypeStruct((N_ROWS, Q_DIM), BF16),
            jax.ShapeDtypeStruct((N_ROWS, KV_EXT), BF16),
            jax.ShapeDtypeStruct((N_ROWS, KV_EXT), BF16),
            jax.ShapeDtypeStruct((N_CTX_ROWS * N_KV_HEADS, HEAD_DIM), F32),
            jax.ShapeDtypeStruct((N_CTX_ROWS * N_KV_HEADS, HEAD_DIM), F32),
        ],
        scratch_shapes=[pltpu.VMEM((geo.tm, D_MODEL), BF16)] + [pltpu.VMEM((geo.tm, KV_DIM), F32)] * 2,
        sides=(), name="qkv",
    )(x, g_mix, mods, w_qkv, g_q, g_k, bd, tab)


def _head_lanes(g, j):
    base = (g * 2 + j) * LANES
    return slice(base, base + LANES)


def _ctx_attn_kernel(geo, j, sink_ref, x_ref, mod_ref, q_ref, kx_ref, vx_ref, wo_ref, o_ref, a_ref):
    for s in range(geo.tm // SEQ):
        rows = slice(s * SEQ, (s + 1) * SEQ)
        for g in range(N_KV_HEADS):
            for pr in range(2):
                qp = q_ref[rows, _head_lanes(g, pr)]
                acc = jnp.zeros((SEQ, LANES), F32)
                for var in range(2):
                    sink = sink_ref[j * N_HEADS + g * Q_PER_KV + 2 * pr + var] * LOG2E
                    sc = _dot_nt(qp, kx_ref[rows, _head_lanes(g, var)])
                    m = jnp.maximum(jnp.max(sc, axis=-1, keepdims=True), sink)
                    e = jnp.exp2(sc - m)
                    den = jnp.sum(e, axis=-1, keepdims=True) + jnp.exp2(sink - m)
                    acc = acc + _dot(e.astype(BF16), vx_ref[rows, _head_lanes(g, var)]) / den
                a_ref[rows, _head_lanes(g, pr)] = acc.astype(BF16)
    y = _dot(a_ref[...], wo_ref[...])
    o_ref[...] = x_ref[...] + _mod_row(mod_ref, 0, 2) * y


def _ctx_attn_call(x, layer, j, mods, sink, q, kx, vx, w_o, sides=()):
    geo = ROWS_WIDE
    return _hosted_call(
        functools.partial(_ctx_attn_kernel, geo, j), geo.ncb,
        in_specs=[
            pl.BlockSpec(memory_space=pltpu.SMEM),
            geo.row_spec(D_MODEL),
            _mod_spec(layer),
            geo.row_spec(Q_DIM), geo.row_spec(KV_EXT), geo.row_spec(KV_EXT),
            _resident((Q_DIM, D_MODEL)),
        ],
        out_specs=[geo.row_spec(D_MODEL)],
        out_shape=[jax.ShapeDtypeStruct((N_ROWS, D_MODEL), F32)],
        scratch_shapes=[pltpu.VMEM((geo.tm, Q_DIM), BF16)],
        sides=sides, name="ctx_attn", aliases={1: 0},
    )(sink, x, mods, q, kx, vx, w_o, *[s.src for s in sides])


def _group_queries(q_ref, rows, g):
    return jnp.concatenate([q_ref[rows, _head_lanes(g, 0)], q_ref[rows, _head_lanes(g, 1)]], axis=0)


def _group_sinks(sink_ref, j, g, n):
    heads = (0, 2, 1, 3)
    return jnp.concatenate(
        [jnp.full((n, 1), sink_ref[j * N_HEADS + g * Q_PER_KV + h] * LOG2E, F32) for h in heads], axis=0)


def _lane_tiles(pieces):
    return [p[:, k * LANES:(k + 1) * LANES] for p in pieces for k in range(p.shape[1] // LANES)]


def _sink_softmax(pieces, sink_col):
    m = jnp.max(functools.reduce(jnp.maximum, _lane_tiles(pieces)), axis=-1, keepdims=True)
    m = jnp.maximum(m, sink_col)
    es = [jnp.exp2(s - m) for s in pieces]
    den = jnp.sum(functools.reduce(jnp.add, _lane_tiles(es)), axis=-1, keepdims=True) + jnp.exp2(sink_col - m)
    return es, 1.0 / den


def _store_group(a_ref, rows, g, pv, n):
    for pr in range(2):
        sub = slice(pr * n, (pr + 1) * n)
        a_ref[rows, _head_lanes(g, pr)] = (pv[0][sub, :] + pv[1][sub, :]).astype(BF16)


def _smp_attn_kernel(geo, j, sink_ref, x_ref, mod_ref, q_ref, kc_ref, kp_ref, kn_ref, vc_ref, vp_ref, vn_ref,
                     ck_ref, cv_ref, wo_ref, o_ref, a_ref, kall_ref, vall_ref):
    i = pl.program_id(0)
    tm = geo.tm
    n_qb = tm // BLOCK
    kwin = tm + 2 * BLOCK
    kall_ref[0:BLOCK, :] = kp_ref[...]
    kall_ref[BLOCK:BLOCK + tm, :] = kc_ref[...]
    kall_ref[BLOCK + tm:kwin, :] = kn_ref[...]
    vall_ref[0:BLOCK, :] = vp_ref[...]
    vall_ref[BLOCK:BLOCK + tm, :] = vc_ref[...]
    vall_ref[BLOCK + tm:kwin, :] = vn_ref[...]
    first = (i % geo.bps) == 0
    last = (i % geo.bps) == geo.bps - 1
    qi = lax.broadcasted_iota(jnp.int32, (Q_PER_KV * BLOCK, BLOCK), 0) & (BLOCK - 1)
    kj = lax.broadcasted_iota(jnp.int32, (Q_PER_KV * BLOCK, BLOCK), 1)
    def scores(b, g):
        rows = slice(b * BLOCK, (b + 1) * BLOCK)
        win = slice(b * BLOCK, (b + 3) * BLOCK)
        q2 = _group_queries(q_ref, rows, g)
        sw = jnp.concatenate([_dot_nt(q2, kall_ref[win, _head_lanes(g, var)]) for var in range(2)], axis=0)
        sc = jnp.concatenate([_dot_nt(q2, ck_ref[:, _head_lanes(g, var)]) for var in range(2)], axis=0)
        return sw, sc

    def probabilities(b, g, sw, sc):
        ok_prev = kj >= (qi + jnp.where(first, BLOCK, 0) if b == 0 else qi)
        ok_next = kj <= (qi - jnp.where(last, BLOCK, 0) if b == n_qb - 1 else qi)
        pieces = [jnp.where(ok_prev, sw[:, 0:BLOCK], NEG_INF), sw[:, BLOCK:2 * BLOCK],
                  jnp.where(ok_next, sw[:, 2 * BLOCK:3 * BLOCK], NEG_INF), sc]
        es, rinv = _sink_softmax(pieces, _group_sinks(sink_ref, j, g, BLOCK))
        return jnp.concatenate(es[:3], axis=-1).astype(BF16), es[3].astype(BF16), rinv

    def weighted_values(b, g, ew, ec, rinv):
        rows = slice(b * BLOCK, (b + 1) * BLOCK)
        win = slice(b * BLOCK, (b + 3) * BLOCK)
        pv = []
        for var in range(2):
            sub = slice(var * 2 * BLOCK, (var + 1) * 2 * BLOCK)
            lanes = _head_lanes(g, var)
            pv.append((_dot(ew[sub, :], vall_ref[win, lanes]) + _dot(ec[sub, :], cv_ref[:, lanes]))
                      * rinv[sub, :])
        _store_group(a_ref, rows, g, pv, BLOCK)

    todo = [(b, g) for b in range(n_qb) for g in range(N_KV_HEADS)]
    ahead = scores(*todo[0])
    behind = None
    for k, (b, g) in enumerate(todo):
        sw, sc = ahead
        if k + 1 < len(todo):
            ahead = scores(*todo[k + 1])
        probs = probabilities(b, g, sw, sc)
        if behind is not None:
            weighted_values(*behind)
        behind = (b, g) + probs
    weighted_values(*behind)
    y = _dot(a_ref[...], wo_ref[...])
    o_ref[...] = x_ref[...] + _mod_row(mod_ref, 1 + i // geo.bps, 2) * y


def _smp_attn_call(x, layer, j, mods, sink, q, kx, vx, ckx, cvx, w_o, sides=()):
    geo = ROWS_NARROW
    prev, nxt = geo.halo_specs(KV_EXT, BLOCK, offset=geo.ncb)
    ctx_spec = pl.BlockSpec((None, None, PAST_LEN, KV_EXT), lambda i: (i // geo.bps, j, 0, 0))
    kwin = geo.tm + 2 * BLOCK
    return _hosted_call(
        functools.partial(_smp_attn_kernel, geo, j), geo.nb - geo.ncb,
        in_specs=[
            pl.BlockSpec(memory_space=pltpu.SMEM),
            geo.row_spec(D_MODEL, geo.ncb),
            _mod_spec(layer),
            geo.row_spec(Q_DIM, geo.ncb),
            geo.row_spec(KV_EXT, geo.ncb), prev, nxt,
            geo.row_spec(KV_EXT, geo.ncb), prev, nxt,
            ctx_spec, ctx_spec,
            _resident((Q_DIM, D_MODEL)),
        ],
        out_specs=[geo.row_spec(D_MODEL, geo.ncb)],
        out_shape=[jax.ShapeDtypeStruct((N_ROWS, D_MODEL), F32)],
        scratch_shapes=[pltpu.VMEM((geo.tm, Q_DIM), BF16), pltpu.VMEM((kwin, KV_EXT), BF16),
                        pltpu.VMEM((kwin, KV_EXT), BF16)],
        sides=sides, name="smp_attn", aliases={1: 0},
    )(sink, x, mods, q, kx, kx, kx, vx, vx, vx, ckx, cvx, w_o, *[s.src for s in sides])


def _ext_heads(t):
    lead = t.shape[:-1]
    h = t.reshape(lead + (N_KV_HEADS, 1, HEAD_DIM))
    z = jnp.zeros_like(h)
    e = jnp.concatenate([jnp.concatenate([h, z], axis=-1), jnp.concatenate([z, h], axis=-1)], axis=-2)
    return e.reshape(lead + (KV_EXT,))


N_FT = D_FF // TN


def _ffn_kernel(geo, split_out, x_ref, xp_ref, xn_ref, g_ref, mod_ref, win_ref, cw_ref, cb_ref, wo_ref, *refs):
    i = pl.program_id(0)
    tm = geo.tm
    ext = tm + 2 * HALO
    hext_ref, gbuf_ref = refs[-2:]
    cond = geo.cond_idx(i)
    g = g_ref[...]
    sh = _mod_row(mod_ref, cond, 3)
    sc = _mod_row(mod_ref, cond, 4)
    x = x_ref[...]
    hext_ref[0:tm, :] = _norm_mod(x, g, sc, sh).astype(BF16)
    hext_ref[tm:tm + HALO, :] = _norm_mod(xn_ref[...], g, sc, sh).astype(BF16)
    hext_ref[tm + HALO:ext, :] = _norm_mod(xp_ref[...], g, sc, sh).astype(BF16)
    seq = geo.seq_len(i)
    pos = (i * tm + lax.broadcasted_iota(jnp.int32, (tm, TN), 0)) & (seq - 1)
    has_prev = pos != 0
    has_next = pos != seq - 1
    for j in range(N_FT):
        sl = slice(j * TN, (j + 1) * TN)
        a = _dot(hext_ref[...], win_ref[:, sl])
        u = _dot(hext_ref[0:tm, :], win_ref[:, D_FF + j * TN:D_FF + (j + 1) * TN])
        a_prev = jnp.where(has_prev, pltpu.roll(a, 1, 0)[:tm, :], 0.0)
        a_next = jnp.where(has_next, pltpu.roll(a, ext - 1, 0)[:tm, :], 0.0)
        a = a_prev * cw_ref[0:1, sl] + a[:tm, :] * cw_ref[1:2, sl] + a_next * cw_ref[2:3, sl] + cb_ref[:, sl]
        gbuf_ref[:, sl] = (_gelu(a) * u).astype(BF16)
    y = _dot(gbuf_ref[...], wo_ref[...])
    res = x + _mod_row(mod_ref, cond, 5) * y
    if split_out:
        oc_ref, os_ref = refs[:2]

        @pl.when(i < geo.ncb)
        def _():
            oc_ref[...] = res

        @pl.when(i >= geo.ncb)
        def _():
            os_ref[...] = res
    else:
        refs[0][...] = res


def _ffn_call(x, layer, mods, g_ffn, w_in, conv_w, conv_b, w_out, split_out, sides=()):
    geo = ROWS_WIDE
    prev, nxt = geo.halo_specs(D_MODEL, HALO)
    if split_out:
        out_specs = [geo.ctx_rows_spec(D_MODEL), geo.smp_rows_spec(D_MODEL)]
        out_shape = [jax.ShapeDtypeStruct((N_CTX_ROWS, D_MODEL), F32),
                     jax.ShapeDtypeStruct((N_SMP_ROWS, D_MODEL), F32)]
    else:
        out_specs = [geo.row_spec(D_MODEL)]
        out_shape = [jax.ShapeDtypeStruct((N_ROWS, D_MODEL), F32)]
    return _hosted_call(
        functools.partial(_ffn_kernel, geo, split_out), geo.nb,
        in_specs=[
            geo.row_spec(D_MODEL), prev, nxt,
            _resident((1, D_MODEL), layer),
            _mod_spec(layer),
            _resident((D_MODEL, 2 * D_FF)),
            _resident((CONV_W, D_FF), layer),
            _resident((1, D_FF), layer),
            _resident((D_FF, D_MODEL)),
        ],
        out_specs=out_specs,
        out_shape=out_shape,
        scratch_shapes=[pltpu.VMEM((geo.tm + 2 * HALO, D_MODEL), BF16), pltpu.VMEM((geo.tm, D_FF), BF16)],
        sides=sides, name="ffn",
    )(x, x, x, g_ffn, mods, w_in, conv_w, conv_b, w_out, *[s.src for s in sides])


def kernel(x_prompt, x_sample, cache_k, cache_v, c, c_ctx, w_ada, b_ada, g_mix, g_ffn, w_ffn_in, ffn_conv_w,
           ffn_conv_b, w_ffn_out, a_w_in, a_g_v, a_w_s, a_b_s, a_w_out, p_w, p_b, p_scale, c_w_qkv, c_g_q,
           c_g_k, c_sink, c_w_o):
    x = (x_prompt.reshape(N_CTX_ROWS, D_MODEL), x_sample.reshape(N_SMP_ROWS, D_MODEL))

    cond8 = jnp.concatenate([c_ctx[None, :], c, jnp.zeros((SUBLANES - N_COND, D_MODEL), F32)], axis=0)
    mods = _ada_call(cond8, w_ada, b_ada)

    g_mix, g_ffn = g_mix[:, None, :], g_ffn[:, None, :]
    ffn_conv_b = ffn_conv_b[:, None, :]
    a_w_s, a_g_v, a_b_s = a_w_s.astype(BF16), a_g_v[:, None, :], a_b_s[:, :, :, None]
    p_w, p_b, p_scale = p_w.astype(BF16), p_b[:, None, :], p_scale[:, None, :]
    gq = jnp.tile(c_g_q, (1, TN // HEAD_DIM))[:, None, :]
    gk = jnp.tile(c_g_k, (1, TN // HEAD_DIM))[:, None, :]
    bd = jnp.asarray(np.kron(np.eye(TN // HEAD_DIM), np.full((HEAD_DIM, HEAD_DIM), 1.0 / HEAD_DIM)), dtype=BF16)
    sink = c_sink.reshape(N_C_LAYERS * N_HEADS)
    ckx = _ext_heads(cache_k.reshape(DEC_BATCH, N_C_LAYERS, PAST_LEN, KV_DIM)).astype(BF16)
    cvx = _ext_heads(cache_v.reshape(DEC_BATCH, N_C_LAYERS, PAST_LEN, KV_DIM)).astype(BF16)

    def ffn_sides(layer, n_in, axis_in, n_out):
        return (_Side(w_ffn_in, layer, axis_in, n_in), _Side(w_ffn_out, layer, 0, n_out))

    assert DEPTH == 4 and N_MIXERS == 3, "the hosting plan below is written for gMLP, pool, attention, gMLP"
    gw = (a_w_in[0].astype(BF16), a_w_out[0].astype(BF16))
    x, *fw = _gmlp_call(x, 0, 0, mods, g_mix, gw[0], a_g_v, a_w_s, a_b_s, gw[1],
                        sides=ffn_sides(0, 2 * N_FT, 1, 2 * N_FT))
    x, *fw = _ffn_call(x, 0, mods, g_ffn, fw[0], ffn_conv_w, ffn_conv_b, fw[1], False,
                       sides=ffn_sides(1, N_FT, 1, N_FT))
    x, w_qkv, w_o, *fw_last = _pool_call(
        x, 1, 0, mods, g_mix, p_w, p_b, p_scale,
        sides=(_Side(c_w_qkv, 0, 1, 12), _Side(c_w_o, 0, 0, 8)) + ffn_sides(3, N_FT, 1, N_FT))
    x, *fw = _ffn_call(x, 1, mods, g_ffn, fw[0], ffn_conv_w, ffn_conv_b, fw[1], False,
                       sides=ffn_sides(2, N_FT, 1, N_FT))
    q, kx, vx, kn, v = _qkv_call(x, 2, 0, mods, g_mix, w_qkv, gq, gk, bd)
    new_k = kn.reshape(BATCH, 1, SEQ, N_KV_HEADS, HEAD_DIM)
    new_v = v.reshape(BATCH, 1, SEQ, N_KV_HEADS, HEAD_DIM)
    (x,) = _ctx_attn_call(x, 2, 0, mods, sink, q, kx, vx, w_o)
    (x,) = _smp_attn_call(x, 2, 0, mods, sink, q, kx, vx, ckx, cvx, w_o)
    x, *gw = _ffn_call(x, 2, mods, g_ffn, fw[0], ffn_conv_w, ffn_conv_b, fw[1], False,
                       sides=(_Side(a_w_in, 1, 1, 8), _Side(a_w_out, 1, 0, 8)))
    (x,) = _gmlp_call(x, 3, 1, mods, g_mix, gw[0], a_g_v, a_w_s, a_b_s, gw[1])
    x = _ffn_call(x, 3, mods, g_ffn, fw_last[0], ffn_conv_w, ffn_conv_b, fw_last[1], True)

    y_prompt = x[0].reshape(BATCH, SEQ, D_MODEL)
    y_sample = x[1].reshape(DEC_BATCH, DEC_SEQ, D_MODEL)
    return (y_prompt, y_sample, new_k, new_v)
```

```python
import functools
from typing import NamedTuple

import numpy as np
import jax
import jax.numpy as jnp
from jax import lax
from jax.experimental import pallas as pl
from jax.experimental.pallas import tpu as pltpu

D_MODEL = 1024
BATCH = 32
SEQ = 256
DEPTH = 4
DEC_BATCH = 2
DEC_SEQ = 2048
PAST_LEN = 256
GRID_W = 64
N_MIXERS = 3
CHUNK = 128
D_GMLP = 2 * D_MODEL
N_GROUPS_A = 8
D_GROUP_A = D_GMLP // N_GROUPS_A
POOL_WINDOWS = (2, 4, 8, 16)
N_POOL_GROUPS = 4
D_POOL_GROUP = D_MODEL // N_POOL_GROUPS
N_HEADS = 16
N_A_LAYERS = (DEPTH + 2) // N_MIXERS
N_B_LAYERS = (DEPTH + 1) // N_MIXERS
N_C_LAYERS = DEPTH // N_MIXERS
N_KV_HEADS = 4
HEAD_DIM = 64
Q_PER_KV = N_HEADS // N_KV_HEADS
WINDOW = 128
BLOCK = 128
ROPE_THETA = 10000.0
D_FF = 2816
CONV_W = 3
EPS = 1e-6
NEG_INF = -1e30
LOG2E = 1.4426950408889634

F32 = jnp.float32
BF16 = jnp.bfloat16

N_CTX_ROWS = BATCH * SEQ
N_SMP_ROWS = DEC_BATCH * DEC_SEQ
N_ROWS = N_CTX_ROWS + N_SMP_ROWS
N_COND = 1 + DEC_BATCH
N_MOD = 6
HALO = 16
SUBLANES = 8
LANES = 128
TN = 256
QKV_ROWS = 256
Q_DIM = N_HEADS * HEAD_DIM
KV_DIM = N_KV_HEADS * HEAD_DIM
KV_EXT = N_KV_HEADS * 2 * LANES
VMEM_CAP = 56 * 1024 * 1024
VMEM_TEMPS = 2 * 1024 * 1024


class _Rows:
    def __init__(self, tm):
        self.tm = tm
        self.nb = N_ROWS // tm
        self.ncb = N_CTX_ROWS // tm
        self.bps = DEC_SEQ // tm

    def cond_idx(self, i):
        return jnp.where(i < self.ncb, 0, 1 + (i - self.ncb) // self.bps)

    def seq_len(self, i):
        return jnp.where(i < self.ncb, SEQ, DEC_SEQ)

    def row_spec(self, width, offset=0):
        return pl.BlockSpec((self.tm, width), lambda i: (i + offset, 0))

    def ctx_rows_spec(self, width):
        return pl.BlockSpec((self.tm, width), lambda i: (jnp.minimum(i, self.ncb - 1), 0))

    def smp_rows_spec(self, width):
        return pl.BlockSpec((self.tm, width), lambda i: (jnp.maximum(i - self.ncb, 0), 0))

    def halo_specs(self, width, rows, offset=0):
        per = self.tm // rows
        last = N_ROWS // rows - 1
        prev = pl.BlockSpec((rows, width), lambda i: (jnp.maximum((i + offset) * per - 1, 0), 0))
        nxt = pl.BlockSpec((rows, width), lambda i: (jnp.minimum((i + offset + 1) * per, last), 0))
        return prev, nxt


ROWS_WIDE = _Rows(1024)
ROWS_NARROW = _Rows(512)


def _gelu(x):
    return 0.5 * x * (1.0 + jnp.tanh(np.sqrt(2.0 / np.pi) * (x + 0.044715 * (x * x * x))))


def _norm_mod(x, g, sc, sh):
    ms = jnp.mean(x * x, axis=-1, keepdims=True)
    return x * lax.rsqrt(ms + EPS) * (g * (1.0 + sc)) + sh


def _mod_row(mod_ref, cond, k):
    return mod_ref[pl.ds(cond, 1), k * D_MODEL:(k + 1) * D_MODEL]


def _dot(a, b):
    return jnp.dot(a, b, preferred_element_type=F32)


def _dot_nt(a, b):
    return lax.dot_general(a, b, (((1,), (1,)), ((), ())), preferred_element_type=F32)


def _padded_bytes(shape, dtype):
    item = jnp.dtype(dtype).itemsize
    dims = [1 if d is None else d for d in shape]
    dims[-1] = -(-dims[-1] // LANES) * LANES
    if len(dims) > 1:
        rows = SUBLANES * 4 // item
        dims[-2] = -(-dims[-2] // rows) * rows
    return item * int(np.prod(dims))


def _vmem_limit(specs, dtypes, scratch_shapes):
    total = 0
    for spec, dtype in zip(specs, dtypes):
        if spec.block_shape is not None:
            buffers = 2 if spec.pipeline_mode is None else spec.pipeline_mode.buffer_count
            total += buffers * _padded_bytes(spec.block_shape, dtype)
    total += sum(_padded_bytes(s.shape, s.dtype) for s in scratch_shapes)
    assert total + VMEM_TEMPS <= VMEM_CAP, f"declared VMEM buffers too large: {total} bytes"
    return VMEM_CAP


def _resident(shape, layer=None):
    nd = len(shape)
    if layer is None:
        return pl.BlockSpec(shape, lambda *_: (0,) * nd, pipeline_mode=pl.Buffered(1))
    return pl.BlockSpec((None,) + tuple(shape), lambda *_: (layer,) + (0,) * nd, pipeline_mode=pl.Buffered(1))


def _mod_spec(layer):
    return _resident((SUBLANES, N_MOD * D_MODEL), layer)


class _Side(NamedTuple):
    src: jax.Array
    layer: int
    axis: int
    n: int


def _side_plumbing(sides):
    in_specs, out_specs, out_shape = [], [], []
    for s in sides:
        _, r, c = s.src.shape
        blk = (r // s.n, c) if s.axis == 0 else (r, c // s.n)

        def idx(i, s=s):
            k = jnp.minimum(i, s.n - 1)
            return (k, 0) if s.axis == 0 else (0, k)

        in_specs.append(pl.BlockSpec((None,) + blk, lambda i, s=s, idx=idx: (s.layer,) + idx(i)))
        out_specs.append(pl.BlockSpec(blk, idx))
        out_shape.append(jax.ShapeDtypeStruct((r, c), BF16))
    return in_specs, out_specs, out_shape


def _with_sides(body, n_in, n_out, n_side):
    def kernel(*refs):
        ins, rest = refs[:n_in], refs[n_in:]
        side_in, rest = rest[:n_side], rest[n_side:]
        outs, rest = rest[:n_out], rest[n_out:]
        side_out, scratch = rest[:n_side], rest[n_side:]
        for src, dst in zip(side_in, side_out):
            dst[...] = src[...].astype(BF16)
        body(*ins, *outs, *scratch)
    return kernel


def _hosted_call(body, geo_steps, in_specs, out_specs, out_shape, scratch_shapes, sides, name, aliases=None):
    s_in, s_out, s_shape = _side_plumbing(sides)
    assert all(s.n <= geo_steps for s in sides)
    in_specs, out_specs, out_shape = list(in_specs) + s_in, list(out_specs) + s_out, list(out_shape) + s_shape

    def run(*operands):
        dtypes = [o.dtype for o in operands] + [o.dtype for o in out_shape]
        return pl.pallas_call(
            _with_sides(body, len(in_specs) - len(sides), len(out_specs) - len(sides), len(sides)),
            grid=(geo_steps,),
            in_specs=in_specs,
            out_specs=out_specs,
            out_shape=out_shape,
            input_output_aliases=aliases or {},
            scratch_shapes=scratch_shapes,
            compiler_params=pltpu.CompilerParams(
                dimension_semantics=("arbitrary",),
                vmem_limit_bytes=_vmem_limit(in_specs + out_specs, dtypes, scratch_shapes)),
            name=name,
        )(*operands)

    return run


ADA_TN = 1536


def _ada_kernel(c_ref, w_ref, b_ref, o_ref):
    c = c_ref[...]
    s = (c * (1.0 / (1.0 + jnp.exp(-c)))).astype(BF16)
    o_ref[...] = _dot(s, w_ref[...].astype(BF16)) + b_ref[...]


def _ada_call(cond8, w_ada, b_ada):
    nt = N_MOD * D_MODEL // ADA_TN
    return pl.pallas_call(
        _ada_kernel,
        grid=(DEPTH, nt),
        in_specs=[
            pl.BlockSpec((SUBLANES, D_MODEL), lambda l, j: (0, 0)),
            pl.BlockSpec((None, D_MODEL, ADA_TN), lambda l, j: (l, 0, j)),
            pl.BlockSpec((None, 1, ADA_TN), lambda l, j: (l, 0, j)),
        ],
        out_specs=pl.BlockSpec((None, SUBLANES, ADA_TN), lambda l, j: (l, 0, j)),
        out_shape=jax.ShapeDtypeStruct((DEPTH, SUBLANES, N_MOD * D_MODEL), F32),
        compiler_params=pltpu.CompilerParams(dimension_semantics=("arbitrary", "arbitrary"),
                                             vmem_limit_bytes=2 * _padded_bytes((D_MODEL, ADA_TN), F32) + VMEM_TEMPS),
        name="ada",
    )(cond8, w_ada, b_ada.reshape(DEPTH, 1, N_MOD * D_MODEL))


def _gmlp_kernel(geo, split_in, layer, a_layer, *refs):
    i = pl.program_id(0)
    tm = geo.tm
    if split_in:
        xc_ref, xs_ref, *refs = refs
        x = jnp.where(i < geo.ncb, xc_ref[...], xs_ref[...])
    else:
        x_ref, *refs = refs
        x = x_ref[...]
    g_ref, mod_ref, win_ref, gv_ref, ws_ref, bs_ref, wout_ref, o_ref, h_ref, u_ref, v_ref, p_ref = refs
    cond = geo.cond_idx(i)
    h_ref[...] = _norm_mod(x, g_ref[layer:layer + 1, :], _mod_row(mod_ref, cond, 1),
                           _mod_row(mod_ref, cond, 0)).astype(BF16)
    ssq = jnp.zeros((tm, LANES), F32)
    for j in range(D_GMLP // TN):
        sl = slice(j * TN, (j + 1) * TN)
        u_ref[:, sl] = _gelu(_dot(h_ref[...], win_ref[:, sl]))
        v = _gelu(_dot(h_ref[...], win_ref[:, D_GMLP + j * TN:D_GMLP + (j + 1) * TN]))
        v_ref[:, sl] = v
        for k in range(TN // LANES):
            vk = v[:, k * LANES:(k + 1) * LANES]
            ssq = ssq + vk * vk
    r = lax.rsqrt(jnp.sum(ssq, axis=-1, keepdims=True) * (1.0 / D_GMLP) + EPS)
    for g in range(N_GROUPS_A):
        sl = slice(g * D_GROUP_A, (g + 1) * D_GROUP_A)
        vn = (v_ref[:, sl] * r * gv_ref[a_layer:a_layer + 1, sl]).astype(BF16)
        for c in range(tm // CHUNK):
            rows = slice(c * CHUNK, (c + 1) * CHUNK)
            sv = _dot(ws_ref[g], vn[rows, :]) + bs_ref[g]
            p_ref[rows, sl] = (u_ref[rows, sl] * sv).astype(BF16)
    y = _dot(p_ref[...], wout_ref[...])
    o_ref[...] = x + _mod_row(mod_ref, cond, 2) * y


def _gmlp_call(x, layer, j, mods, g_mix, w_in, g_v, w_s, b_s, w_out, sides=()):
    split_in = isinstance(x, tuple)
    geo = ROWS_NARROW if split_in else ROWS_WIDE
    if split_in:
        x_specs = [geo.ctx_rows_spec(D_MODEL), geo.smp_rows_spec(D_MODEL)]
        xs = x
    else:
        x_specs = [geo.row_spec(D_MODEL)]
        xs = (x,)
    return _hosted_call(
        functools.partial(_gmlp_kernel, geo, split_in, layer, j), geo.nb,
        in_specs=x_specs + [
            _resident((DEPTH, D_MODEL)),
            _mod_spec(layer),
            _resident((D_MODEL, 2 * D_GMLP)),
            _resident((N_A_LAYERS, D_GMLP)),
            _resident((N_GROUPS_A, CHUNK, CHUNK), j),
            _resident((N_GROUPS_A, CHUNK, 1), j),
            _resident((D_GMLP, D_MODEL)),
        ],
        out_specs=[geo.row_spec(D_MODEL)],
        out_shape=[jax.ShapeDtypeStruct((N_ROWS, D_MODEL), F32)],
        scratch_shapes=[
            pltpu.VMEM((geo.tm, D_MODEL), BF16),
            pltpu.VMEM((geo.tm, D_GMLP), F32),
            pltpu.VMEM((geo.tm, D_GMLP), F32),
            pltpu.VMEM((geo.tm, D_GMLP), BF16),
        ],
        sides=sides, name="gmlp",
    )(*xs, g_mix, mods, w_in, g_v, w_s, b_s, w_out, *[s.src for s in sides])


def _pool_kernel(geo, layer, j, x_ref, xp_ref, xn_ref, g_ref, mod_ref, pw_ref, pb_ref, ps_ref, o_ref, hext_ref):
    i = pl.program_id(0)
    tm = geo.tm
    ext = tm + 2 * HALO
    cond = geo.cond_idx(i)
    g = g_ref[layer:layer + 1, :]
    sc = _mod_row(mod_ref, cond, 1)
    sh = _mod_row(mod_ref, cond, 0)
    gate = _mod_row(mod_ref, cond, 2)
    x = x_ref[...]
    hext_ref[0:tm, :] = _norm_mod(x, g, sc, sh)
    hext_ref[tm:tm + HALO, :] = _norm_mod(xn_ref[...], g, sc, sh)
    hext_ref[tm + HALO:ext, :] = _norm_mod(xp_ref[...], g, sc, sh)
    seq = geo.seq_len(i)
    e = lax.broadcasted_iota(jnp.int32, (ext, D_POOL_GROUP), 0)
    pos = (i * tm + jnp.where(e < tm + HALO, e, e - ext)) & (seq - 1)
    p = (i * tm + lax.broadcasted_iota(jnp.int32, (tm, D_POOL_GROUP), 0)) & (seq - 1)
    for grp in range(N_POOL_GROUPS):
        sl = slice(grp * D_POOL_GROUP, (grp + 1) * D_POOL_GROUP)
        half = POOL_WINDOWS[grp] // 2
        he = hext_ref[:, sl]
        fwd = he
        bwd = jnp.where(pos >= 1, pltpu.roll(he, 1, 0), 0.0)
        span = 1
        while span < half:
            fwd = fwd + jnp.where(pos < seq - span, pltpu.roll(fwd, ext - span, 0), 0.0)
            bwd = bwd + jnp.where(pos >= span, pltpu.roll(bwd, span, 0), 0.0)
            span *= 2
        cnt = (jnp.minimum(p + half - 1, seq - 1) - jnp.maximum(p - half, 0) + 1).astype(F32)
        d = (fwd[:tm, :] + bwd[:tm, :]) / cnt - he[:tm, :]
        y = (_dot(d.astype(BF16), pw_ref[grp]) + pb_ref[j:j + 1, sl]) * ps_ref[j:j + 1, sl]
        o_ref[:, sl] = x[:, sl] + gate[:, sl] * y


def _pool_call(x, layer, j, mods, g_mix, p_w, p_b, p_scale, sides=()):
    geo = ROWS_WIDE
    prev, nxt = geo.halo_specs(D_MODEL, HALO)
    return _hosted_call(
        functools.partial(_pool_kernel, geo, layer, j), geo.nb,
        in_specs=[
            geo.row_spec(D_MODEL), prev, nxt,
            _resident((DEPTH, D_MODEL)),
            _mod_spec(layer),
            _resident((N_POOL_GROUPS, D_POOL_GROUP, D_POOL_GROUP), j),
            _resident((N_B_LAYERS, D_MODEL)),
            _resident((N_B_LAYERS, D_MODEL)),
        ],
        out_specs=[geo.row_spec(D_MODEL)],
        out_shape=[jax.ShapeDtypeStruct((N_ROWS, D_MODEL), F32)],
        scratch_shapes=[pltpu.VMEM((geo.tm + 2 * HALO, D_MODEL), F32)],
        sides=sides, name="pool",
    )(x, x, x, g_mix, mods, p_w, p_b, p_scale, *[s.src for s in sides])


def _rope_tables(tm):
    n_freq = HEAD_DIM // 4
    inv = ROPE_THETA ** (-np.arange(n_freq, dtype=np.float64) / n_freq)
    t = np.arange(DEC_SEQ)
    lane = np.arange(LANES) % HEAD_DIM
    quarter, f = lane // n_freq, lane % n_freq
    posn = np.where(quarter[None, :] < 2, (t // GRID_W)[:, None], (t % GRID_W)[:, None])
    ang = posn * inv[f][None, :]
    cos, sin = np.cos(ang), np.sin(ang)
    odd = (quarter % 2 == 1)[None, :]
    tab = np.concatenate([cos, np.where(odd, sin, -sin)], axis=1)
    ident = np.concatenate([np.ones((tm, LANES)), np.zeros((tm, LANES))], axis=1)
    return np.concatenate([ident, tab], axis=0).astype(np.float32)


def _head_meansq(t, bd_ref):
    sq = t * t
    hi = sq.astype(BF16)
    lo = (sq - hi.astype(F32)).astype(BF16)
    return _dot(hi, bd_ref[...]) + _dot(lo, bd_ref[...])


def _rope(t, tab_ref):
    q = HEAD_DIM // 4
    second = (lax.broadcasted_iota(jnp.int32, (1, LANES), 1) & q) != 0
    partner = jnp.where(second, pltpu.roll(t, q, 1), pltpu.roll(t, LANES - q, 1))
    return t * tab_ref[:, 0:LANES] + partner * tab_ref[:, LANES:2 * LANES]


def _kv_ext(t, o_ref):
    lo = lax.broadcasted_iota(jnp.int32, (t.shape[0], LANES), 1) < HEAD_DIM
    for m in range(KV_DIM // LANES):
        a = t[:, m * LANES:(m + 1) * LANES]
        r = pltpu.roll(a, HEAD_DIM, 1)
        base = m * 4 * LANES
        o_ref[:, base:base + LANES] = jnp.where(lo, a, 0.0).astype(o_ref.dtype)
        o_ref[:, base + LANES:base + 2 * LANES] = jnp.where(lo, 0.0, r).astype(o_ref.dtype)
        o_ref[:, base + 2 * LANES:base + 3 * LANES] = jnp.where(lo, r, 0.0).astype(o_ref.dtype)
        o_ref[:, base + 3 * LANES:base + 4 * LANES] = jnp.where(lo, 0.0, a).astype(o_ref.dtype)


def _store_head_rows(o_ref, t):
    for h in range(N_KV_HEADS):
        o_ref[pl.ds(h, t.shape[0], stride=N_KV_HEADS), :] = t[:, h * HEAD_DIM:(h + 1) * HEAD_DIM]


def _qkv_kernel(geo, layer, j, x_ref, g_ref, mod_ref, w_ref, gq_ref, gk_ref, bd_ref, tab_ref,
                q_ref, kx_ref, vx_ref, kn_ref, v_ref, h_ref, kn_s, v_s):
    i = pl.program_id(0)
    cond = geo.cond_idx(i)
    h_ref[...] = _norm_mod(x_ref[...], g_ref[layer:layer + 1, :], _mod_row(mod_ref, cond, 1),
                           _mod_row(mod_ref, cond, 0)).astype(BF16)
    scale = HEAD_DIM ** -0.5 * LOG2E
    n_q = Q_DIM // TN
    for r0 in range(0, geo.tm, QKV_ROWS):
        rs = slice(r0, r0 + QKV_ROWS)
        tab = tab_ref.at[rs, :]
        hr = h_ref[rs, :]
        ts = [_dot(hr, w_ref[:, c * TN:(c + 1) * TN]) for c in range(n_q + 2)]
        for c in range(n_q):
            t = ts[c]
            t = t * lax.rsqrt(_head_meansq(t, bd_ref) + EPS) * (gq_ref[j:j + 1, :] * scale)
            for m in range(TN // LANES):
                q_ref[rs, c * TN + m * LANES:c * TN + (m + 1) * LANES] = _rope(
                    t[:, m * LANES:(m + 1) * LANES], tab).astype(BF16)
        t = ts[n_q]
        t = t * lax.rsqrt(_head_meansq(t, bd_ref) + EPS) * gk_ref[j:j + 1, :]
        kn_s[rs, :] = t
        kr = jnp.concatenate([_rope(t[:, m * LANES:(m + 1) * LANES], tab) for m in range(KV_DIM // LANES)], axis=1)
        _kv_ext(kr, kx_ref.at[rs, :])
        v = ts[n_q + 1]
        v_s[rs, :] = v
        _kv_ext(v, vx_ref.at[rs, :])

    @pl.when(i < geo.ncb)
    def _():
        _store_head_rows(kn_ref, kn_s[...])
        _store_head_rows(v_ref, v_s[...])


def _qkv_call(x, layer, j, mods, g_mix, w_qkv, g_q, g_k, bd):
    geo = ROWS_WIDE
    tab = jnp.asarray(_rope_tables(geo.tm))
    tab_spec = pl.BlockSpec((geo.tm, 2 * LANES),
                            lambda i: (jnp.where(i < geo.ncb, 0, 1 + (i - geo.ncb) % geo.bps), 0))
    cache_spec = pl.BlockSpec((geo.tm * N_KV_HEADS, HEAD_DIM), lambda i: (jnp.minimum(i, geo.ncb - 1), 0))
    return _hosted_call(
        functools.partial(_qkv_kernel, geo, layer, j), geo.nb,
        in_specs=[
            geo.row_spec(D_MODEL),
            _resident((DEPTH, D_MODEL)),
            _mod_spec(layer),
            _resident((D_MODEL, Q_DIM + 2 * KV_DIM)),
            _resident((N_C_LAYERS, TN)),
            _resident((N_C_LAYERS, TN)),
            _resident((TN, TN)),
            tab_spec,
        ],
        out_specs=[geo.row_spec(Q_DIM), geo.row_spec(KV_EXT), geo.row_spec(KV_EXT), cache_spec, cache_spec],
        out_shape=[
            jax.ShapeDtypeStruct((N_ROWS, Q_DIM), BF16),
            jax.ShapeDtypeStruct((N_ROWS, KV_EXT), BF16),
            jax.ShapeDtypeStruct((N_ROWS, KV_EXT), BF16),
            jax.ShapeDtypeStruct((N_CTX_ROWS * N_KV_HEADS, HEAD_DIM), F32),
            jax.ShapeDtypeStruct((N_CTX_ROWS * N_KV_HEADS, HEAD_DIM), F32),
        ],
        scratch_shapes=[pltpu.VMEM((geo.tm, D_MODEL), BF16)] + [pltpu.VMEM((geo.tm, KV_DIM), F32)] * 2,
        sides=(), name="qkv",
    )(x, g_mix, mods, w_qkv, g_q, g_k, bd, tab)


def _head_lanes(g, j):
    base = (g * 2 + j) * LANES
    return slice(base, base + LANES)


def _ctx_attn_kernel(geo, j, sink_ref, x_ref, mod_ref, q_ref, kx_ref, vx_ref, wo_ref, o_ref, a_ref):
    for s in range(geo.tm // SEQ):
        rows = slice(s * SEQ, (s + 1) * SEQ)
        for g in range(N_KV_HEADS):
            for pr in range(2):
                qp = q_ref[rows, _head_lanes(g, pr)]
                acc = jnp.zeros((SEQ, LANES), F32)
                for var in range(2):
                    sink = sink_ref[j * N_HEADS + g * Q_PER_KV + 2 * pr + var] * LOG2E
                    sc = _dot_nt(qp, kx_ref[rows, _head_lanes(g, var)])
                    m = jnp.maximum(jnp.max(sc, axis=-1, keepdims=True), sink)
                    e = jnp.exp2(sc - m)
                    den = jnp.sum(e, axis=-1, keepdims=True) + jnp.exp2(sink - m)
                    acc = acc + _dot(e.astype(BF16), vx_ref[rows, _head_lanes(g, var)]) / den
                a_ref[rows, _head_lanes(g, pr)] = acc.astype(BF16)
    y = _dot(a_ref[...], wo_ref[...])
    o_ref[...] = x_ref[...] + _mod_row(mod_ref, 0, 2) * y


def _ctx_attn_call(x, layer, j, mods, sink, q, kx, vx, w_o, sides=()):
    geo = ROWS_WIDE
    return _hosted_call(
        functools.partial(_ctx_attn_kernel, geo, j), geo.ncb,
        in_specs=[
            pl.BlockSpec(memory_space=pltpu.SMEM),
            geo.row_spec(D_MODEL),
            _mod_spec(layer),
            geo.row_spec(Q_DIM), geo.row_spec(KV_EXT), geo.row_spec(KV_EXT),
            _resident((Q_DIM, D_MODEL)),
        ],
        out_specs=[geo.row_spec(D_MODEL)],
        out_shape=[jax.ShapeDtypeStruct((N_ROWS, D_MODEL), F32)],
        scratch_shapes=[pltpu.VMEM((geo.tm, Q_DIM), BF16)],
        sides=sides, name="ctx_attn", aliases={1: 0},
    )(sink, x, mods, q, kx, vx, w_o, *[s.src for s in sides])


def _group_queries(q_ref, rows, g):
    return jnp.concatenate([q_ref[rows, _head_lanes(g, 0)], q_ref[rows, _head_lanes(g, 1)]], axis=0)


def _group_sinks(sink_ref, j, g, n):
    heads = (0, 2, 1, 3)
    return jnp.concatenate(
        [jnp.full((n, 1), sink_ref[j * N_HEADS + g * Q_PER_KV + h] * LOG2E, F32) for h in heads], axis=0)


def _lane_tiles(pieces):
    return [p[:, k * LANES:(k + 1) * LANES] for p in pieces for k in range(p.shape[1] // LANES)]


def _sink_softmax(pieces, sink_col):
    m = jnp.max(functools.reduce(jnp.maximum, _lane_tiles(pieces)), axis=-1, keepdims=True)
    m = jnp.maximum(m, sink_col)
    es = [jnp.exp2(s - m) for s in pieces]
    den = jnp.sum(functools.reduce(jnp.add, _lane_tiles(es)), axis=-1, keepdims=True) + jnp.exp2(sink_col - m)
    return es, 1.0 / den


def _store_group(a_ref, rows, g, pv, n):
    for pr in range(2):
        sub = slice(pr * n, (pr + 1) * n)
        a_ref[rows, _head_lanes(g, pr)] = (pv[0][sub, :] + pv[1][sub, :]).astype(BF16)


def _smp_attn_kernel(geo, j, sink_ref, x_ref, mod_ref, q_ref, kc_ref, kp_ref, kn_ref, vc_ref, vp_ref, vn_ref,
                     ckraw_ref, cvraw_ref, wo_ref, o_ref, a_ref, kall_ref, vall_ref, ck_ref, cv_ref):
    i = pl.program_id(0)
    tm = geo.tm
    n_qb = tm // BLOCK
    kwin = tm + 2 * BLOCK
    kall_ref[0:BLOCK, :] = kp_ref[...]
    kall_ref[BLOCK:BLOCK + tm, :] = kc_ref[...]
    kall_ref[BLOCK + tm:kwin, :] = kn_ref[...]
    vall_ref[0:BLOCK, :] = vp_ref[...]
    vall_ref[BLOCK:BLOCK + tm, :] = vc_ref[...]
    vall_ref[BLOCK + tm:kwin, :] = vn_ref[...]
    _kv_ext(ckraw_ref[...], ck_ref)
    _kv_ext(cvraw_ref[...], cv_ref)
    first = (i % geo.bps) == 0
    last = (i % geo.bps) == geo.bps - 1
    qi = lax.broadcasted_iota(jnp.int32, (Q_PER_KV * BLOCK, BLOCK), 0) & (BLOCK - 1)
    kj = lax.broadcasted_iota(jnp.int32, (Q_PER_KV * BLOCK, BLOCK), 1)

    def scores(b, g):
        rows = slice(b * BLOCK, (b + 1) * BLOCK)
        win = slice(b * BLOCK, (b + 3) * BLOCK)
        q2 = _group_queries(q_ref, rows, g)
        sw = jnp.concatenate([_dot_nt(q2, kall_ref[win, _head_lanes(g, var)]) for var in range(2)], axis=0)
        sc = jnp.concatenate([_dot_nt(q2, ck_ref[:, _head_lanes(g, var)]) for var in range(2)], axis=0)
        return sw, sc

    def probabilities(b, g, sw, sc):
        ok_prev = kj >= (qi + jnp.where(first, BLOCK, 0) if b == 0 else qi)
        ok_next = kj <= (qi - jnp.where(last, BLOCK, 0) if b == n_qb - 1 else qi)
        pieces = [jnp.where(ok_prev, sw[:, 0:BLOCK], NEG_INF), sw[:, BLOCK:2 * BLOCK],
                  jnp.where(ok_next, sw[:, 2 * BLOCK:3 * BLOCK], NEG_INF), sc]
        es, rinv = _sink_softmax(pieces, _group_sinks(sink_ref, j, g, BLOCK))
        return jnp.concatenate(es[:3], axis=-1).astype(BF16), es[3].astype(BF16), rinv

    def weighted_values(b, g, ew, ec, rinv):
        rows = slice(b * BLOCK, (b + 1) * BLOCK)
        win = slice(b * BLOCK, (b + 3) * BLOCK)
        pv = []
        for var in range(2):
            sub = slice(var * 2 * BLOCK, (var + 1) * 2 * BLOCK)
            lanes = _head_lanes(g, var)
            pv.append((_dot(ew[sub, :], vall_ref[win, lanes]) + _dot(ec[sub, :], cv_ref[:, lanes]))
                      * rinv[sub, :])
        _store_group(a_ref, rows, g, pv, BLOCK)

    todo = [(b, g) for b in range(n_qb) for g in range(N_KV_HEADS)]
    ahead = scores(*todo[0])
    behind = None
    for k, (b, g) in enumerate(todo):
        sw, sc = ahead
        if k + 1 < len(todo):
            ahead = scores(*todo[k + 1])
        probs = probabilities(b, g, sw, sc)
        if behind is not None:
            weighted_values(*behind)
        behind = (b, g) + probs
    weighted_values(*behind)
    y = _dot(a_ref[...], wo_ref[...])
    o_ref[...] = x_ref[...] + _mod_row(mod_ref, 1 + i // geo.bps, 2) * y


def _smp_attn_call(x, layer, j, mods, sink, q, kx, vx, cache_k, cache_v, w_o, sides=()):
    geo = ROWS_NARROW
    prev, nxt = geo.halo_specs(KV_EXT, BLOCK, offset=geo.ncb)
    ctx_spec = pl.BlockSpec((None, None, PAST_LEN, KV_DIM), lambda i: (i // geo.bps, j, 0, 0))
    kwin = geo.tm + 2 * BLOCK
    return _hosted_call(
        functools.partial(_smp_attn_kernel, geo, j), geo.nb - geo.ncb,
        in_specs=[
            pl.BlockSpec(memory_space=pltpu.SMEM),
            geo.row_spec(D_MODEL, geo.ncb),
            _mod_spec(layer),
            geo.row_spec(Q_DIM, geo.ncb),
            geo.row_spec(KV_EXT, geo.ncb), prev, nxt,
            geo.row_spec(KV_EXT, geo.ncb), prev, nxt,
            ctx_spec, ctx_spec,
            _resident((Q_DIM, D_MODEL)),
        ],
        out_specs=[geo.row_spec(D_MODEL, geo.ncb)],
        out_shape=[jax.ShapeDtypeStruct((N_ROWS, D_MODEL), F32)],
        scratch_shapes=[pltpu.VMEM((geo.tm, Q_DIM), BF16), pltpu.VMEM((kwin, KV_EXT), BF16),
                        pltpu.VMEM((kwin, KV_EXT), BF16), pltpu.VMEM((PAST_LEN, KV_EXT), BF16),
                        pltpu.VMEM((PAST_LEN, KV_EXT), BF16)],
        sides=sides, name="smp_attn", aliases={1: 0},
    )(sink, x, mods, q, kx, kx, kx, vx, vx, vx, cache_k, cache_v, w_o, *[s.src for s in sides])


N_FT = D_FF // TN


def _ffn_kernel(geo, split_out, layer, x_ref, xp_ref, xn_ref, g_ref, mod_ref, win_ref, cw_ref, cb_ref, wo_ref, *refs):
    i = pl.program_id(0)
    tm = geo.tm
    ext = tm + 2 * HALO
    hext_ref, gbuf_ref = refs[-2:]
    cond = geo.cond_idx(i)
    g = g_ref[layer:layer + 1, :]
    sh = _mod_row(mod_ref, cond, 3)
    sc = _mod_row(mod_ref, cond, 4)
    x = x_ref[...]
    hext_ref[0:tm, :] = _norm_mod(x, g, sc, sh).astype(BF16)
    hext_ref[tm:tm + HALO, :] = _norm_mod(xn_ref[...], g, sc, sh).astype(BF16)
    hext_ref[tm + HALO:ext, :] = _norm_mod(xp_ref[...], g, sc, sh).astype(BF16)
    seq = geo.seq_len(i)
    pos = (i * tm + lax.broadcasted_iota(jnp.int32, (tm, TN), 0)) & (seq - 1)
    has_prev = pos != 0
    has_next = pos != seq - 1
    for j in range(N_FT):
        sl = slice(j * TN, (j + 1) * TN)
        a = _dot(hext_ref[...], win_ref[:, sl])
        u = _dot(hext_ref[0:tm, :], win_ref[:, D_FF + j * TN:D_FF + (j + 1) * TN])
        a_prev = jnp.where(has_prev, pltpu.roll(a, 1, 0)[:tm, :], 0.0)
        a_next = jnp.where(has_next, pltpu.roll(a, ext - 1, 0)[:tm, :], 0.0)
        a = (a_prev * cw_ref[0:1, sl] + a[:tm, :] * cw_ref[1:2, sl] + a_next * cw_ref[2:3, sl]
             + cb_ref[layer:layer + 1, sl])
        gbuf_ref[:, sl] = (_gelu(a) * u).astype(BF16)
    y = _dot(gbuf_ref[...], wo_ref[...])
    res = x + _mod_row(mod_ref, cond, 5) * y
    if split_out:
        oc_ref, os_ref = refs[:2]

        @pl.when(i < geo.ncb)
        def _():
            oc_ref[...] = res

        @pl.when(i >= geo.ncb)
        def _():
            os_ref[...] = res
    else:
        refs[0][...] = res


def _ffn_call(x, layer, mods, g_ffn, w_in, conv_w, conv_b, w_out, split_out, sides=()):
    geo = ROWS_WIDE
    prev, nxt = geo.halo_specs(D_MODEL, HALO)
    if split_out:
        out_specs = [geo.ctx_rows_spec(D_MODEL), geo.smp_rows_spec(D_MODEL)]
        out_shape = [jax.ShapeDtypeStruct((N_CTX_ROWS, D_MODEL), F32),
                     jax.ShapeDtypeStruct((N_SMP_ROWS, D_MODEL), F32)]
    else:
        out_specs = [geo.row_spec(D_MODEL)]
        out_shape = [jax.ShapeDtypeStruct((N_ROWS, D_MODEL), F32)]
    return _hosted_call(
        functools.partial(_ffn_kernel, geo, split_out, layer), geo.nb,
        in_specs=[
            geo.row_spec(D_MODEL), prev, nxt,
            _resident((DEPTH, D_MODEL)),
            _mod_spec(layer),
            _resident((D_MODEL, 2 * D_FF)),
            _resident((CONV_W, D_FF), layer),
            _resident((DEPTH, D_FF)),
            _resident((D_FF, D_MODEL)),
        ],
        out_specs=out_specs,
        out_shape=out_shape,
        scratch_shapes=[pltpu.VMEM((geo.tm + 2 * HALO, D_MODEL), BF16), pltpu.VMEM((geo.tm, D_FF), BF16)],
        sides=sides, name="ffn",
    )(x, x, x, g_ffn, mods, w_in, conv_w, conv_b, w_out, *[s.src for s in sides])


def kernel(x_prompt, x_sample, cache_k, cache_v, c, c_ctx, w_ada, b_ada, g_mix, g_ffn, w_ffn_in, ffn_conv_w,
           ffn_conv_b, w_ffn_out, a_w_in, a_g_v, a_w_s, a_b_s, a_w_out, p_w, p_b, p_scale, c_w_qkv, c_g_q,
           c_g_k, c_sink, c_w_o):
    x = (x_prompt.reshape(N_CTX_ROWS, D_MODEL), x_sample.reshape(N_SMP_ROWS, D_MODEL))

    cond8 = jnp.concatenate([c_ctx[None, :], c, jnp.zeros((SUBLANES - N_COND, D_MODEL), F32)], axis=0)
    mods = _ada_call(cond8, w_ada, b_ada)

    a_w_s, a_b_s, p_w = a_w_s.astype(BF16), a_b_s[:, :, :, None], p_w.astype(BF16)
    gq = jnp.tile(c_g_q, (1, TN // HEAD_DIM))
    gk = jnp.tile(c_g_k, (1, TN // HEAD_DIM))
    bd = jnp.asarray(np.kron(np.eye(TN // HEAD_DIM), np.full((HEAD_DIM, HEAD_DIM), 1.0 / HEAD_DIM)), dtype=BF16)
    sink = c_sink.reshape(N_C_LAYERS * N_HEADS)
    cache_k = cache_k.reshape(DEC_BATCH, N_C_LAYERS, PAST_LEN, KV_DIM)
    cache_v = cache_v.reshape(DEC_BATCH, N_C_LAYERS, PAST_LEN, KV_DIM)

    def ffn_sides(layer, n_in, axis_in, n_out):
        return (_Side(w_ffn_in, layer, axis_in, n_in), _Side(w_ffn_out, layer, 0, n_out))

    assert DEPTH == 4 and N_MIXERS == 3, "the hosting plan below is written for gMLP, pool, attention, gMLP"
    gw = (a_w_in[0].astype(BF16), a_w_out[0].astype(BF16))
    x, *fw = _gmlp_call(x, 0, 0, mods, g_mix, gw[0], a_g_v, a_w_s, a_b_s, gw[1],
                        sides=ffn_sides(0, 2 * N_FT, 1, 2 * N_FT))
    x, *fw = _ffn_call(x, 0, mods, g_ffn, fw[0], ffn_conv_w, ffn_conv_b, fw[1], False,
                       sides=ffn_sides(1, N_FT, 1, N_FT))
    x, w_qkv, w_o, *fw_last = _pool_call(
        x, 1, 0, mods, g_mix, p_w, p_b, p_scale,
        sides=(_Side(c_w_qkv, 0, 1, 12), _Side(c_w_o, 0, 0, 8)) + ffn_sides(3, N_FT, 1, N_FT))
    x, *fw = _ffn_call(x, 1, mods, g_ffn, fw[0], ffn_conv_w, ffn_conv_b, fw[1], False,
                       sides=ffn_sides(2, N_FT, 1, N_FT))
    q, kx, vx, kn, v = _qkv_call(x, 2, 0, mods, g_mix, w_qkv, gq, gk, bd)
    new_k = kn.reshape(BATCH, 1, SEQ, N_KV_HEADS, HEAD_DIM)
    new_v = v.reshape(BATCH, 1, SEQ, N_KV_HEADS, HEAD_DIM)
    (x,) = _ctx_attn_call(x, 2, 0, mods, sink, q, kx, vx, w_o)
    (x,) = _smp_attn_call(x, 2, 0, mods, sink, q, kx, vx, cache_k, cache_v, w_o)
    x, *gw = _ffn_call(x, 2, mods, g_ffn, fw[0], ffn_conv_w, ffn_conv_b, fw[1], False,
                       sides=(_Side(a_w_in, 1, 1, 8), _Side(a_w_out, 1, 0, 8)))
    (x,) = _gmlp_call(x, 3, 1, mods, g_mix, gw[0], a_g_v, a_w_s, a_b_s, gw[1])
    x = _ffn_call(x, 3, mods, g_ffn, fw_last[0], ffn_conv_w, ffn_conv_b, fw_last[1], True)

    y_prompt = x[0].reshape(BATCH, SEQ, D_MODEL)
    y_sample = x[1].reshape(DEC_BATCH, DEC_SEQ, D_MODEL)
    return (y_prompt, y_sample, new_k, new_v)
```

```python
import functools
from typing import NamedTuple

import numpy as np
import jax
import jax.numpy as jnp
from jax import lax
from jax.experimental import pallas as pl
from jax.experimental.pallas import tpu as pltpu

D_MODEL = 1024
BATCH = 32
SEQ = 256
DEPTH = 4
DEC_BATCH = 2
DEC_SEQ = 2048
PAST_LEN = 256
GRID_W = 64
N_MIXERS = 3
CHUNK = 128
D_GMLP = 2 * D_MODEL
N_GROUPS_A = 8
D_GROUP_A = D_GMLP // N_GROUPS_A
POOL_WINDOWS = (2, 4, 8, 16)
N_POOL_GROUPS = 4
D_POOL_GROUP = D_MODEL // N_POOL_GROUPS
N_HEADS = 16
N_A_LAYERS = (DEPTH + 2) // N_MIXERS
N_B_LAYERS = (DEPTH + 1) // N_MIXERS
N_C_LAYERS = DEPTH // N_MIXERS
N_KV_HEADS = 4
HEAD_DIM = 64
Q_PER_KV = N_HEADS // N_KV_HEADS
WINDOW = 128
BLOCK = 128
ROPE_THETA = 10000.0
D_FF = 2816
CONV_W = 3
EPS = 1e-6
NEG_INF = -1e30
LOG2E = 1.4426950408889634

F32 = jnp.float32
BF16 = jnp.bfloat16

N_CTX_ROWS = BATCH * SEQ
N_SMP_ROWS = DEC_BATCH * DEC_SEQ
N_ROWS = N_CTX_ROWS + N_SMP_ROWS
N_COND = 1 + DEC_BATCH
N_MOD = 6
HALO = 16
SUBLANES = 8
LANES = 128
TN = 256
QKV_ROWS = 256
Q_DIM = N_HEADS * HEAD_DIM
KV_DIM = N_KV_HEADS * HEAD_DIM
KV_EXT = N_KV_HEADS * 2 * LANES
VMEM_CAP = 56 * 1024 * 1024
VMEM_TEMPS = 2 * 1024 * 1024


class _Rows:
    def __init__(self, tm):
        self.tm = tm
        self.nb = N_ROWS // tm
        self.ncb = N_CTX_ROWS // tm
        self.bps = DEC_SEQ // tm

    def cond_idx(self, i):
        return jnp.where(i < self.ncb, 0, 1 + (i - self.ncb) // self.bps)

    def seq_len(self, i):
        return jnp.where(i < self.ncb, SEQ, DEC_SEQ)

    def row_spec(self, width, offset=0):
        return pl.BlockSpec((self.tm, width), lambda i: (i + offset, 0))

    def ctx_rows_spec(self, width):
        return pl.BlockSpec((self.tm, width), lambda i: (jnp.minimum(i, self.ncb - 1), 0))

    def smp_rows_spec(self, width):
        return pl.BlockSpec((self.tm, width), lambda i: (jnp.maximum(i - self.ncb, 0), 0))

    def halo_specs(self, width, rows, offset=0):
        per = self.tm // rows
        last = N_ROWS // rows - 1
        prev = pl.BlockSpec((rows, width), lambda i: (jnp.maximum((i + offset) * per - 1, 0), 0))
        nxt = pl.BlockSpec((rows, width), lambda i: (jnp.minimum((i + offset + 1) * per, last), 0))
        return prev, nxt


ROWS_WIDE = _Rows(1024)
ROWS_NARROW = _Rows(512)


def _gelu(x):
    return 0.5 * x * (1.0 + jnp.tanh(np.sqrt(2.0 / np.pi) * (x + 0.044715 * (x * x * x))))


def _norm_mod(x, g, sc, sh):
    ms = jnp.mean(x * x, axis=-1, keepdims=True)
    return x * lax.rsqrt(ms + EPS) * (g * (1.0 + sc)) + sh


def _mod_row(mod_ref, cond, k):
    return mod_ref[pl.ds(cond, 1), k * D_MODEL:(k + 1) * D_MODEL]


def _dot(a, b):
    return jnp.dot(a, b, preferred_element_type=F32)


def _dot_nt(a, b):
    return lax.dot_general(a, b, (((1,), (1,)), ((), ())), preferred_element_type=F32)


def _padded_bytes(shape, dtype):
    item = jnp.dtype(dtype).itemsize
    dims = [1 if d is None else d for d in shape]
    dims[-1] = -(-dims[-1] // LANES) * LANES
    if len(dims) > 1:
        rows = SUBLANES * 4 // item
        dims[-2] = -(-dims[-2] // rows) * rows
    return item * int(np.prod(dims))


def _vmem_limit(specs, dtypes, scratch_shapes):
    total = 0
    for spec, dtype in zip(specs, dtypes):
        if spec.block_shape is not None:
            buffers = 2 if spec.pipeline_mode is None else spec.pipeline_mode.buffer_count
            total += buffers * _padded_bytes(spec.block_shape, dtype)
    total += sum(_padded_bytes(s.shape, s.dtype) for s in scratch_shapes)
    assert total + VMEM_TEMPS <= VMEM_CAP, f"declared VMEM buffers too large: {total} bytes"
    return VMEM_CAP


def _resident(shape, layer=None):
    nd = len(shape)
    if layer is None:
        return pl.BlockSpec(shape, lambda *_: (0,) * nd, pipeline_mode=pl.Buffered(1))
    return pl.BlockSpec((None,) + tuple(shape), lambda *_: (layer,) + (0,) * nd, pipeline_mode=pl.Buffered(1))


def _mod_spec(layer):
    return _resident((SUBLANES, N_MOD * D_MODEL), layer)


class _Side(NamedTuple):
    src: jax.Array
    layer: int
    axis: int
    n: int


def _side_plumbing(sides):
    in_specs, out_specs, out_shape = [], [], []
    for s in sides:
        _, r, c = s.src.shape
        blk = (r // s.n, c) if s.axis == 0 else (r, c // s.n)

        def idx(i, s=s):
            k = jnp.minimum(i, s.n - 1)
            return (k, 0) if s.axis == 0 else (0, k)

        in_specs.append(pl.BlockSpec((None,) + blk, lambda i, s=s, idx=idx: (s.layer,) + idx(i)))
        out_specs.append(pl.BlockSpec(blk, idx))
        out_shape.append(jax.ShapeDtypeStruct((r, c), BF16))
    return in_specs, out_specs, out_shape


def _with_sides(body, n_in, n_out, n_side):
    def kernel(*refs):
        ins, rest = refs[:n_in], refs[n_in:]
        side_in, rest = rest[:n_side], rest[n_side:]
        outs, rest = rest[:n_out], rest[n_out:]
        side_out, scratch = rest[:n_side], rest[n_side:]
        for src, dst in zip(side_in, side_out):
            dst[...] = src[...].astype(BF16)
        body(*ins, *outs, *scratch)
    return kernel


def _hosted_call(body, geo_steps, in_specs, out_specs, out_shape, scratch_shapes, sides, name, aliases=None):
    s_in, s_out, s_shape = _side_plumbing(sides)
    assert all(s.n <= geo_steps for s in sides)
    in_specs, out_specs, out_shape = list(in_specs) + s_in, list(out_specs) + s_out, list(out_shape) + s_shape

    def run(*operands):
        dtypes = [o.dtype for o in operands] + [o.dtype for o in out_shape]
        return pl.pallas_call(
            _with_sides(body, len(in_specs) - len(sides), len(out_specs) - len(sides), len(sides)),
            grid=(geo_steps,),
            in_specs=in_specs,
            out_specs=out_specs,
            out_shape=out_shape,
            input_output_aliases=aliases or {},
            scratch_shapes=scratch_shapes,
            compiler_params=pltpu.CompilerParams(
                dimension_semantics=("arbitrary",),
                vmem_limit_bytes=_vmem_limit(in_specs + out_specs, dtypes, scratch_shapes)),
            name=name,
        )(*operands)

    return run


ADA_TN = 1536
ADA_TILES = N_MOD * D_MODEL // ADA_TN
ADA_BUFFERS = 3


def _ada_kernel(c_ref, w_hbm, b_ref, o_ref, buf_ref, sem_ref):
    n = DEPTH * ADA_TILES

    def tile_copy(k):
        layer, j = divmod(k, ADA_TILES)
        slot = k % ADA_BUFFERS
        return pltpu.make_async_copy(w_hbm.at[layer, :, pl.ds(j * ADA_TN, ADA_TN)], buf_ref.at[slot],
                                     sem_ref.at[slot])

    for k in range(min(ADA_BUFFERS, n)):
        tile_copy(k).start()
    c = c_ref[...]
    s = (c * (1.0 / (1.0 + jnp.exp(-c)))).astype(BF16)
    for k in range(n):
        layer, j = divmod(k, ADA_TILES)
        cols = slice(j * ADA_TN, (j + 1) * ADA_TN)
        tile_copy(k).wait()
        o_ref[layer, :, cols] = _dot(s, buf_ref[k % ADA_BUFFERS].astype(BF16)) + b_ref[layer, :, cols]
        if k + ADA_BUFFERS < n:
            tile_copy(k + ADA_BUFFERS).start()


def _ada_call(cond8, w_ada, b_ada):
    return pl.pallas_call(
        _ada_kernel,
        in_specs=[
            pl.BlockSpec(memory_space=pltpu.VMEM),
            pl.BlockSpec(memory_space=pl.ANY),
            pl.BlockSpec(memory_space=pltpu.VMEM),
        ],
        out_specs=pl.BlockSpec(memory_space=pltpu.VMEM),
        out_shape=jax.ShapeDtypeStruct((DEPTH, SUBLANES, N_MOD * D_MODEL), F32),
        scratch_shapes=[pltpu.VMEM((ADA_BUFFERS, D_MODEL, ADA_TN), F32), pltpu.SemaphoreType.DMA((ADA_BUFFERS,))],
        compiler_params=pltpu.CompilerParams(
            vmem_limit_bytes=(ADA_BUFFERS + 1) * _padded_bytes((D_MODEL, ADA_TN), F32) + 2 * VMEM_TEMPS),
        name="ada",
    )(cond8, w_ada, b_ada.reshape(DEPTH, 1, N_MOD * D_MODEL))


def _gmlp_kernel(geo, split_in, layer, a_layer, *refs):
    i = pl.program_id(0)
    tm = geo.tm
    if split_in:
        xc_ref, xs_ref, *refs = refs
        x = jnp.where(i < geo.ncb, xc_ref[...], xs_ref[...])
    else:
        x_ref, *refs = refs
        x = x_ref[...]
    g_ref, mod_ref, win_ref, gv_ref, ws_ref, bs_ref, wout_ref, o_ref, h_ref, u_ref, v_ref, p_ref = refs
    cond = geo.cond_idx(i)
    h_ref[...] = _norm_mod(x, g_ref[layer:layer + 1, :], _mod_row(mod_ref, cond, 1),
                           _mod_row(mod_ref, cond, 0)).astype(BF16)
    ssq = jnp.zeros((tm, LANES), F32)
    for j in range(D_GMLP // TN):
        sl = slice(j * TN, (j + 1) * TN)
        u_ref[:, sl] = _gelu(_dot(h_ref[...], win_ref[:, sl]))
        v = _gelu(_dot(h_ref[...], win_ref[:, D_GMLP + j * TN:D_GMLP + (j + 1) * TN]))
        v_ref[:, sl] = v
        for k in range(TN // LANES):
            vk = v[:, k * LANES:(k + 1) * LANES]
            ssq = ssq + vk * vk
    r = lax.rsqrt(jnp.sum(ssq, axis=-1, keepdims=True) * (1.0 / D_GMLP) + EPS)
    for g in range(N_GROUPS_A):
        sl = slice(g * D_GROUP_A, (g + 1) * D_GROUP_A)
        vn = (v_ref[:, sl] * r * gv_ref[a_layer:a_layer + 1, sl]).astype(BF16)
        for c in range(tm // CHUNK):
            rows = slice(c * CHUNK, (c + 1) * CHUNK)
            sv = _dot(ws_ref[g], vn[rows, :]) + bs_ref[g]
            p_ref[rows, sl] = (u_ref[rows, sl] * sv).astype(BF16)
    y = _dot(p_ref[...], wout_ref[...])
    o_ref[...] = x + _mod_row(mod_ref, cond, 2) * y


def _gmlp_call(x, layer, j, mods, g_mix, w_in, g_v, w_s, b_s, w_out, sides=()):
    split_in = isinstance(x, tuple)
    geo = ROWS_NARROW if split_in else ROWS_WIDE
    if split_in:
        x_specs = [geo.ctx_rows_spec(D_MODEL), geo.smp_rows_spec(D_MODEL)]
        xs = x
    else:
        x_specs = [geo.row_spec(D_MODEL)]
        xs = (x,)
    return _hosted_call(
        functools.partial(_gmlp_kernel, geo, split_in, layer, j), geo.nb,
        in_specs=x_specs + [
            _resident((DEPTH, D_MODEL)),
            _mod_spec(layer),
            _resident((D_MODEL, 2 * D_GMLP)),
            _resident((N_A_LAYERS, D_GMLP)),
            _resident((N_GROUPS_A, CHUNK, CHUNK), j),
            _resident((N_GROUPS_A, CHUNK, 1), j),
            _resident((D_GMLP, D_MODEL)),
        ],
        out_specs=[geo.row_spec(D_MODEL)],
        out_shape=[jax.ShapeDtypeStruct((N_ROWS, D_MODEL), F32)],
        scratch_shapes=[
            pltpu.VMEM((geo.tm, D_MODEL), BF16),
            pltpu.VMEM((geo.tm, D_GMLP), F32),
            pltpu.VMEM((geo.tm, D_GMLP), F32),
            pltpu.VMEM((geo.tm, D_GMLP), BF16),
        ],
        sides=sides, name="gmlp",
    )(*xs, g_mix, mods, w_in, g_v, w_s, b_s, w_out, *[s.src for s in sides])


def _pool_kernel(geo, layer, j, x_ref, xp_ref, xn_ref, g_ref, mod_ref, pw_ref, pb_ref, ps_ref, o_ref, hext_ref):
    i = pl.program_id(0)
    tm = geo.tm
    ext = tm + 2 * HALO
    cond = geo.cond_idx(i)
    g = g_ref[layer:layer + 1, :]
    sc = _mod_row(mod_ref, cond, 1)
    sh = _mod_row(mod_ref, cond, 0)
    gate = _mod_row(mod_ref, cond, 2)
    x = x_ref[...]
    hext_ref[0:tm, :] = _norm_mod(x, g, sc, sh)
    hext_ref[tm:tm + HALO, :] = _norm_mod(xn_ref[...], g, sc, sh)
    hext_ref[tm + HALO:ext, :] = _norm_mod(xp_ref[...], g, sc, sh)
    seq = geo.seq_len(i)
    e = lax.broadcasted_iota(jnp.int32, (ext, D_POOL_GROUP), 0)
    pos = (i * tm + jnp.where(e < tm + HALO, e, e - ext)) & (seq - 1)
    p = (i * tm + lax.broadcasted_iota(jnp.int32, (tm, D_POOL_GROUP), 0)) & (seq - 1)
    for grp in range(N_POOL_GROUPS):
        sl = slice(grp * D_POOL_GROUP, (grp + 1) * D_POOL_GROUP)
        half = POOL_WINDOWS[grp] // 2
        he = hext_ref[:, sl]
        fwd = he
        bwd = jnp.where(pos >= 1, pltpu.roll(he, 1, 0), 0.0)
        span = 1
        while span < half:
            fwd = fwd + jnp.where(pos < seq - span, pltpu.roll(fwd, ext - span, 0), 0.0)
            bwd = bwd + jnp.where(pos >= span, pltpu.roll(bwd, span, 0), 0.0)
            span *= 2
        cnt = (jnp.minimum(p + half - 1, seq - 1) - jnp.maximum(p - half, 0) + 1).astype(F32)
        d = (fwd[:tm, :] + bwd[:tm, :]) / cnt - he[:tm, :]
        y = (_dot(d.astype(BF16), pw_ref[grp]) + pb_ref[j:j + 1, sl]) * ps_ref[j:j + 1, sl]
        o_ref[:, sl] = x[:, sl] + gate[:, sl] * y


def _pool_call(x, layer, j, mods, g_mix, p_w, p_b, p_scale, sides=()):
    geo = ROWS_WIDE
    prev, nxt = geo.halo_specs(D_MODEL, HALO)
    return _hosted_call(
        functools.partial(_pool_kernel, geo, layer, j), geo.nb,
        in_specs=[
            geo.row_spec(D_MODEL), prev, nxt,
            _resident((DEPTH, D_MODEL)),
            _mod_spec(layer),
            _resident((N_POOL_GROUPS, D_POOL_GROUP, D_POOL_GROUP), j),
            _resident((N_B_LAYERS, D_MODEL)),
            _resident((N_B_LAYERS, D_MODEL)),
        ],
        out_specs=[geo.row_spec(D_MODEL)],
        out_shape=[jax.ShapeDtypeStruct((N_ROWS, D_MODEL), F32)],
        scratch_shapes=[pltpu.VMEM((geo.tm + 2 * HALO, D_MODEL), F32)],
        sides=sides, name="pool",
    )(x, x, x, g_mix, mods, p_w, p_b, p_scale, *[s.src for s in sides])


def _rope_tables(tm):
    n_freq = HEAD_DIM // 4
    inv = ROPE_THETA ** (-np.arange(n_freq, dtype=np.float64) / n_freq)
    t = np.arange(DEC_SEQ)
    lane = np.arange(LANES) % HEAD_DIM
    quarter, f = lane // n_freq, lane % n_freq
    posn = np.where(quarter[None, :] < 2, (t // GRID_W)[:, None], (t % GRID_W)[:, None])
    ang = posn * inv[f][None, :]
    cos, sin = np.cos(ang), np.sin(ang)
    odd = (quarter % 2 == 1)[None, :]
    tab = np.concatenate([cos, np.where(odd, sin, -sin)], axis=1)
    ident = np.concatenate([np.ones((tm, LANES)), np.zeros((tm, LANES))], axis=1)
    return np.concatenate([ident, tab], axis=0).astype(np.float32)


def _head_meansq(t, bd_ref):
    sq = t * t
    hi = sq.astype(BF16)
    lo = (sq - hi.astype(F32)).astype(BF16)
    return _dot(hi, bd_ref[...]) + _dot(lo, bd_ref[...])


def _rope(t, tab_ref):
    q = HEAD_DIM // 4
    second = (lax.broadcasted_iota(jnp.int32, (1, LANES), 1) & q) != 0
    partner = jnp.where(second, pltpu.roll(t, q, 1), pltpu.roll(t, LANES - q, 1))
    return t * tab_ref[:, 0:LANES] + partner * tab_ref[:, LANES:2 * LANES]


def _kv_ext(t, o_ref):
    lo = lax.broadcasted_iota(jnp.int32, (t.shape[0], LANES), 1) < HEAD_DIM
    for m in range(KV_DIM // LANES):
        a = t[:, m * LANES:(m + 1) * LANES]
        r = pltpu.roll(a, HEAD_DIM, 1)
        base = m * 4 * LANES
        o_ref[:, base:base + LANES] = jnp.where(lo, a, 0.0).astype(o_ref.dtype)
        o_ref[:, base + LANES:base + 2 * LANES] = jnp.where(lo, 0.0, r).astype(o_ref.dtype)
        o_ref[:, base + 2 * LANES:base + 3 * LANES] = jnp.where(lo, r, 0.0).astype(o_ref.dtype)
        o_ref[:, base + 3 * LANES:base + 4 * LANES] = jnp.where(lo, 0.0, a).astype(o_ref.dtype)


def _store_head_rows(o_ref, t):
    for h in range(N_KV_HEADS):
        o_ref[pl.ds(h, t.shape[0], stride=N_KV_HEADS), :] = t[:, h * HEAD_DIM:(h + 1) * HEAD_DIM]


def _qkv_kernel(geo, layer, j, x_ref, g_ref, mod_ref, w_ref, gq_ref, gk_ref, bd_ref, tab_ref,
                q_ref, kx_ref, vx_ref, kn_ref, v_ref, h_ref, kn_s, v_s):
    i = pl.program_id(0)
    cond = geo.cond_idx(i)
    h_ref[...] = _norm_mod(x_ref[...], g_ref[layer:layer + 1, :], _mod_row(mod_ref, cond, 1),
                           _mod_row(mod_ref, cond, 0)).astype(BF16)
    scale = HEAD_DIM ** -0.5 * LOG2E
    n_q = Q_DIM // TN
    for r0 in range(0, geo.tm, QKV_ROWS):
        rs = slice(r0, r0 + QKV_ROWS)
        tab = tab_ref.at[rs, :]
        hr = h_ref[rs, :]
        ts = [_dot(hr, w_ref[:, c * TN:(c + 1) * TN]) for c in range(n_q + 2)]
        for c in range(n_q):
            t = ts[c]
            t = t * lax.rsqrt(_head_meansq(t, bd_ref) + EPS) * (gq_ref[j:j + 1, :] * scale)
            for m in range(TN // LANES):
                q_ref[rs, c * TN + m * LANES:c * TN + (m + 1) * LANES] = _rope(
                    t[:, m * LANES:(m + 1) * LANES], tab).astype(BF16)
        t = ts[n_q]
        t = t * lax.rsqrt(_head_meansq(t, bd_ref) + EPS) * gk_ref[j:j + 1, :]
        kn_s[rs, :] = t
        kr = jnp.concatenate([_rope(t[:, m * LANES:(m + 1) * LANES], tab) for m in range(KV_DIM // LANES)], axis=1)
        _kv_ext(kr, kx_ref.at[rs, :])
        v = ts[n_q + 1]
        v_s[rs, :] = v
        _kv_ext(v, vx_ref.at[rs, :])

    @pl.when(i < geo.ncb)
    def _():
        _store_head_rows(kn_ref, kn_s[...])
        _store_head_rows(v_ref, v_s[...])


def _qkv_call(x, layer, j, mods, g_mix, w_qkv, g_q, g_k, bd):
    geo = ROWS_WIDE
    tab = jnp.asarray(_rope_tables(geo.tm))
    tab_spec = pl.BlockSpec((geo.tm, 2 * LANES),
                            lambda i: (jnp.where(i < geo.ncb, 0, 1 + (i - geo.ncb) % geo.bps), 0))
    cache_spec = pl.BlockSpec((geo.tm * N_KV_HEADS, HEAD_DIM), lambda i: (jnp.minimum(i, geo.ncb - 1), 0))
    return _hosted_call(
        functools.partial(_qkv_kernel, geo, layer, j), geo.nb,
        in_specs=[
            geo.row_spec(D_MODEL),
            _resident((DEPTH, D_MODEL)),
            _mod_spec(layer),
            _resident((D_MODEL, Q_DIM + 2 * KV_DIM)),
            _resident((N_C_LAYERS, TN)),
            _resident((N_C_LAYERS, TN)),
            _resident((TN, TN)),
            tab_spec,
        ],
        out_specs=[geo.row_spec(Q_DIM), geo.row_spec(KV_EXT), geo.row_spec(KV_EXT), cache_spec, cache_spec],
        out_shape=[
            jax.ShapeDtypeStruct((N_ROWS, Q_DIM), BF16),
            jax.ShapeDtypeStruct((N_ROWS, KV_EXT), BF16),
            jax.ShapeDtypeStruct((N_ROWS, KV_EXT), BF16),
            jax.ShapeDtypeStruct((N_CTX_ROWS * N_KV_HEADS, HEAD_DIM), F32),
            jax.ShapeDtypeStruct((N_CTX_ROWS * N_KV_HEADS, HEAD_DIM), F32),
        ],
        scratch_shapes=[pltpu.VMEM((geo.tm, D_MODEL), BF16)] + [pltpu.VMEM((geo.tm, KV_DIM), F32)] * 2,
        sides=(), name="qkv",
    )(x, g_mix, mods, w_qkv, g_q, g_k, bd, tab)


def _head_lanes(g, j):
    base = (g * 2 + j) * LANES
    return slice(base, base + LANES)


def _ctx_attn_kernel(geo, j, sink_ref, x_ref, mod_ref, q_ref, kx_ref, vx_ref, wo_ref, o_ref, a_ref):
    for s in range(geo.tm // SEQ):
        rows = slice(s * SEQ, (s + 1) * SEQ)
        for g in range(N_KV_HEADS):
            for pr in range(2):
                qp = q_ref[rows, _head_lanes(g, pr)]
                acc = jnp.zeros((SEQ, LANES), F32)
                for var in range(2):
                    sink = sink_ref[j * N_HEADS + g * Q_PER_KV + 2 * pr + var] * LOG2E
                    sc = _dot_nt(qp, kx_ref[rows, _head_lanes(g, var)])
                    m = jnp.maximum(jnp.max(sc, axis=-1, keepdims=True), sink)
                    e = jnp.exp2(sc - m)
                    den = jnp.sum(e, axis=-1, keepdims=True) + jnp.exp2(sink - m)
                    acc = acc + _dot(e.astype(BF16), vx_ref[rows, _head_lanes(g, var)]) / den
                a_ref[rows, _head_lanes(g, pr)] = acc.astype(BF16)
    y = _dot(a_ref[...], wo_ref[...])
    o_ref[...] = x_ref[...] + _mod_row(mod_ref, 0, 2) * y


def _ctx_attn_call(x, layer, j, mods, sink, q, kx, vx, w_o, sides=()):
    geo = ROWS_WIDE
    return _hosted_call(
        functools.partial(_ctx_attn_kernel, geo, j), geo.ncb,
        in_specs=[
            pl.BlockSpec(memory_space=pltpu.SMEM),
            geo.row_spec(D_MODEL),
            _mod_spec(layer),
            geo.row_spec(Q_DIM), geo.row_spec(KV_EXT), geo.row_spec(KV_EXT),
            _resident((Q_DIM, D_MODEL)),
        ],
        out_specs=[geo.row_spec(D_MODEL)],
        out_shape=[jax.ShapeDtypeStruct((N_ROWS, D_MODEL), F32)],
        scratch_shapes=[pltpu.VMEM((geo.tm, Q_DIM), BF16)],
        sides=sides, name="ctx_attn", aliases={1: 0},
    )(sink, x, mods, q, kx, vx, w_o, *[s.src for s in sides])


def _group_queries(q_ref, rows, g):
    return jnp.concatenate([q_ref[rows, _head_lanes(g, 0)], q_ref[rows, _head_lanes(g, 1)]], axis=0)


def _group_sinks(sink_ref, j, g, n):
    heads = (0, 2, 1, 3)
    return jnp.concatenate(
        [jnp.full((n, 1), sink_ref[j * N_HEADS + g * Q_PER_KV + h] * LOG2E, F32) for h in heads], axis=0)


def _lane_tiles(pieces):
    return [p[:, k * LANES:(k + 1) * LANES] for p in pieces for k in range(p.shape[1] // LANES)]


def _sink_softmax(pieces, sink_col):
    m = jnp.max(functools.reduce(jnp.maximum, _lane_tiles(pieces)), axis=-1, keepdims=True)
    m = jnp.maximum(m, sink_col)
    es = [jnp.exp2(s - m) for s in pieces]
    den = jnp.sum(functools.reduce(jnp.add, _lane_tiles(es)), axis=-1, keepdims=True) + jnp.exp2(sink_col - m)
    return es, 1.0 / den


def _store_group(a_ref, rows, g, pv, n):
    for pr in range(2):
        sub = slice(pr * n, (pr + 1) * n)
        a_ref[rows, _head_lanes(g, pr)] = (pv[0][sub, :] + pv[1][sub, :]).astype(BF16)


def _smp_attn_kernel(geo, j, sink_ref, x_ref, mod_ref, q_ref, kc_ref, kp_ref, kn_ref, vc_ref, vp_ref, vn_ref,
                     ckraw_ref, cvraw_ref, wo_ref, o_ref, a_ref, kall_ref, vall_ref, ck_ref, cv_ref):
    i = pl.program_id(0)
    tm = geo.tm
    n_qb = tm // BLOCK
    kwin = tm + 2 * BLOCK
    kall_ref[0:BLOCK, :] = kp_ref[...]
    kall_ref[BLOCK:BLOCK + tm, :] = kc_ref[...]
    kall_ref[BLOCK + tm:kwin, :] = kn_ref[...]
    vall_ref[0:BLOCK, :] = vp_ref[...]
    vall_ref[BLOCK:BLOCK + tm, :] = vc_ref[...]
    vall_ref[BLOCK + tm:kwin, :] = vn_ref[...]
    _kv_ext(ckraw_ref[...], ck_ref)
    _kv_ext(cvraw_ref[...], cv_ref)
    first = (i % geo.bps) == 0
    last = (i % geo.bps) == geo.bps - 1
    qi = lax.broadcasted_iota(jnp.int32, (Q_PER_KV * BLOCK, BLOCK), 0) & (BLOCK - 1)
    kj = lax.broadcasted_iota(jnp.int32, (Q_PER_KV * BLOCK, BLOCK), 1)

    def scores(b, g):
        rows = slice(b * BLOCK, (b + 1) * BLOCK)
        win = slice(b * BLOCK, (b + 3) * BLOCK)
        q2 = _group_queries(q_ref, rows, g)
        sw = jnp.concatenate([_dot_nt(q2, kall_ref[win, _head_lanes(g, var)]) for var in range(2)], axis=0)
        sc = jnp.concatenate([_dot_nt(q2, ck_ref[:, _head_lanes(g, var)]) for var in range(2)], axis=0)
        return sw, sc

    def probabilities(b, g, sw, sc):
        ok_prev = kj >= (qi + jnp.where(first, BLOCK, 0) if b == 0 else qi)
        ok_next = kj <= (qi - jnp.where(last, BLOCK, 0) if b == n_qb - 1 else qi)
        pieces = [jnp.where(ok_prev, sw[:, 0:BLOCK], NEG_INF), sw[:, BLOCK:2 * BLOCK],
                  jnp.where(ok_next, sw[:, 2 * BLOCK:3 * BLOCK], NEG_INF), sc]
        es, rinv = _sink_softmax(pieces, _group_sinks(sink_ref, j, g, BLOCK))
        return jnp.concatenate(es[:3], axis=-1).astype(BF16), es[3].astype(BF16), rinv

    def weighted_values(b, g, ew, ec, rinv):
        rows = slice(b * BLOCK, (b + 1) * BLOCK)
        win = slice(b * BLOCK, (b + 3) * BLOCK)
        pv = []
        for var in range(2):
            sub = slice(var * 2 * BLOCK, (var + 1) * 2 * BLOCK)
            lanes = _head_lanes(g, var)
            pv.append((_dot(ew[sub, :], vall_ref[win, lanes]) + _dot(ec[sub, :], cv_ref[:, lanes]))
                      * rinv[sub, :])
        _store_group(a_ref, rows, g, pv, BLOCK)

    todo = [(b, g) for b in range(n_qb) for g in range(N_KV_HEADS)]
    ahead = scores(*todo[0])
    behind = None
    for k, (b, g) in enumerate(todo):
        sw, sc = ahead
        if k + 1 < len(todo):
            ahead = scores(*todo[k + 1])
        probs = probabilities(b, g, sw, sc)
        if behind is not None:
            weighted_values(*behind)
        behind = (b, g) + probs
    weighted_values(*behind)
    y = _dot(a_ref[...], wo_ref[...])
    o_ref[...] = x_ref[...] + _mod_row(mod_ref, 1 + i // geo.bps, 2) * y


def _smp_attn_call(x, layer, j, mods, sink, q, kx, vx, cache_k, cache_v, w_o, sides=()):
    geo = ROWS_NARROW
    prev, nxt = geo.halo_specs(KV_EXT, BLOCK, offset=geo.ncb)
    ctx_spec = pl.BlockSpec((None, None, PAST_LEN, KV_DIM), lambda i: (i // geo.bps, j, 0, 0))
    kwin = geo.tm + 2 * BLOCK
    return _hosted_call(
        functools.partial(_smp_attn_kernel, geo, j), geo.nb - geo.ncb,
        in_specs=[
            pl.BlockSpec(memory_space=pltpu.SMEM),
            geo.row_spec(D_MODEL, geo.ncb),
            _mod_spec(layer),
            geo.row_spec(Q_DIM, geo.ncb),
            geo.row_spec(KV_EXT, geo.ncb), prev, nxt,
            geo.row_spec(KV_EXT, geo.ncb), prev, nxt,
            ctx_spec, ctx_spec,
            _resident((Q_DIM, D_MODEL)),
        ],
        out_specs=[geo.row_spec(D_MODEL, geo.ncb)],
        out_shape=[jax.ShapeDtypeStruct((N_ROWS, D_MODEL), F32)],
        scratch_shapes=[pltpu.VMEM((geo.tm, Q_DIM), BF16), pltpu.VMEM((kwin, KV_EXT), BF16),
                        pltpu.VMEM((kwin, KV_EXT), BF16), pltpu.VMEM((PAST_LEN, KV_EXT), BF16),
                        pltpu.VMEM((PAST_LEN, KV_EXT), BF16)],
        sides=sides, name="smp_attn", aliases={1: 0},
    )(sink, x, mods, q, kx, kx, kx, vx, vx, vx, cache_k, cache_v, w_o, *[s.src for s in sides])


N_FT = D_FF // TN


def _ffn_kernel(geo, split_out, layer, x_ref, xp_ref, xn_ref, g_ref, mod_ref, win_ref, cw_ref, cb_ref, wo_ref, *refs):
    i = pl.program_id(0)
    tm = geo.tm
    ext = tm + 2 * HALO
    hext_ref, gbuf_ref = refs[-2:]
    cond = geo.cond_idx(i)
    g = g_ref[layer:layer + 1, :]
    sh = _mod_row(mod_ref, cond, 3)
    sc = _mod_row(mod_ref, cond, 4)
    x = x_ref[...]
    hext_ref[0:tm, :] = _norm_mod(x, g, sc, sh).astype(BF16)
    hext_ref[tm:tm + HALO, :] = _norm_mod(xn_ref[...], g, sc, sh).astype(BF16)
    hext_ref[tm + HALO:ext, :] = _norm_mod(xp_ref[...], g, sc, sh).astype(BF16)
    seq = geo.seq_len(i)
    pos = (i * tm + lax.broadcasted_iota(jnp.int32, (tm, TN), 0)) & (seq - 1)
    has_prev = pos != 0
    has_next = pos != seq - 1
    for j in range(N_FT):
        sl = slice(j * TN, (j + 1) * TN)
        a = _dot(hext_ref[...], win_ref[:, sl])
        u = _dot(hext_ref[0:tm, :], win_ref[:, D_FF + j * TN:D_FF + (j + 1) * TN])
        a_prev = jnp.where(has_prev, pltpu.roll(a, 1, 0)[:tm, :], 0.0)
        a_next = jnp.where(has_next, pltpu.roll(a, ext - 1, 0)[:tm, :], 0.0)
        a = (a_prev * cw_ref[0:1, sl] + a[:tm, :] * cw_ref[1:2, sl] + a_next * cw_ref[2:3, sl]
             + cb_ref[layer:layer + 1, sl])
        gbuf_ref[:, sl] = (_gelu(a) * u).astype(BF16)
    y = _dot(gbuf_ref[...], wo_ref[...])
    res = x + _mod_row(mod_ref, cond, 5) * y
    if split_out:
        oc_ref, os_ref = refs[:2]

        @pl.when(i < geo.ncb)
        def _():
            oc_ref[...] = res

        @pl.when(i >= geo.ncb)
        def _():
            os_ref[...] = res
    else:
        refs[0][...] = res


def _ffn_call(x, layer, mods, g_ffn, w_in, conv_w, conv_b, w_out, split_out, sides=()):
    geo = ROWS_WIDE
    prev, nxt = geo.halo_specs(D_MODEL, HALO)
    if split_out:
        out_specs = [geo.ctx_rows_spec(D_MODEL), geo.smp_rows_spec(D_MODEL)]
        out_shape = [jax.ShapeDtypeStruct((N_CTX_ROWS, D_MODEL), F32),
                     jax.ShapeDtypeStruct((N_SMP_ROWS, D_MODEL), F32)]
    else:
        out_specs = [geo.row_spec(D_MODEL)]
        out_shape = [jax.ShapeDtypeStruct((N_ROWS, D_MODEL), F32)]
    return _hosted_call(
        functools.partial(_ffn_kernel, geo, split_out, layer), geo.nb,
        in_specs=[
            geo.row_spec(D_MODEL), prev, nxt,
            _resident((DEPTH, D_MODEL)),
            _mod_spec(layer),
            _resident((D_MODEL, 2 * D_FF)),
            _resident((CONV_W, D_FF), layer),
            _resident((DEPTH, D_FF)),
            _resident((D_FF, D_MODEL)),
        ],
        out_specs=out_specs,
        out_shape=out_shape,
        scratch_shapes=[pltpu.VMEM((geo.tm + 2 * HALO, D_MODEL), BF16), pltpu.VMEM((geo.tm, D_FF), BF16)],
        sides=sides, name="ffn",
    )(x, x, x, g_ffn, mods, w_in, conv_w, conv_b, w_out, *[s.src for s in sides])


def kernel(x_prompt, x_sample, cache_k, cache_v, c, c_ctx, w_ada, b_ada, g_mix, g_ffn, w_ffn_in, ffn_conv_w,
           ffn_conv_b, w_ffn_out, a_w_in, a_g_v, a_w_s, a_b_s, a_w_out, p_w, p_b, p_scale, c_w_qkv, c_g_q,
           c_g_k, c_sink, c_w_o):
    x = (x_prompt.reshape(N_CTX_ROWS, D_MODEL), x_sample.reshape(N_SMP_ROWS, D_MODEL))

    cond8 = jnp.concatenate([c_ctx[None, :], c, jnp.zeros((SUBLANES - N_COND, D_MODEL), F32)], axis=0)
    mods = _ada_call(cond8, w_ada, b_ada)

    a_w_s, a_b_s, p_w = a_w_s.astype(BF16), a_b_s[:, :, :, None], p_w.astype(BF16)
    gq = jnp.tile(c_g_q, (1, TN // HEAD_DIM))
    gk = jnp.tile(c_g_k, (1, TN // HEAD_DIM))
    bd = jnp.asarray(np.kron(np.eye(TN // HEAD_DIM), np.full((HEAD_DIM, HEAD_DIM), 1.0 / HEAD_DIM)), dtype=BF16)
    sink = c_sink.reshape(N_C_LAYERS * N_HEADS)
    cache_k = cache_k.reshape(DEC_BATCH, N_C_LAYERS, PAST_LEN, KV_DIM)
    cache_v = cache_v.reshape(DEC_BATCH, N_C_LAYERS, PAST_LEN, KV_DIM)

    def ffn_sides(layer, n_in, axis_in, n_out):
        return (_Side(w_ffn_in, layer, axis_in, n_in), _Side(w_ffn_out, layer, 0, n_out))

    assert DEPTH == 4 and N_MIXERS == 3, "the hosting plan below is written for gMLP, pool, attention, gMLP"
    gw = (a_w_in[0].astype(BF16), a_w_out[0].astype(BF16))
    x, *fw = _gmlp_call(x, 0, 0, mods, g_mix, gw[0], a_g_v, a_w_s, a_b_s, gw[1],
                        sides=ffn_sides(0, 2 * N_FT, 1, 2 * N_FT))
    x, *fw = _ffn_call(x, 0, mods, g_ffn, fw[0], ffn_conv_w, ffn_conv_b, fw[1], False,
                       sides=ffn_sides(1, N_FT, 1, N_FT))
    x, w_qkv, w_o, *fw_last = _pool_call(
        x, 1, 0, mods, g_mix, p_w, p_b, p_scale,
        sides=(_Side(c_w_qkv, 0, 1, 12), _Side(c_w_o, 0, 0, 8)) + ffn_sides(3, N_FT, 1, N_FT))
    x, *fw = _ffn_call(x, 1, mods, g_ffn, fw[0], ffn_conv_w, ffn_conv_b, fw[1], False,
                       sides=ffn_sides(2, N_FT, 1, N_FT))
    q, kx, vx, kn, v = _qkv_call(x, 2, 0, mods, g_mix, w_qkv, gq, gk, bd)
    new_k = kn.reshape(BATCH, 1, SEQ, N_KV_HEADS, HEAD_DIM)
    new_v = v.reshape(BATCH, 1, SEQ, N_KV_HEADS, HEAD_DIM)
    (x,) = _ctx_attn_call(x, 2, 0, mods, sink, q, kx, vx, w_o)
    (x,) = _smp_attn_call(x, 2, 0, mods, sink, q, kx, vx, cache_k, cache_v, w_o)
    x, *gw = _ffn_call(x, 2, mods, g_ffn, fw[0], ffn_conv_w, ffn_conv_b, fw[1], False,
                       sides=(_Side(a_w_in, 1, 1, 8), _Side(a_w_out, 1, 0, 8)))
    (x,) = _gmlp_call(x, 3, 1, mods, g_mix, gw[0], a_g_v, a_w_s, a_b_s, gw[1])
    x = _ffn_call(x, 3, mods, g_ffn, fw_last[0], ffn_conv_w, ffn_conv_b, fw_last[1], True)

    y_prompt = x[0].reshape(BATCH, SEQ, D_MODEL)
    y_sample = x[1].reshape(DEC_BATCH, DEC_SEQ, D_MODEL)
    return (y_prompt, y_sample, new_k, new_v)
```
